```python
import jax, jax.numpy as jnp
from jax import lax
import numpy as np

D_MODEL = 1024
BATCH = 8
SEQ = 2048
DEPTH = 4
DEC_BATCH = 32
DEC_SEQ = 1
PAST_LEN = 8192
PAGE_SIZE = 128

N_EVEN = (DEPTH + 1) // 2
N_ODD = DEPTH // 2
MIX_WIDTH = D_MODEL
CONV_CH = MIX_WIDTH // 2
CONV_GROUPS = 8
CONV_W = 3
SGU_CH = MIX_WIDTH // 2
SGU_GROUPS = 4
SGU_GROUP_CH = SGU_CH // SGU_GROUPS
CHUNK = 128
N_HEADS = 16
HEAD_DIM = MIX_WIDTH // N_HEADS
N_KV_HEADS = 4
GROUP = N_HEADS // N_KV_HEADS
ROPE_DIM = HEAD_DIM // 4
ROPE_THETA = 500000.0
IDX_HEADS = 8
IDX_DIM = 64
IDX_ROPE_DIM = IDX_DIM // 4
TOPK_MAX = 256
Q_BLOCK = 64
D_FF = 256 * ((8 * D_MODEL // 3 + 255) // 256)
N_EXPERTS = 8
TOP_K = 2
D_EXPERT = D_MODEL
ALPHA = (2 * DEPTH) ** 0.25
BETA = (8 * DEPTH) ** -0.25
LN_EPS = 1e-5

EVEN_IN = 3 * CONV_CH + 2 * SGU_CH
EVEN_SPLITS = [CONV_CH, 2 * CONV_CH, 3 * CONV_CH, 3 * CONV_CH + SGU_CH]
Q_END = N_HEADS * HEAD_DIM
K_END = Q_END + N_KV_HEADS * HEAD_DIM
V_END = K_END + N_KV_HEADS * HEAD_DIM
QI_END = V_END + IDX_HEADS * IDX_DIM
KI_END = QI_END + IDX_DIM
ODD_IN = KI_END + IDX_HEADS
ODD_SPLITS = [Q_END, K_END, V_END, QI_END, KI_END]

kernel_name = 'hybrid_conv_sgu_dsa_moe_decoder_step'


def layer_norm(x, g, b):
    xf = x.astype(jnp.float32)
    mu = jnp.mean(xf, axis=-1, keepdims=True)
    var = jnp.mean(jnp.square(xf - mu), axis=-1, keepdims=True)
    return ((xf - mu) * lax.rsqrt(var + LN_EPS) * g.astype(jnp.float32) + b.astype(jnp.float32)).astype(x.dtype)


def deepnorm(x, f, g, b):
    return layer_norm(ALPHA * x + f, g, b)


def rope_partial(x, pos, rot_dim):
    half = rot_dim // 2
    inv = ROPE_THETA ** (-jnp.arange(half, dtype=jnp.float32) / half)
    ang = pos.astype(jnp.float32)[:, None] * inv[None, :]
    cos = jnp.cos(ang)[:, None, :]
    sin = jnp.sin(ang)[:, None, :]
    xf = x.astype(jnp.float32)
    x1, x2, rest = xf[..., :half], xf[..., half:rot_dim], xf[..., rot_dim:]
    out = jnp.concatenate([x1 * cos - x2 * sin, x1 * sin + x2 * cos, rest], axis=-1)
    return out.astype(x.dtype)


def short_conv(z, prev, w):
    t = z.shape[1]
    ext = jnp.concatenate([prev.astype(z.dtype), z], axis=1)
    y = sum(w[j] * ext[:, j:j + t] for j in range(CONV_W))
    return y, ext[:, -(CONV_W - 1):]


def chunk_mix(v, w_s, b_s):
    bsz, t = v.shape[:2]
    pad = (-t) % CHUNK
    vp = jnp.pad(v, ((0, 0), (0, pad), (0, 0), (0, 0)))
    n = (t + pad) // CHUNK
    vc = vp.reshape(bsz, n, CHUNK, SGU_GROUPS, SGU_GROUP_CH)
    causal = jnp.tril(jnp.ones((CHUNK, CHUNK), dtype=bool))
    wm = jnp.where(causal[None], w_s, 0.0).astype(v.dtype)
    s = jnp.einsum('gij,bnjgc->bnigc', wm, vc) + b_s.T[None, None, :, :, None].astype(v.dtype)
    return s.reshape(bsz, n * CHUNK, SGU_GROUPS, SGU_GROUP_CH)[:, :t]


def even_mixer(x, conv_prev, w_in, conv_w, sgu_g, sgu_bn, sgu_w, sgu_b, w_out):
    bsz, t, _ = x.shape
    h = x @ w_in
    gate_b, gate_c, h_in, u, v = jnp.split(h, EVEN_SPLITS, axis=-1)
    conv_out, conv_state = short_conv(gate_c * h_in, conv_prev, conv_w)
    y_a = gate_b * conv_out
    v_n = layer_norm(v, sgu_g, sgu_bn)
    s = chunk_mix(v_n.reshape(bsz, t, SGU_GROUPS, SGU_GROUP_CH), sgu_w, sgu_b).reshape(bsz, t, SGU_CH)
    y_b = u * s
    return jnp.concatenate([y_a, y_b], axis=-1) @ w_out, conv_state, v_n


def odd_project(x, w_in, pos):
    bsz, t, _ = x.shape
    h = x @ w_in
    q, k, v, qi, ki, wi = jnp.split(h, ODD_SPLITS, axis=-1)
    q = rope_partial(q.reshape(bsz, t, N_HEADS, HEAD_DIM), pos, ROPE_DIM)
    k = rope_partial(k.reshape(bsz, t, N_KV_HEADS, HEAD_DIM), pos, ROPE_DIM)
    v = v.reshape(bsz, t, N_KV_HEADS, HEAD_DIM)
    qi = rope_partial(qi.reshape(bsz, t, IDX_HEADS, IDX_DIM), pos, IDX_ROPE_DIM)
    ki = rope_partial(ki.reshape(bsz, t, 1, IDX_DIM), pos, IDX_ROPE_DIM)[:, :, 0]
    wi = wi.astype(jnp.float32) * (IDX_HEADS ** -0.5)
    return q, k, v, qi, ki, wi


def dsa_attend(q, qi, wi, qpos, kidx, gather_kv, topk):
    bsz, tq = q.shape[:2]
    n_keys = kidx.shape[1]
    dots = jnp.einsum('bthd,bsd->bths', qi.astype(jnp.float32), kidx.astype(jnp.float32)) * (IDX_DIM ** -0.5)
    score = jnp.einsum('bth,bths->bts', wi, jax.nn.relu(dots))
    admissible = jnp.arange(n_keys)[None, None, :] <= qpos[None, :, None]
    score = jnp.where(admissible, score, -jnp.inf)
    top_val, sel = lax.top_k(score, topk)
    valid = jnp.isfinite(top_val)
    k_sel, v_sel = gather_kv(sel)
    qg = q.reshape(bsz, tq, N_KV_HEADS, GROUP, HEAD_DIM).astype(jnp.float32)
    logits = jnp.einsum('btkgd,btskd->btkgs', qg, k_sel.astype(jnp.float32)) * (HEAD_DIM ** -0.5)
    logits = jnp.where(valid[:, :, None, None, :], logits, -jnp.inf)
    p = jax.nn.softmax(logits, axis=-1)
    out = jnp.einsum('btkgs,btskd->btkgd', p, v_sel.astype(jnp.float32))
    return out.reshape(bsz, tq, N_HEADS * HEAD_DIM).astype(q.dtype)


def dsa_prompt(q, k, v, qi, ki, wi):
    bsz, t = q.shape[:2]
    topk = min(TOPK_MAX, t // 4)
    bidx = jnp.arange(bsz)[:, None, None]

    def gather(sel):
        return k[bidx, sel], v[bidx, sel]

    def block(start):
        sl = lambda a: lax.dynamic_slice_in_dim(a, start, Q_BLOCK, axis=1)
        qpos = start + jnp.arange(Q_BLOCK)
        return dsa_attend(sl(q), sl(qi), sl(wi), qpos, ki, gather, topk)

    outs = lax.map(block, jnp.arange(0, t, Q_BLOCK))
    return jnp.moveaxis(outs, 0, 1).reshape(bsz, t, N_HEADS * HEAD_DIM)


def dsa_sample(q, k, v, qi, ki, wi, ck, cv, cki, page_table):
    bsz, t = q.shape[:2]
    past = page_table.shape[1] * PAGE_SIZE
    topk = min(TOPK_MAX, (past + t) // 4)
    ki_past = cki[page_table].reshape(bsz, past, IDX_DIM)
    ki_all = jnp.concatenate([ki_past.astype(ki.dtype), ki], axis=1)
    bidx = jnp.arange(bsz)[:, None, None]

    def gather(sel):
        in_past = (sel < past)[..., None, None]
        sp = jnp.minimum(sel, past - 1)
        phys = page_table[bidx, sp // PAGE_SIZE]
        slot = sp % PAGE_SIZE
        sn = jnp.clip(sel - past, 0, t - 1)
        k_g = jnp.where(in_past, ck[phys, slot].astype(k.dtype), k[bidx, sn])
        v_g = jnp.where(in_past, cv[phys, slot].astype(v.dtype), v[bidx, sn])
        return k_g, v_g

    qpos = past + jnp.arange(t)
    return dsa_attend(q, qi, wi, qpos, ki_all, gather, topk)


def swiglu(x, w1, w3, w2):
    return (jax.nn.silu(x @ w1) * (x @ w3)) @ w2


def moe_swiglu(x, wr, br, w1, w3, w2):
    logits = (x @ wr).astype(jnp.float32) + br.astype(jnp.float32)
    top_val, top_idx = lax.top_k(logits, TOP_K)
    gates = jax.nn.softmax(top_val, axis=-1)
    combine = jnp.einsum('btk,btke->bte', gates, jax.nn.one_hot(top_idx, N_EXPERTS, dtype=jnp.float32))
    y = jnp.zeros_like(x)
    for e in range(N_EXPERTS):
        h = jax.nn.silu(x @ w1[e]) * (x @ w3[e])
        y = y + (combine[..., e:e + 1].astype(x.dtype) * h) @ w2[e]
    return y


def setup_inputs(seed: int = 0) -> dict:
    key = jax.random.key(seed)
    ks = jax.random.split(key, 32)
    nrm = lambda k, shape, s: jax.random.normal(k, shape, jnp.float32) * s
    n_pages = PAST_LEN // PAGE_SIZE
    used = DEC_BATCH * n_pages
    n_pool = used + max(1, used // 4)
    page_table = jax.random.permutation(ks[6], n_pool)[:used].reshape(DEC_BATCH, n_pages).astype(jnp.int32)
    return {
        'x_prompt': nrm(ks[0], (BATCH, SEQ, D_MODEL), 1.0),
        'x_sample': nrm(ks[1], (DEC_BATCH, DEC_SEQ, D_MODEL), 1.0),
        'cache_k': nrm(ks[2], (N_ODD, n_pool, PAGE_SIZE, N_KV_HEADS, HEAD_DIM), 1.0),
        'cache_v': nrm(ks[3], (N_ODD, n_pool, PAGE_SIZE, N_KV_HEADS, HEAD_DIM), 1.0),
        'cache_kidx': nrm(ks[4], (N_ODD, n_pool, PAGE_SIZE, IDX_DIM), 1.0),
        'state_conv': nrm(ks[5], (N_EVEN, DEC_BATCH, CONV_W - 1, CONV_CH), 1.0),
        'page_table': page_table,
        'ln1_g': 1.0 + nrm(ks[7], (DEPTH, D_MODEL), 0.05),
        'ln1_b': nrm(ks[8], (DEPTH, D_MODEL), 0.02),
        'ln2_g': 1.0 + nrm(ks[9], (DEPTH, D_MODEL), 0.05),
        'ln2_b': nrm(ks[10], (DEPTH, D_MODEL), 0.02),
        'w_in_even': nrm(ks[11], (N_EVEN, D_MODEL, EVEN_IN), D_MODEL ** -0.5),
        'conv_w': nrm(ks[12], (N_EVEN, CONV_W, CONV_CH), CONV_W ** -0.5),
        'sgu_ln_g': 1.0 + nrm(ks[13], (N_EVEN, SGU_CH), 0.05),
        'sgu_ln_b': nrm(ks[14], (N_EVEN, SGU_CH), 0.02),
        'sgu_w': nrm(ks[15], (N_EVEN, SGU_GROUPS, CHUNK, CHUNK), CHUNK ** -0.5),
        'sgu_b': 1.0 + nrm(ks[16], (N_EVEN, SGU_GROUPS, CHUNK), 0.1),
        'w_out_even': nrm(ks[17], (N_EVEN, MIX_WIDTH, D_MODEL), BETA * MIX_WIDTH ** -0.5),
        'ffn_w1': nrm(ks[18], (N_EVEN, D_MODEL, D_FF), D_MODEL ** -0.5),
        'ffn_w3': nrm(ks[19], (N_EVEN, D_MODEL, D_FF), D_MODEL ** -0.5),
        'ffn_w2': nrm(ks[20], (N_EVEN, D_FF, D_MODEL), BETA * D_FF ** -0.5),
        'w_in_odd': nrm(ks[21], (N_ODD, D_MODEL, ODD_IN), D_MODEL ** -0.5),
        'w_out_odd': nrm(ks[22], (N_ODD, N_HEADS * HEAD_DIM, D_MODEL), BETA * (N_HEADS * HEAD_DIM) ** -0.5),
        'router_w': nrm(ks[23], (N_ODD, D_MODEL, N_EXPERTS), D_MODEL ** -0.5),
        'router_b': nrm(ks[24], (N_ODD, N_EXPERTS), 0.01),
        'moe_w1': nrm(ks[25], (N_ODD, N_EXPERTS, D_MODEL, D_EXPERT), D_MODEL ** -0.5),
        'moe_w3': nrm(ks[26], (N_ODD, N_EXPERTS, D_MODEL, D_EXPERT), D_MODEL ** -0.5),
        'moe_w2': nrm(ks[27], (N_ODD, N_EXPERTS, D_EXPERT, D_MODEL), BETA * D_EXPERT ** -0.5),
    }


def reference(x_prompt, x_sample, cache_k, cache_v, cache_kidx, state_conv, page_table,
              ln1_g, ln1_b, ln2_g, ln2_b, w_in_even, conv_w, sgu_ln_g, sgu_ln_b, sgu_w, sgu_b,
              w_out_even, ffn_w1, ffn_w3, ffn_w2, w_in_odd, w_out_odd, router_w, router_b,
              moe_w1, moe_w3, moe_w2):
    bp, tp, _ = x_prompt.shape
    past = page_table.shape[1] * PAGE_SIZE
    pos_p = jnp.arange(tp)
    pos_s = past + jnp.arange(x_sample.shape[1])
    xp, xs = x_prompt, x_sample
    k_p, v_p, ki_p, conv_p = [], [], [], []
    k_s, v_s, ki_s, conv_s, chunk_s = [], [], [], [], []
    for layer in range(DEPTH):
        i = layer // 2
        if layer % 2 == 0:
            ew = (w_in_even[i], conv_w[i], sgu_ln_g[i], sgu_ln_b[i], sgu_w[i], sgu_b[i], w_out_even[i])
            zero_prev = jnp.zeros((bp, CONV_W - 1, CONV_CH), xp.dtype)
            mp, cp, _ = even_mixer(xp, zero_prev, *ew)
            ms, cs, vs = even_mixer(xs, state_conv[i], *ew)
            conv_p.append(cp)
            conv_s.append(cs)
            chunk_s.append(vs)
            xp = deepnorm(xp, mp, ln1_g[layer], ln1_b[layer])
            xs = deepnorm(xs, ms, ln1_g[layer], ln1_b[layer])
            xp = deepnorm(xp, swiglu(xp, ffn_w1[i], ffn_w3[i], ffn_w2[i]), ln2_g[layer], ln2_b[layer])
            xs = deepnorm(xs, swiglu(xs, ffn_w1[i], ffn_w3[i], ffn_w2[i]), ln2_g[layer], ln2_b[layer])
        else:
            qp, kp, vp, qip, kip, wip = odd_project(xp, w_in_odd[i], pos_p)
            ap = dsa_prompt(qp, kp, vp, qip, kip, wip) @ w_out_odd[i]
            qs, kn, vn, qis, kis, wis = odd_project(xs, w_in_odd[i], pos_s)
            a_s = dsa_sample(qs, kn, vn, qis, kis, wis, cache_k[i], cache_v[i], cache_kidx[i], page_table) @ w_out_odd[i]
            k_p.append(kp)
            v_p.append(vp)
            ki_p.append(kip)
            k_s.append(kn)
            v_s.append(vn)
            ki_s.append(kis)
            xp = deepnorm(xp, ap, ln1_g[layer], ln1_b[layer])
            xs = deepnorm(xs, a_s, ln1_g[layer], ln1_b[layer])
            mw = (router_w[i], router_b[i], moe_w1[i], moe_w3[i], moe_w2[i])
            xp = deepnorm(xp, moe_swiglu(xp, *mw), ln2_g[layer], ln2_b[layer])
            xs = deepnorm(xs, moe_swiglu(xs, *mw), ln2_g[layer], ln2_b[layer])
    y_prompt, y_sample = xp, xs
    k_prompt, v_prompt, kidx_prompt = jnp.stack(k_p), jnp.stack(v_p), jnp.stack(ki_p)
    conv_prompt = jnp.stack(conv_p)
    k_sample, v_sample, kidx_sample = jnp.stack(k_s), jnp.stack(v_s), jnp.stack(ki_s)
    conv_sample = jnp.stack(conv_s)
    chunk_v_sample = jnp.stack(chunk_s)
    return (y_prompt, y_sample, k_prompt, v_prompt, kidx_prompt, conv_prompt,
            k_sample, v_sample, kidx_sample, conv_sample, chunk_v_sample)
```

```python
import functools

import jax
import jax.numpy as jnp
import numpy as np
from jax import lax
from jax.experimental import pallas as pl
from jax.experimental.pallas import tpu as pltpu

D_MODEL = 1024
DEPTH = 4
PAGE_SIZE = 128
CONV_CH = 512
CONV_W = 3
SGU_CH = 512
SGU_GROUPS = 4
CHUNK = 128
N_HEADS = 16
HEAD_DIM = 64
N_KV_HEADS = 4
GROUP = N_HEADS // N_KV_HEADS
KV_DIM = N_KV_HEADS * HEAD_DIM
ROPE_DIM = 16
ROPE_THETA = 500000.0
IDX_HEADS = 8
IDX_DIM = 64
TOPK_MAX = 256
D_FF = 2816
N_EXPERTS = 8
ALPHA = (2 * DEPTH) ** 0.25
LN_EPS = 1e-5

EVEN_IN = 3 * CONV_CH + 2 * SGU_CH
Q_END = N_HEADS * HEAD_DIM
K_END = Q_END + KV_DIM
V_END = K_END + KV_DIM
QI_END = V_END + IDX_HEADS * IDX_DIM
ODD_IN = QI_END + IDX_DIM + IDX_HEADS
ODD_IN_PAD = QI_END + 128

LANES = 128
VMEM_LIMIT = 56 * 1024 * 1024
KEY_NEG_INF = np.int32(-2139095041)
BF16 = jnp.bfloat16
F32 = jnp.float32


def _params(n_axes):
    return pltpu.CompilerParams(dimension_semantics=("arbitrary",) * n_axes, vmem_limit_bytes=VMEM_LIMIT)


def _const_spec(shape):
    nd = len(shape)
    return pl.BlockSpec(shape, lambda *_: (0,) * nd)


def _ln(x, g, b):
    mu = jnp.mean(x, axis=-1, keepdims=True)
    xc = x - mu
    var = jnp.mean(xc * xc, axis=-1, keepdims=True)
    return xc * lax.rsqrt(var + LN_EPS) * g + b


def _dot(a, b):
    return jnp.dot(a, b, preferred_element_type=F32)


def _dot_nt(a, b):
    return lax.dot_general(a, b, (((1,), (1,)), ((), ())), preferred_element_type=F32)


def _silu(x):
    return x * (1.0 / (1.0 + jnp.exp(-x)))


def _float_key(score):
    score = jnp.where(score == 0.0, 0.0, score)
    bits = pltpu.bitcast(score, jnp.int32)
    return jnp.where(bits < 0, bits ^ jnp.int32(0x7FFFFFFF), bits)


def _kth_largest_key(count_ge, rows, k):
    def body(i, lo):
        cand = lo + jnp.left_shift(jnp.int32(1), jnp.int32(31) - i)
        return jnp.where(count_ge(cand) >= k, cand, lo)
    lo0 = jnp.full((rows, 1), np.iinfo(np.int32).min, jnp.int32)
    return lax.fori_loop(0, 32, body, lo0)


def _tie_limit(count_lt, need, rows, n_bits):
    def body(i, p):
        cand = p + jnp.left_shift(jnp.int32(1), jnp.int32(n_bits - 1) - i)
        return jnp.where(count_lt(cand) < need, cand, p)
    return lax.fori_loop(0, n_bits, body, jnp.zeros((rows, 1), jnp.int32))


def _even_mixer_kernel(x_ref, win_ref, cw_ref, sg_ref, sb_ref, wm_ref, bs_ref, wout_ref, g_ref, b_ref,
                       o_ref, cs_ref, zbuf, ybuf, *, tm):
    t = pl.program_id(1)
    x = x_ref[...]
    h = _dot(x.astype(BF16), win_ref[...])
    gate_b = h[:, 0:CONV_CH]
    z = h[:, CONV_CH:2 * CONV_CH] * h[:, 2 * CONV_CH:3 * CONV_CH]
    u = h[:, 3 * CONV_CH:3 * CONV_CH + SGU_CH]
    v = h[:, 3 * CONV_CH + SGU_CH:]

    @pl.when(t == 0)
    def _():
        zbuf[0:8, :] = jnp.zeros((8, CONV_CH), F32)

    zbuf[8:8 + tm, :] = z
    conv = cw_ref[0:1, :] * zbuf[6:6 + tm, :] + cw_ref[1:2, :] * zbuf[7:7 + tm, :] + cw_ref[2:3, :] * z
    ybuf[:, 0:CONV_CH] = (gate_b * conv).astype(BF16)
    cs_ref[0] = zbuf[tm + 6:tm + 8, :]
    zbuf[0:8, :] = zbuf[tm:tm + 8, :]

    vn = _ln(v, sg_ref[...], sb_ref[...])
    row = lax.broadcasted_iota(jnp.int32, (CHUNK, CHUNK), 0)
    col = lax.broadcasted_iota(jnp.int32, (CHUNK, CHUNK), 1)
    for g in range(SGU_GROUPS):
        wm = jnp.where(row >= col, wm_ref[g], 0.0).astype(BF16)
        for c in range(tm // CHUNK):
            rs = slice(c * CHUNK, (c + 1) * CHUNK)
            cs = slice(g * CHUNK, (g + 1) * CHUNK)
            s = _dot(wm, vn[rs, cs].astype(BF16)) + bs_ref[g]
            ybuf[rs, CONV_CH + g * CHUNK:CONV_CH + (g + 1) * CHUNK] = (u[rs, cs] * s).astype(BF16)

    m = _dot(ybuf[...], wout_ref[...])
    o_ref[...] = _ln(ALPHA * x + m, g_ref[...], b_ref[...])


def _even_mixer(x, bsz, t, w_in, conv_w, sgu_g, sgu_bn, sgu_w, bs_b, w_out, g, b, tm=256):
    nt = t // tm
    kern = functools.partial(_even_mixer_kernel, tm=tm)
    return pl.pallas_call(
        kern,
        grid=(bsz, nt),
        in_specs=[
            pl.BlockSpec((tm, D_MODEL), lambda i, j: (i * nt + j, 0)),
            _const_spec((D_MODEL, EVEN_IN)),
            _const_spec((CONV_W, CONV_CH)),
            _const_spec((1, SGU_CH)),
            _const_spec((1, SGU_CH)),
            _const_spec((SGU_GROUPS, CHUNK, CHUNK)),
            _const_spec((SGU_GROUPS, CHUNK, CHUNK)),
            _const_spec((D_MODEL, D_MODEL)),
            _const_spec((1, D_MODEL)),
            _const_spec((1, D_MODEL)),
        ],
        out_specs=[
            pl.BlockSpec((tm, D_MODEL), lambda i, j: (i * nt + j, 0)),
            pl.BlockSpec((1, CONV_W - 1, CONV_CH), lambda i, j: (i, 0, 0)),
        ],
        out_shape=[
            jax.ShapeDtypeStruct((bsz * t, D_MODEL), F32),
            jax.ShapeDtypeStruct((bsz, CONV_W - 1, CONV_CH), F32),
        ],
        scratch_shapes=[pltpu.VMEM((tm + 8, CONV_CH), F32), pltpu.VMEM((tm, D_MODEL), BF16)],
        compiler_params=_params(2),
        name="even_mixer",
    )(x, w_in, conv_w, sgu_g, sgu_bn, sgu_w, bs_b, w_out, g, b)


def _even_mixer_step_kernel(x_ref, p0_ref, p1_ref, win_ref, cw_ref, sg_ref, sb_ref, wd_ref, b0_ref, wout_ref,
                            g_ref, b_ref, o_ref, z_ref, vn_ref):
    x = x_ref[...]
    h = _dot(x.astype(BF16), win_ref[...])
    gate_b = h[:, 0:CONV_CH]
    z = h[:, CONV_CH:2 * CONV_CH] * h[:, 2 * CONV_CH:3 * CONV_CH]
    u = h[:, 3 * CONV_CH:3 * CONV_CH + SGU_CH]
    v = h[:, 3 * CONV_CH + SGU_CH:]
    conv = cw_ref[0:1, :] * p0_ref[...] + cw_ref[1:2, :] * p1_ref[...] + cw_ref[2:3, :] * z
    vn = _ln(v, sg_ref[...], sb_ref[...])
    s = wd_ref[...].astype(BF16).astype(F32) * vn.astype(BF16).astype(F32) + b0_ref[...]
    y = jnp.concatenate([gate_b * conv, u * s], axis=-1).astype(BF16)
    m = _dot(y, wout_ref[...])
    o_ref[...] = _ln(ALPHA * x + m, g_ref[...], b_ref[...])
    z_ref[...] = z
    vn_ref[...] = vn


def _even_mixer_step(x, p0, p1, w_in, conv_w, sgu_g, sgu_bn, wd, b0, w_out, g, b):
    n = x.shape[0]
    shapes = [(n, D_MODEL), (n, CONV_CH), (n, CONV_CH), (D_MODEL, EVEN_IN), (CONV_W, CONV_CH), (1, SGU_CH),
              (1, SGU_CH), (1, SGU_CH), (1, SGU_CH), (D_MODEL, D_MODEL), (1, D_MODEL), (1, D_MODEL)]
    return pl.pallas_call(
        _even_mixer_step_kernel,
        grid=(1,),
        in_specs=[_const_spec(s) for s in shapes],
        out_specs=[_const_spec((n, D_MODEL)), _const_spec((n, CONV_CH)), _const_spec((n, SGU_CH))],
        out_shape=[jax.ShapeDtypeStruct((n, D_MODEL), F32), jax.ShapeDtypeStruct((n, CONV_CH), F32),
                   jax.ShapeDtypeStruct((n, SGU_CH), F32)],
        compiler_params=_params(1),
        name="even_mixer_step",
    )(x, p0, p1, w_in, conv_w, sgu_g, sgu_bn, wd, b0, w_out, g, b)


FF_TILE = 256


def _ffn_kernel(x_ref, w1_ref, w3_ref, w2_ref, g_ref, b_ref, o_ref, hbuf):
    x = x_ref[...]
    xb = x.astype(BF16)
    for f in range(D_FF // FF_TILE):
        cs = slice(f * FF_TILE, (f + 1) * FF_TILE)
        hbuf[:, cs] = (_silu(_dot(xb, w1_ref[:, cs])) * _dot(xb, w3_ref[:, cs])).astype(BF16)
    y = _dot(hbuf[...], w2_ref[...])
    o_ref[...] = _ln(ALPHA * x + y, g_ref[...], b_ref[...])


def _ffn(x, w1, w3, w2, g, b, tm):
    m = x.shape[0]
    return pl.pallas_call(
        _ffn_kernel,
        grid=(m // tm,),
        in_specs=[
            pl.BlockSpec((tm, D_MODEL), lambda i: (i, 0)),
            _const_spec((D_MODEL, D_FF)),
            _const_spec((D_MODEL, D_FF)),
            _const_spec((D_FF, D_MODEL)),
            _const_spec((1, D_MODEL)),
            _const_spec((1, D_MODEL)),
        ],
        out_specs=pl.BlockSpec((tm, D_MODEL), lambda i: (i, 0)),
        out_shape=jax.ShapeDtypeStruct((m, D_MODEL), F32),
        scratch_shapes=[pltpu.VMEM((tm, D_FF), BF16)],
        compiler_params=_params(1),
        name="ffn",
    )(x, w1, w3, w2, g, b)


def _rope(xc, c, sa, sb):
    return xc * c + pltpu.roll(xc, 8, 1) * sa + pltpu.roll(xc, LANES - 8, 1) * sb


def _odd_proj_kernel(x_ref, w_ref, tab_ref, q_ref, k_ref, v_ref, qi_ref, kw_ref):
    h = _dot(x_ref[...].astype(BF16), w_ref[...])
    c, sa, sb = tab_ref[:, 0:LANES], tab_ref[:, LANES:2 * LANES], tab_ref[:, 2 * LANES:3 * LANES]
    ch, sah, sbh = tab_ref[:, 3 * LANES:4 * LANES], tab_ref[:, 4 * LANES:5 * LANES], tab_ref[:, 5 * LANES:]
    for j in range(Q_END // LANES):
        q_ref[:, j * LANES:(j + 1) * LANES] = _rope(h[:, j * LANES:(j + 1) * LANES], c, sa, sb)
    for j in range(KV_DIM // LANES):
        lo = Q_END + j * LANES
        k_ref[:, j * LANES:(j + 1) * LANES] = _rope(h[:, lo:lo + LANES], c, sa, sb)
    v_ref[...] = h[:, K_END:V_END]
    for j in range(IDX_HEADS * IDX_DIM // LANES):
        lo = V_END + j * LANES
        qi_ref[:, j * LANES:(j + 1) * LANES] = _rope(h[:, lo:lo + LANES], c, sa, sb)
    lane = lax.broadcasted_iota(jnp.int32, (1, LANES), 1)
    scale = jnp.where(lane >= IDX_DIM, IDX_HEADS ** -0.5, 1.0).astype(F32)
    kw_ref[...] = _rope(h[:, QI_END:QI_END + LANES], ch, sah, sbh) * scale


def _odd_proj(x, w, tab, tm, n_tab_blocks):
    m = x.shape[0]
    widths = [Q_END, KV_DIM, KV_DIM, IDX_HEADS * IDX_DIM, LANES]
    return pl.pallas_call(
        _odd_proj_kernel,
        grid=(m // tm,),
        in_specs=[
            pl.BlockSpec((tm, D_MODEL), lambda i: (i, 0)),
            _const_spec((D_MODEL, ODD_IN_PAD)),
            pl.BlockSpec((tm, 6 * LANES), lambda i: (i % n_tab_blocks, 0)),
        ],
        out_specs=[pl.BlockSpec((tm, w_), lambda i: (i, 0)) for w_ in widths],
        out_shape=[jax.ShapeDtypeStruct((m, w_), F32) for w_ in widths],
        compiler_params=_params(1),
        name="odd_proj",
    )(x, w, tab)


def _rope_tables(pos):
    half = ROPE_DIM // 2
    inv = ROPE_THETA ** (-jnp.arange(half, dtype=F32) / half)
    ang = pos.astype(F32)[:, None] * inv[None, :]
    cos, sin = jnp.cos(ang), jnp.sin(ang)
    n = pos.shape[0]
    one = jnp.ones((n, HEAD_DIM - ROPE_DIM), F32)
    zero = jnp.zeros((n, HEAD_DIM - ROPE_DIM), F32)
    zh = jnp.zeros((n, half), F32)
    c64 = jnp.concatenate([cos, cos, one], axis=-1)
    sa64 = jnp.concatenate([zh, sin, zero], axis=-1)
    sb64 = jnp.concatenate([-sin, zh, zero], axis=-1)
    o64, z64 = jnp.ones((n, HEAD_DIM), F32), jnp.zeros((n, HEAD_DIM), F32)
    return jnp.concatenate([c64, c64, sa64, sa64, sb64, sb64, c64, o64, sa64, z64, sb64, z64], axis=-1)


def _dsa_prompt_kernel(x_ref, q_ref, qi_ref, kwq_ref, kwk_ref, k_ref, v_ref, wout_ref, g_ref, b_ref,
                       o_ref, key_ref, sel_ref, abuf, *, tq, n_keys, topk):
    t = pl.program_id(1)
    col = lax.broadcasted_iota(jnp.int32, (tq, n_keys), 1)
    qpos = t * tq + lax.broadcasted_iota(jnp.int32, (tq, n_keys), 0)
    adm = col <= qpos

    ki = kwk_ref[:, 0:IDX_DIM].astype(BF16)
    score = jnp.zeros((tq, n_keys), F32)
    for h in range(IDX_HEADS):
        d = _dot_nt(qi_ref[:, h * IDX_DIM:(h + 1) * IDX_DIM].astype(BF16), ki) * (IDX_DIM ** -0.5)
        score = score + kwq_ref[:, IDX_DIM + h:IDX_DIM + h + 1] * jnp.maximum(d, 0.0)
    key_ref[...] = _float_key(jnp.where(adm, score, -jnp.inf))

    def count_ge(cand):
        return jnp.sum((key_ref[...] >= cand).astype(F32), axis=-1, keepdims=True)

    tau = _kth_largest_key(count_ge, tq, float(topk))
    key = key_ref[...]
    gt = key > tau
    eq = key == tau
    need = float(topk) - jnp.sum(gt.astype(F32), axis=-1, keepdims=True)
    n_eq = jnp.sum(eq.astype(F32), axis=-1, keepdims=True)
    sel_ref[...] = jnp.where((gt | eq) & adm, 0.0, -jnp.inf)
    crowded = (n_eq > need) & (tau > KEY_NEG_INF)

    @pl.when(jnp.max(crowded.astype(F32)) > 0.0)
    def _():
        def count_lt(p):
            k_ = key_ref[...]
            c_ = lax.broadcasted_iota(jnp.int32, (tq, n_keys), 1)
            return jnp.sum(((k_ == tau) & (c_ < p)).astype(F32), axis=-1, keepdims=True)
        lim = _tie_limit(count_lt, need, tq, int(n_keys).bit_length())
        k_ = key_ref[...]
        c_ = lax.broadcasted_iota(jnp.int32, (tq, n_keys), 1)
        keep = (k_ > tau) | ((k_ == tau) & (c_ <= lim))
        keep = keep & (c_ <= t * tq + lax.broadcasted_iota(jnp.int32, (tq, n_keys), 0))
        sel_ref[...] = jnp.where(keep, 0.0, -jnp.inf)

    for h in range(N_HEADS):
        kv = h // GROUP
        qh = q_ref[:, h * HEAD_DIM:(h + 1) * HEAD_DIM].astype(BF16)
        kh = k_ref[:, kv * HEAD_DIM:(kv + 1) * HEAD_DIM].astype(BF16)
        vh = v_ref[:, kv * HEAD_DIM:(kv + 1) * HEAD_DIM].astype(BF16)
        logit = _dot_nt(qh, kh) * (HEAD_DIM ** -0.5) + sel_ref[...]
        p = jnp.exp(logit - jnp.max(logit, axis=-1, keepdims=True))
        den = jnp.sum(p, axis=-1, keepdims=True)
        abuf[:, h * HEAD_DIM:(h + 1) * HEAD_DIM] = (_dot(p.astype(BF16), vh) / den).astype(BF16)

    m = _dot(abuf[...], wout_ref[...])
    o_ref[...] = _ln(ALPHA * x_ref[...] + m, g_ref[...], b_ref[...])


def _dsa_prompt(x, q, qi, kw, k, v, w_out, g, b, bsz, t, tq=256):
    nq = t // tq
    topk = min(TOPK_MAX, t // 4)
    kern = functools.partial(_dsa_prompt_kernel, tq=tq, n_keys=t, topk=topk)
    qmap = lambda i, j: (i * nq + j, 0)
    bmap = lambda i, j: (i, 0)
    return pl.pallas_call(
        kern,
        grid=(bsz, nq),
        in_specs=[
            pl.BlockSpec((tq, D_MODEL), qmap),
            pl.BlockSpec((tq, Q_END), qmap),
            pl.BlockSpec((tq, IDX_HEADS * IDX_DIM), qmap),
            pl.BlockSpec((tq, LANES), qmap),
            pl.BlockSpec((t, LANES), bmap),
            pl.BlockSpec((t, KV_DIM), bmap),
            pl.BlockSpec((t, KV_DIM), bmap),
            _const_spec((Q_END, D_MODEL)),
            _const_spec((1, D_MODEL)),
            _const_spec((1, D_MODEL)),
        ],
        out_specs=pl.BlockSpec((tq, D_MODEL), qmap),
        out_shape=jax.ShapeDtypeStruct((bsz * t, D_MODEL), F32),
        scratch_shapes=[pltpu.VMEM((tq, t), jnp.int32), pltpu.VMEM((tq, t), F32), pltpu.VMEM((tq, Q_END), BF16)],
        compiler_params=_params(2),
        name="dsa_prompt",
    )(x, q, qi, kw, kw, k, v, w_out, g, b)


IDX_PAGES_PER_STEP = 16
KV_PAGES_PER_STEP = 8


def _dsa_step_score_kernel(pt_ref, qi_ref, wi_ref, kin_ref, *rest):
    pages = rest[:IDX_PAGES_PER_STEP]
    o_ref, on_ref, kbuf = rest[IDX_PAGES_PER_STEP:]
    for j, p in enumerate(pages):
        kbuf[j * PAGE_SIZE:(j + 1) * PAGE_SIZE, :] = p[0].astype(BF16)
    qi = qi_ref[0]
    wi = wi_ref[0]
    d = _dot_nt(qi.astype(BF16), kbuf[...]) * (IDX_DIM ** -0.5)
    o_ref[0] = jnp.sum(wi * jnp.maximum(d, 0.0), axis=0, keepdims=True)
    kin = kin_ref[0].astype(BF16).astype(F32)
    dn = jnp.sum(qi.astype(BF16).astype(F32) * kin, axis=-1, keepdims=True) * (IDX_DIM ** -0.5)
    sn = jnp.sum(wi * jnp.maximum(dn, 0.0), axis=0, keepdims=True)
    on_ref[0] = jnp.broadcast_to(sn, (1, LANES))


def _dsa_step_score(page_table, qi3, wi3, kin3, cache_kidx):
    n, n_pages = page_table.shape
    steps = n_pages // IDX_PAGES_PER_STEP
    page_specs = [
        pl.BlockSpec((1, PAGE_SIZE, IDX_DIM),
                     functools.partial(lambda i, s, pt, j: (pt[i, s * IDX_PAGES_PER_STEP + j], 0, 0), j=j))
        for j in range(IDX_PAGES_PER_STEP)
    ]
    grid_spec = pltpu.PrefetchScalarGridSpec(
        num_scalar_prefetch=1,
        grid=(n, steps),
        in_specs=[
            pl.BlockSpec((1, IDX_HEADS, IDX_DIM), lambda i, s, pt: (i, 0, 0)),
            pl.BlockSpec((1, IDX_HEADS, 1), lambda i, s, pt: (i, 0, 0)),
            pl.BlockSpec((1, 1, IDX_DIM), lambda i, s, pt: (i, 0, 0)),
        ] + page_specs,
        out_specs=[
            pl.BlockSpec((1, 1, IDX_PAGES_PER_STEP * PAGE_SIZE), lambda i, s, pt: (i, 0, s)),
            pl.BlockSpec((1, 1, LANES), lambda i, s, pt: (i, 0, 0)),
        ],
        scratch_shapes=[pltpu.VMEM((IDX_PAGES_PER_STEP * PAGE_SIZE, IDX_DIM), BF16)],
    )
    return pl.pallas_call(
        _dsa_step_score_kernel,
        grid_spec=grid_spec,
        out_shape=[jax.ShapeDtypeStruct((n, 1, n_pages * PAGE_SIZE), F32),
                   jax.ShapeDtypeStruct((n, 1, LANES), F32)],
        compiler_params=_params(2),
        name="dsa_step_score",
    )(page_table, qi3, wi3, kin3, *([cache_kidx] * IDX_PAGES_PER_STEP))


def _dsa_step_attend_kernel(pt_ref, sp_ref, sn_ref, qbd_ref, kn_ref, vn_ref, *rest, past, topk):
    kp = rest[:KV_PAGES_PER_STEP]
    vp = rest[KV_PAGES_PER_STEP:2 * KV_PAGES_PER_STEP]
    o_ref, bias_ref, biasn_ref, kbuf, vbuf, m_ref, l_ref, acc_ref = rest[2 * KV_PAGES_PER_STEP:]
    s = pl.program_id(1)
    n_steps = pl.num_programs(1)
    step_keys = KV_PAGES_PER_STEP * PAGE_SIZE

    @pl.when(s == 0)
    def _():
        lane = lax.broadcasted_iota(jnp.int32, (1, LANES), 1)
        key_p = _float_key(sp_ref[0])
        key_n = jnp.where(lane == 0, _float_key(sn_ref[0]), KEY_NEG_INF - 1)

        def count_ge(cand):
            return (jnp.sum((key_p >= cand).astype(F32), axis=-1, keepdims=True)
                    + jnp.sum((key_n >= cand).astype(F32), axis=-1, keepdims=True))

        tau = _kth_largest_key(count_ge, 1, float(topk))
        gt_p, eq_p = key_p > tau, key_p == tau
        need = (float(topk) - jnp.sum(gt_p.astype(F32), axis=-1, keepdims=True)
                - jnp.sum((key_n > tau).astype(F32), axis=-1, keepdims=True))
        idx = lax.broadcasted_iota(jnp.int32, (1, past), 1)

        def count_lt(p):
            return jnp.sum((eq_p & (idx < p)).astype(F32), axis=-1, keepdims=True)

        lim = _tie_limit(count_lt, need, 1, int(past).bit_length())
        keep_p = gt_p | (eq_p & (idx <= lim))
        n_keep = jnp.sum(keep_p.astype(F32), axis=-1, keepdims=True)
        keep_n = (key_n > tau) | ((key_n == tau) & (n_keep < float(topk)))
        bias_ref[...] = jnp.where(keep_p, 0.0, -jnp.inf)
        biasn_ref[...] = jnp.where(keep_n & (lane == 0), 0.0, -jnp.inf)
        m_ref[...] = jnp.full(m_ref.shape, -jnp.inf, F32)
        l_ref[...] = jnp.zeros(l_ref.shape, F32)
        acc_ref[...] = jnp.zeros(acc_ref.shape, F32)

    for j in range(KV_PAGES_PER_STEP):
        kbuf[j * PAGE_SIZE:(j + 1) * PAGE_SIZE, :] = kp[j][0].astype(BF16)
        vbuf[j * PAGE_SIZE:(j + 1) * PAGE_SIZE, :] = vp[j][0].astype(BF16)
    qbd = qbd_ref[0].astype(BF16)
    bias = bias_ref[:, pl.ds(pl.multiple_of(s * step_keys, step_keys), step_keys)]
    logit = _dot_nt(qbd, kbuf[...]) * (HEAD_DIM ** -0.5) + bias
    m_new = jnp.maximum(m_ref[...], jnp.max(logit, axis=-1, keepdims=True))
    m_safe = jnp.where(m_new == -jnp.inf, 0.0, m_new)
    scale = jnp.exp(m_ref[...] - m_safe)
    p = jnp.exp(logit - m_safe)
    l_ref[...] = l_ref[...] * scale + jnp.sum(p, axis=-1, keepdims=True)
    acc_ref[...] = acc_ref[...] * scale + _dot(p.astype(BF16), vbuf[...])
    m_ref[...] = m_new

    @pl.when(s == n_steps - 1)
    def _():
        kn = kn_ref[0].astype(BF16).astype(F32)
        vn = vn_ref[0].astype(BF16).astype(F32)
        ln_ = jnp.sum(qbd.astype(F32) * kn, axis=-1, keepdims=True) * (HEAD_DIM ** -0.5) + biasn_ref[:, 0:1]
        m_old = m_ref[...]
        m_new2 = jnp.maximum(m_old, ln_)
        sc = jnp.exp(m_old - m_new2)
        pn = jnp.exp(ln_ - m_new2)
        den = l_ref[...] * sc + pn
        r = (acc_ref[...] * sc + pn.astype(BF16).astype(F32) * vn) / den
        pieces = [r[h:h + 1, (h // GROUP) * HEAD_DIM:(h // GROUP + 1) * HEAD_DIM] for h in range(N_HEADS)]
        o_ref[0] = jnp.concatenate(pieces, axis=-1)


def _dsa_step_attend(page_table, sp, sn, qbd, kn3, vn3, cache_k, cache_v):
    n, n_pages = page_table.shape
    past = n_pages * PAGE_SIZE
    steps = n_pages // KV_PAGES_PER_STEP
    topk = min(TOPK_MAX, (past + 1) // 4)
    page_specs = [
        pl.BlockSpec((1, PAGE_SIZE, KV_DIM),
                     functools.partial(lambda i, s, pt, j: (pt[i, s * KV_PAGES_PER_STEP + j], 0, 0), j=j))
        for j in range(KV_PAGES_PER_STEP)
    ]
    row = lambda i, s, pt: (i, 0, 0)
    grid_spec = pltpu.PrefetchScalarGridSpec(
        num_scalar_prefetch=1,
        grid=(n, steps),
        in_specs=[
            pl.BlockSpec((1, 1, past), row),
            pl.BlockSpec((1, 1, LANES), row),
            pl.BlockSpec((1, N_HEADS, KV_DIM), row),
            pl.BlockSpec((1, 1, KV_DIM), row),
            pl.BlockSpec((1, 1, KV_DIM), row),
        ] + page_specs + page_specs,
        out_specs=pl.BlockSpec((1, 1, Q_END), row),
        scratch_shapes=[
            pltpu.VMEM((1, past), F32), pltpu.VMEM((1, LANES), F32),
            pltpu.VMEM((KV_PAGES_PER_STEP * PAGE_SIZE, KV_DIM), BF16),
            pltpu.VMEM((KV_PAGES_PER_STEP * PAGE_SIZE, KV_DIM), BF16),
            pltpu.VMEM((N_HEADS, 1), F32), pltpu.VMEM((N_HEADS, 1), F32), pltpu.VMEM((N_HEADS, KV_DIM), F32),
        ],
    )
    kern = functools.partial(_dsa_step_attend_kernel, past=past, topk=topk)
    return pl.pallas_call(
        kern,
        grid_spec=grid_spec,
        out_shape=jax.ShapeDtypeStruct((n, 1, Q_END), F32),
        compiler_params=_params(2),
        name="dsa_step_attend",
    )(page_table, sp, sn, qbd, kn3, vn3, *([cache_k] * KV_PAGES_PER_STEP), *([cache_v] * KV_PAGES_PER_STEP))


def _proj_ln_kernel(a_ref, x_ref, w_ref, g_ref, b_ref, o_ref):
    m = _dot(a_ref[...].astype(BF16), w_ref[...])
    o_ref[...] = _ln(ALPHA * x_ref[...] + m, g_ref[...], b_ref[...])


def _proj_ln(a, x, w, g, b):
    n = x.shape[0]
    shapes = [(n, a.shape[1]), (n, D_MODEL), w.shape, (1, D_MODEL), (1, D_MODEL)]
    return pl.pallas_call(
        _proj_ln_kernel,
        grid=(1,),
        in_specs=[_const_spec(s) for s in shapes],
        out_specs=_const_spec((n, D_MODEL)),
        out_shape=jax.ShapeDtypeStruct((n, D_MODEL), F32),
        compiler_params=_params(1),
        name="proj_ln",
    )(a, x, w, g, b)


def _moe_kernel(x_ref, wr_ref, br_ref, w1_ref, w3_ref, w2_ref, g_ref, b_ref, o_ref, acc, comb, xb):
    e = pl.program_id(1)
    tm = x_ref.shape[0]
    lane = lax.broadcasted_iota(jnp.int32, (tm, LANES), 1)

    @pl.when(e == 0)
    def _():
        x = x_ref[...]
        logits = jnp.dot(x, wr_ref[...], preferred_element_type=F32, precision=lax.Precision.HIGHEST)
        logits = jnp.where(lane < N_EXPERTS, logits + br_ref[...], -jnp.inf)
        lane_f = lane.astype(F32)
        m1 = jnp.max(logits, axis=-1, keepdims=True)
        i1 = jnp.min(jnp.where(logits == m1, lane_f, float(LANES)), axis=-1, keepdims=True)
        rest = jnp.where(lane_f == i1, -jnp.inf, logits)
        m2 = jnp.max(rest, axis=-1, keepdims=True)
        i2 = jnp.min(jnp.where(rest == m2, lane_f, float(LANES)), axis=-1, keepdims=True)
        e2 = jnp.exp(m2 - m1)
        den = 1.0 + e2
        comb[...] = jnp.where(lane_f == i1, 1.0 / den, 0.0) + jnp.where(lane_f == i2, e2 / den, 0.0)
        xb[...] = x.astype(BF16)
        acc[...] = jnp.zeros(acc.shape, F32)

    c = jnp.sum(jnp.where(lane == e, comb[...], 0.0), axis=-1, keepdims=True)
    h = _silu(_dot(xb[...], w1_ref[0])) * _dot(xb[...], w3_ref[0])
    acc[...] += _dot((c * h).astype(BF16), w2_ref[0])

    @pl.when(e == N_EXPERTS - 1)
    def _():
        o_ref[...] = _ln(ALPHA * x_ref[...] + acc[...], g_ref[...], b_ref[...])


def _moe(x, wr, br, w1, w3, w2, g, b, tm):
    m = x.shape[0]
    wspec = pl.BlockSpec((1, D_MODEL, D_MODEL), lambda i, e: (e, 0, 0))
    return pl.pallas_call(
        _moe_kernel,
        grid=(m // tm, N_EXPERTS),
        in_specs=[
            pl.BlockSpec((tm, D_MODEL), lambda i, e: (i, 0)),
            _const_spec((D_MODEL, LANES)),
            _const_spec((1, LANES)),
            wspec, wspec, wspec,
            _const_spec((1, D_MODEL)),
            _const_spec((1, D_MODEL)),
        ],
        out_specs=pl.BlockSpec((tm, D_MODEL), lambda i, e: (i, 0)),
        out_shape=jax.ShapeDtypeStruct((m, D_MODEL), F32),
        scratch_shapes=[pltpu.VMEM((tm, D_MODEL), F32), pltpu.VMEM((tm, LANES), F32),
                        pltpu.VMEM((tm, D_MODEL), BF16)],
        compiler_params=_params(2),
        name="moe",
    )(x, wr, br, w1, w3, w2, g, b)


def kernel(x_prompt, x_sample, cache_k, cache_v, cache_kidx, state_conv, page_table, ln1_g, ln1_b, ln2_g, ln2_b,
           w_in_even, conv_w, sgu_ln_g, sgu_ln_b, sgu_w, sgu_b, w_out_even, ffn_w1, ffn_w3, ffn_w2, w_in_odd,
           w_out_odd, router_w, router_b, moe_w1, moe_w3, moe_w2):
    bp, tp, _ = x_prompt.shape
    ns = x_sample.shape[0]
    n_pool = cache_k.shape[1]
    past = page_table.shape[1] * PAGE_SIZE

    xp = x_prompt.reshape(bp * tp, D_MODEL)
    xs = x_sample.reshape(ns, D_MODEL)
    tab_p = _rope_tables(jnp.arange(tp))
    tab_s = _rope_tables(jnp.full((ns,), past, jnp.int32))
    row = lambda a: a.reshape(1, -1)

    k_p, v_p, ki_p, conv_p = [], [], [], []
    k_s, v_s, ki_s, conv_s, chunk_s = [], [], [], [], []
    for layer in range(DEPTH):
        i = layer // 2
        g1, b1, g2, b2 = row(ln1_g[layer]), row(ln1_b[layer]), row(ln2_g[layer]), row(ln2_b[layer])
        if layer % 2 == 0:
            w_in = w_in_even[i].astype(BF16)
            w_out = w_out_even[i].astype(BF16)
            sg, sbn = row(sgu_ln_g[i]), row(sgu_ln_b[i])
            bs_b = jnp.broadcast_to(sgu_b[i][:, :, None], (SGU_GROUPS, CHUNK, CHUNK))
            xp, cp = _even_mixer(xp, bp, tp, w_in, conv_w[i], sg, sbn, sgu_w[i], bs_b, w_out, g1, b1)
            wd = row(jnp.repeat(sgu_w[i][:, 0, 0], CHUNK))
            b0 = row(jnp.repeat(sgu_b[i][:, 0], CHUNK))
            xs, zs, vs = _even_mixer_step(xs, state_conv[i][:, 0], state_conv[i][:, 1], w_in, conv_w[i], sg, sbn,
                                          wd, b0, w_out, g1, b1)
            conv_p.append(cp)
            conv_s.append(jnp.stack([state_conv[i][:, 1], zs], axis=1))
            chunk_s.append(vs.reshape(ns, 1, SGU_CH))
            w1, w3, w2 = ffn_w1[i].astype(BF16), ffn_w3[i].astype(BF16), ffn_w2[i].astype(BF16)
            xp = _ffn(xp, w1, w3, w2, g2, b2, tm=512)
            xs = _ffn(xs, w1, w3, w2, g2, b2, tm=ns)
        else:
            w_in = jnp.pad(w_in_odd[i], ((0, 0), (0, ODD_IN_PAD - ODD_IN))).astype(BF16)
            w_out = w_out_odd[i].astype(BF16)
            qp, kp, vp, qip, kwp = _odd_proj(xp, w_in, tab_p, tm=512, n_tab_blocks=tp // 512)
            xp = _dsa_prompt(xp, qp, qip, kwp, kp, vp, w_out, g1, b1, bp, tp)
            k_p.append(kp.reshape(bp, tp, N_KV_HEADS, HEAD_DIM))
            v_p.append(vp.reshape(bp, tp, N_KV_HEADS, HEAD_DIM))
            ki_p.append(kwp[:, :IDX_DIM].reshape(bp, tp, IDX_DIM))

            qs, kn, vn, qis, kws = _odd_proj(xs, w_in, tab_s, tm=ns, n_tab_blocks=1)
            kis = kws[:, :IDX_DIM]
            wis = kws[:, IDX_DIM:IDX_DIM + IDX_HEADS]
            sp, sn = _dsa_step_score(page_table, qis.reshape(ns, IDX_HEADS, IDX_DIM), wis.reshape(ns, IDX_HEADS, 1),
                                     kis.reshape(ns, 1, IDX_DIM), cache_kidx[i])
            head_kv = jnp.arange(N_HEADS) // GROUP
            qbd = (qs.reshape(ns, N_HEADS, 1, HEAD_DIM)
                   * (head_kv[:, None] == jnp.arange(N_KV_HEADS)[None, :]).astype(F32)[None, :, :, None])
            qbd = qbd.reshape(ns, N_HEADS, KV_DIM)
            a_s = _dsa_step_attend(page_table, sp, sn, qbd, kn.reshape(ns, 1, KV_DIM), vn.reshape(ns, 1, KV_DIM),
                                   cache_k[i].reshape(n_pool, PAGE_SIZE, KV_DIM),
                                   cache_v[i].reshape(n_pool, PAGE_SIZE, KV_DIM))
            xs = _proj_ln(a_s.reshape(ns, Q_END), xs, w_out, g1, b1)
            k_s.append(kn.reshape(ns, 1, N_KV_HEADS, HEAD_DIM))
            v_s.append(vn.reshape(ns, 1, N_KV_HEADS, HEAD_DIM))
            ki_s.append(kis.reshape(ns, 1, IDX_DIM))

            wr = jnp.pad(router_w[i], ((0, 0), (0, LANES - N_EXPERTS)))
            br = jnp.pad(row(router_b[i]), ((0, 0), (0, LANES - N_EXPERTS)))
            w1, w3, w2 = moe_w1[i].astype(BF16), moe_w3[i].astype(BF16), moe_w2[i].astype(BF16)
            xp = _moe(xp, wr, br, w1, w3, w2, g2, b2, tm=1024)
            xs = _moe(xs, wr, br, w1, w3, w2, g2, b2, tm=ns)

    return (xp.reshape(bp, tp, D_MODEL), xs.reshape(ns, 1, D_MODEL),
            jnp.stack(k_p), jnp.stack(v_p), jnp.stack(ki_p), jnp.stack(conv_p),
            jnp.stack(k_s), jnp.stack(v_s), jnp.stack(ki_s), jnp.stack(conv_s), jnp.stack(chunk_s))
```

```python
import functools

import jax
import jax.numpy as jnp
import numpy as np
from jax import lax
from jax.experimental import pallas as pl
from jax.experimental.pallas import tpu as pltpu

D_MODEL = 1024
DEPTH = 4
PAGE_SIZE = 128
CONV_CH = 512
CONV_W = 3
SGU_CH = 512
SGU_GROUPS = 4
CHUNK = 128
N_HEADS = 16
HEAD_DIM = 64
N_KV_HEADS = 4
GROUP = N_HEADS // N_KV_HEADS
KV_DIM = N_KV_HEADS * HEAD_DIM
ROPE_DIM = 16
ROPE_THETA = 500000.0
IDX_HEADS = 8
IDX_DIM = 64
TOPK_MAX = 256
D_FF = 2816
N_EXPERTS = 8
ALPHA = (2 * DEPTH) ** 0.25
LN_EPS = 1e-5

EVEN_IN = 3 * CONV_CH + 2 * SGU_CH
Q_END = N_HEADS * HEAD_DIM
K_END = Q_END + KV_DIM
V_END = K_END + KV_DIM
QI_END = V_END + IDX_HEADS * IDX_DIM

LANES = 128
VMEM_LIMIT = 56 * 1024 * 1024
KEY_NEG_INF = np.int32(-2139095041)
BF16 = jnp.bfloat16
F32 = jnp.float32


def _params(n_axes):
    return pltpu.CompilerParams(dimension_semantics=("arbitrary",) * n_axes, vmem_limit_bytes=VMEM_LIMIT)


def _const_spec(shape):
    nd = len(shape)
    return pl.BlockSpec(shape, lambda *_: (0,) * nd)


def _ln(x, g, b):
    mu = jnp.mean(x, axis=-1, keepdims=True)
    xc = x - mu
    var = jnp.mean(xc * xc, axis=-1, keepdims=True)
    return xc * lax.rsqrt(var + LN_EPS) * g + b


def _dot(a, b):
    return jnp.dot(a, b, preferred_element_type=F32)


def _dot_nt(a, b):
    return lax.dot_general(a, b, (((1,), (1,)), ((), ())), preferred_element_type=F32)


def _split(a):
    hi = a.astype(BF16)
    return hi, (a - hi.astype(F32)).astype(BF16)


def _dot3(a, b, dot=_dot):
    ah, al = _split(a)
    bh, bl = _split(b)
    return dot(ah, bh) + (dot(ah, bl) + dot(al, bh))


def _silu(x):
    return x * (1.0 / (1.0 + jnp.exp(-x)))


def _float_key(score):
    score = jnp.where(score == 0.0, 0.0, score)
    bits = pltpu.bitcast(score, jnp.int32)
    return jnp.where(bits < 0, bits ^ jnp.int32(0x7FFFFFFF), bits)


def _kth_largest_key(count_ge, rows, k):
    def body(i, lo):
        cand = lo + jnp.left_shift(jnp.int32(1), jnp.int32(31) - i)
        return jnp.where(count_ge(cand) >= k, cand, lo)
    lo0 = jnp.full((rows, 1), np.iinfo(np.int32).min, jnp.int32)
    return lax.fori_loop(0, 32, body, lo0)


def _tie_limit(count_lt, need, rows, n_bits):
    def body(i, p):
        cand = p + jnp.left_shift(jnp.int32(1), jnp.int32(n_bits - 1) - i)
        return jnp.where(count_lt(cand) < need, cand, p)
    return lax.fori_loop(0, n_bits, body, jnp.zeros((rows, 1), jnp.int32))


def _even_mixer_kernel(x_ref, win_ref, cw_ref, sg_ref, sb_ref, wm_ref, bs_ref, wout_ref, g_ref, b_ref,
                       o_ref, cs_ref, zbuf, ybuf, *, tm):
    t = pl.program_id(1)
    x = x_ref[...]
    h = _dot(x.astype(BF16), win_ref[...])
    gate_b = h[:, 0:CONV_CH]
    z = h[:, CONV_CH:2 * CONV_CH] * h[:, 2 * CONV_CH:3 * CONV_CH]
    u = h[:, 3 * CONV_CH:3 * CONV_CH + SGU_CH]
    v = h[:, 3 * CONV_CH + SGU_CH:]

    @pl.when(t == 0)
    def _():
        zbuf[0:8, :] = jnp.zeros((8, CONV_CH), F32)

    zbuf[8:8 + tm, :] = z
    conv = cw_ref[0:1, :] * zbuf[6:6 + tm, :] + cw_ref[1:2, :] * zbuf[7:7 + tm, :] + cw_ref[2:3, :] * z
    ybuf[:, 0:CONV_CH] = (gate_b * conv).astype(BF16)
    cs_ref[0] = zbuf[tm + 6:tm + 8, :]
    zbuf[0:8, :] = zbuf[tm:tm + 8, :]

    vn = _ln(v, sg_ref[...], sb_ref[...])
    row = lax.broadcasted_iota(jnp.int32, (CHUNK, CHUNK), 0)
    col = lax.broadcasted_iota(jnp.int32, (CHUNK, CHUNK), 1)
    for g in range(SGU_GROUPS):
        wm = jnp.where(row >= col, wm_ref[g], 0.0).astype(BF16)
        for c in range(tm // CHUNK):
            rs = slice(c * CHUNK, (c + 1) * CHUNK)
            cs = slice(g * CHUNK, (g + 1) * CHUNK)
            s = _dot(wm, vn[rs, cs].astype(BF16)) + bs_ref[g]
            ybuf[rs, CONV_CH + g * CHUNK:CONV_CH + (g + 1) * CHUNK] = (u[rs, cs] * s).astype(BF16)

    m = _dot(ybuf[...], wout_ref[...])
    o_ref[...] = _ln(ALPHA * x + m, g_ref[...], b_ref[...])


def _even_mixer(x, bsz, t, w_in, conv_w, sgu_g, sgu_bn, sgu_w, bs_b, w_out, g, b, tm=256):
    nt = t // tm
    kern = functools.partial(_even_mixer_kernel, tm=tm)
    return pl.pallas_call(
        kern,
        grid=(bsz, nt),
        in_specs=[
            pl.BlockSpec((tm, D_MODEL), lambda i, j: (i * nt + j, 0)),
            _const_spec((D_MODEL, EVEN_IN)),
            _const_spec((CONV_W, CONV_CH)),
            _const_spec((1, SGU_CH)),
            _const_spec((1, SGU_CH)),
            _const_spec((SGU_GROUPS, CHUNK, CHUNK)),
            _const_spec((SGU_GROUPS, CHUNK, CHUNK)),
            _const_spec((D_MODEL, D_MODEL)),
            _const_spec((1, D_MODEL)),
            _const_spec((1, D_MODEL)),
        ],
        out_specs=[
            pl.BlockSpec((tm, D_MODEL), lambda i, j: (i * nt + j, 0)),
            pl.BlockSpec((1, CONV_W - 1, CONV_CH), lambda i, j: (i, 0, 0)),
        ],
        out_shape=[
            jax.ShapeDtypeStruct((bsz * t, D_MODEL), F32),
            jax.ShapeDtypeStruct((bsz, CONV_W - 1, CONV_CH), F32),
        ],
        scratch_shapes=[pltpu.VMEM((tm + 8, CONV_CH), F32), pltpu.VMEM((tm, D_MODEL), BF16)],
        compiler_params=_params(2),
        name="even_mixer",
    )(x, w_in, conv_w, sgu_g, sgu_bn, sgu_w, bs_b, w_out, g, b)


def _even_mixer_step_kernel(x_ref, p0_ref, p1_ref, win_ref, cw_ref, sg_ref, sb_ref, wd_ref, b0_ref, wout_ref,
                            g_ref, b_ref, o_ref, z_ref, vn_ref):
    x = x_ref[...]
    gate_b, gate_c, h_in, u, v = [_dot3(x, win_ref[:, j * CONV_CH:(j + 1) * CONV_CH]) for j in range(5)]
    z = gate_c * h_in
    conv = cw_ref[0:1, :] * p0_ref[...] + cw_ref[1:2, :] * p1_ref[...] + cw_ref[2:3, :] * z
    vn = _ln(v, sg_ref[...], sb_ref[...])
    s = wd_ref[...] * vn + b0_ref[...]
    y = jnp.concatenate([gate_b * conv, u * s], axis=-1)
    m = _dot3(y, wout_ref[...])
    o_ref[...] = _ln(ALPHA * x + m, g_ref[...], b_ref[...])
    z_ref[...] = z
    vn_ref[...] = vn


def _even_mixer_step(x, p0, p1, w_in, conv_w, sgu_g, sgu_bn, wd, b0, w_out, g, b):
    n = x.shape[0]
    shapes = [(n, D_MODEL), (n, CONV_CH), (n, CONV_CH), (D_MODEL, EVEN_IN), (CONV_W, CONV_CH), (1, SGU_CH),
              (1, SGU_CH), (1, SGU_CH), (1, SGU_CH), (D_MODEL, D_MODEL), (1, D_MODEL), (1, D_MODEL)]
    return pl.pallas_call(
        _even_mixer_step_kernel,
        grid=(1,),
        in_specs=[_const_spec(s) for s in shapes],
        out_specs=[_const_spec((n, D_MODEL)), _const_spec((n, CONV_CH)), _const_spec((n, SGU_CH))],
        out_shape=[jax.ShapeDtypeStruct((n, D_MODEL), F32), jax.ShapeDtypeStruct((n, CONV_CH), F32),
                   jax.ShapeDtypeStruct((n, SGU_CH), F32)],
        compiler_params=_params(1),
        name="even_mixer_step",
    )(x, p0, p1, w_in, conv_w, sgu_g, sgu_bn, wd, b0, w_out, g, b)


FF_TILE = 256


def _ffn_kernel(x_ref, w1_ref, w3_ref, w2_ref, g_ref, b_ref, o_ref, hbuf):
    x = x_ref[...]
    xb = x.astype(BF16)
    for f in range(D_FF // FF_TILE):
        cs = slice(f * FF_TILE, (f + 1) * FF_TILE)
        hbuf[:, cs] = (_silu(_dot(xb, w1_ref[:, cs])) * _dot(xb, w3_ref[:, cs])).astype(BF16)
    y = _dot(hbuf[...], w2_ref[...])
    o_ref[...] = _ln(ALPHA * x + y, g_ref[...], b_ref[...])


def _ffn(x, w1, w3, w2, g, b, tm):
    m = x.shape[0]
    return pl.pallas_call(
        _ffn_kernel,
        grid=(m // tm,),
        in_specs=[
            pl.BlockSpec((tm, D_MODEL), lambda i: (i, 0)),
            _const_spec((D_MODEL, D_FF)),
            _const_spec((D_MODEL, D_FF)),
            _const_spec((D_FF, D_MODEL)),
            _const_spec((1, D_MODEL)),
            _const_spec((1, D_MODEL)),
        ],
        out_specs=pl.BlockSpec((tm, D_MODEL), lambda i: (i, 0)),
        out_shape=jax.ShapeDtypeStruct((m, D_MODEL), F32),
        scratch_shapes=[pltpu.VMEM((tm, D_FF), BF16)],
        compiler_params=_params(1),
        name="ffn",
    )(x, w1, w3, w2, g, b)


def _ffn_step_kernel(x_ref, w1_ref, w3_ref, w2_ref, g_ref, b_ref, o_ref, acc):
    f = pl.program_id(0)

    @pl.when(f == 0)
    def _():
        acc[...] = jnp.zeros(acc.shape, F32)

    x = x_ref[...]
    h = _silu(_dot3(x, w1_ref[...])) * _dot3(x, w3_ref[...])
    acc[...] += _dot3(h, w2_ref[...])

    @pl.when(f == pl.num_programs(0) - 1)
    def _():
        o_ref[...] = _ln(ALPHA * x + acc[...], g_ref[...], b_ref[...])


def _ffn_step(x, w1, w3, w2, g, b):
    n = x.shape[0]
    return pl.pallas_call(
        _ffn_step_kernel,
        grid=(D_FF // FF_TILE,),
        in_specs=[
            _const_spec((n, D_MODEL)),
            pl.BlockSpec((D_MODEL, FF_TILE), lambda f: (0, f)),
            pl.BlockSpec((D_MODEL, FF_TILE), lambda f: (0, f)),
            pl.BlockSpec((FF_TILE, D_MODEL), lambda f: (f, 0)),
            _const_spec((1, D_MODEL)),
            _const_spec((1, D_MODEL)),
        ],
        out_specs=_const_spec((n, D_MODEL)),
        out_shape=jax.ShapeDtypeStruct((n, D_MODEL), F32),
        scratch_shapes=[pltpu.VMEM((n, D_MODEL), F32)],
        compiler_params=_params(1),
        name="ffn_step",
    )(x, w1, w3, w2, g, b)


QQI = Q_END + IDX_HEADS * IDX_DIM
KVT_ROWS = 2 * KV_DIM + IDX_DIM


def _rope(xc, c, sa, sb):
    return xc * c + pltpu.roll(xc, 8, 1) * sa + pltpu.roll(xc, LANES - 8, 1) * sb


def _rope_rows(ht, r0, cos, sin, out_ref, o0, n_rows):
    x1, x2 = ht[r0:r0 + 8, :], ht[r0 + 8:r0 + 16, :]
    out_ref[o0:o0 + 8, :] = x1 * cos - x2 * sin
    out_ref[o0 + 8:o0 + 16, :] = x1 * sin + x2 * cos
    out_ref[o0 + 16:o0 + n_rows, :] = ht[r0 + 16:r0 + n_rows, :]


def _odd_proj_kernel(x_ref, wq_ref, wkt_ref, tab_ref, tabt_ref,
                     q_ref, qi_ref, wi_ref, kt_ref, vt_ref, kit_ref, ktb_ref, vtb_ref, kitb_ref, *, precise):
    if precise:
        x = x_ref[...]
        h = _dot3(x, wq_ref[...])
        ht = _dot3(wkt_ref[...], x, dot=_dot_nt)
    else:
        xb = x_ref[...].astype(BF16)
        h = _dot(xb, wq_ref[...])
        ht = _dot_nt(wkt_ref[...], xb)
    c, sa, sb = tab_ref[:, 0:LANES], tab_ref[:, LANES:2 * LANES], tab_ref[:, 2 * LANES:3 * LANES]
    for j in range(Q_END // LANES):
        cs = slice(j * LANES, (j + 1) * LANES)
        q_ref[:, cs] = (_rope(h[:, cs], c, sa, sb) * (HEAD_DIM ** -0.5)).astype(q_ref.dtype)
    for j in range(IDX_HEADS * IDX_DIM // LANES):
        cs = slice(j * LANES, (j + 1) * LANES)
        qi_ref[:, cs] = (_rope(h[:, Q_END + j * LANES:Q_END + (j + 1) * LANES], c, sa, sb)
                         * (IDX_DIM ** -0.5)).astype(qi_ref.dtype)
    wi_ref[...] = h[:, QQI:QQI + LANES] * (IDX_HEADS ** -0.5)

    cos, sin = tabt_ref[0:8, :], tabt_ref[8:16, :]
    for hd in range(N_KV_HEADS):
        _rope_rows(ht, hd * HEAD_DIM, cos, sin, kt_ref, hd * HEAD_DIM, HEAD_DIM)
    vt_ref[...] = ht[KV_DIM:2 * KV_DIM, :]
    _rope_rows(ht, 2 * KV_DIM, cos, sin, kit_ref, 0, IDX_DIM)
    ktb_ref[...] = kt_ref[...].astype(BF16)
    vtb_ref[...] = vt_ref[...].astype(BF16)
    kitb_ref[...] = kit_ref[...].astype(BF16)


def _odd_proj(x, wq, wkt, tab, tabt, bsz, t, tm, precise=False):
    nt = t // tm
    tok = lambda i: (i, 0)
    feat = lambda i: (i // nt, 0, i % nt)
    q_dtype = F32 if precise else BF16
    row_widths = [(Q_END, q_dtype), (IDX_HEADS * IDX_DIM, q_dtype), (LANES, F32)]
    feat_rows = [(KV_DIM, F32), (KV_DIM, F32), (IDX_DIM, F32), (KV_DIM, BF16), (KV_DIM, BF16), (IDX_DIM, BF16)]
    return pl.pallas_call(
        functools.partial(_odd_proj_kernel, precise=precise),
        grid=(bsz * nt,),
        in_specs=[
            pl.BlockSpec((tm, D_MODEL), tok),
            _const_spec((D_MODEL, QQI + LANES)),
            _const_spec((KVT_ROWS, D_MODEL)),
            pl.BlockSpec((tm, 3 * LANES), lambda i: (i % nt, 0)),
            pl.BlockSpec((ROPE_DIM, tm), lambda i: (0, i % nt)),
        ],
        out_specs=[pl.BlockSpec((tm, w_), tok) for w_, _ in row_widths]
        + [pl.BlockSpec((None, r_, tm), feat) for r_, _ in feat_rows],
        out_shape=[jax.ShapeDtypeStruct((bsz * t, w_), d_) for w_, d_ in row_widths]
        + [jax.ShapeDtypeStruct((bsz, r_, t), d_) for r_, d_ in feat_rows],
        compiler_params=_params(1),
        name="odd_proj",
    )(x, wq, wkt, tab, tabt)


def _rope_tables(pos):
    half = ROPE_DIM // 2
    inv = ROPE_THETA ** (-jnp.arange(half, dtype=F32) / half)
    ang = pos.astype(F32)[:, None] * inv[None, :]
    cos, sin = jnp.cos(ang), jnp.sin(ang)
    n = pos.shape[0]
    one = jnp.ones((n, HEAD_DIM - ROPE_DIM), F32)
    zero = jnp.zeros((n, HEAD_DIM - ROPE_DIM), F32)
    zh = jnp.zeros((n, half), F32)
    c64 = jnp.concatenate([cos, cos, one], axis=-1)
    sa64 = jnp.concatenate([zh, sin, zero], axis=-1)
    sb64 = jnp.concatenate([-sin, zh, zero], axis=-1)
    tab = jnp.concatenate([c64, c64, sa64, sa64, sb64, sb64], axis=-1)
    tabt = jnp.concatenate([cos.T, sin.T], axis=0)
    return tab, tabt


CAUSAL_SPAN = 512


def _dsa_prompt_kernel(x_ref, q_ref, qi_ref, wi_ref, kit_ref, kt_ref, vt_ref, wout_ref, g_ref, b_ref,
                       o_ref, key_ref, sel_ref, abuf, *, tq, n_keys, topk):
    t = pl.program_id(1)

    def attend(nk):
        col = lax.broadcasted_iota(jnp.int32, (tq, nk), 1)
        qpos = t * tq + lax.broadcasted_iota(jnp.int32, (tq, nk), 0)
        adm = col <= qpos

        ki = kit_ref[:, 0:nk]
        score = jnp.zeros((tq, nk), F32)
        for h in range(IDX_HEADS):
            d = _dot(qi_ref[:, h * IDX_DIM:(h + 1) * IDX_DIM], ki)
            score = score + wi_ref[:, h:h + 1] * jnp.maximum(d, 0.0)
        key_ref[:, 0:nk] = _float_key(jnp.where(adm, score, -jnp.inf))

        def count_ge(cand):
            return jnp.sum((key_ref[:, 0:nk] >= cand).astype(F32), axis=-1, keepdims=True)

        tau = _kth_largest_key(count_ge, tq, float(topk))
        key = key_ref[:, 0:nk]
        gt = key > tau
        eq = key == tau
        need = float(topk) - jnp.sum(gt.astype(F32), axis=-1, keepdims=True)
        n_eq = jnp.sum(eq.astype(F32), axis=-1, keepdims=True)
        sel_ref[:, 0:nk] = jnp.where((gt | eq) & adm, 0.0, -jnp.inf)
        crowded = (n_eq > need) & (tau > KEY_NEG_INF)

        @pl.when(jnp.max(crowded.astype(F32)) > 0.0)
        def _():
            def count_lt(p):
                k_ = key_ref[:, 0:nk]
                c_ = lax.broadcasted_iota(jnp.int32, (tq, nk), 1)
                return jnp.sum(((k_ == tau) & (c_ < p)).astype(F32), axis=-1, keepdims=True)
            lim = _tie_limit(count_lt, need, tq, int(nk).bit_length())
            k_ = key_ref[:, 0:nk]
            c_ = lax.broadcasted_iota(jnp.int32, (tq, nk), 1)
            keep = (k_ > tau) | ((k_ == tau) & (c_ <= lim))
            keep = keep & (c_ <= t * tq + lax.broadcasted_iota(jnp.int32, (tq, nk), 0))
            sel_ref[:, 0:nk] = jnp.where(keep, 0.0, -jnp.inf)

        for h in range(N_HEADS):
            kv = h // GROUP
            rows = slice(kv * HEAD_DIM, (kv + 1) * HEAD_DIM)
            logit = _dot(q_ref[:, h * HEAD_DIM:(h + 1) * HEAD_DIM], kt_ref[rows, 0:nk]) + sel_ref[:, 0:nk]
            p = jnp.exp(logit - jnp.max(logit, axis=-1, keepdims=True))
            den = jnp.sum(p, axis=-1, keepdims=True)
            o = _dot_nt(p.astype(BF16), vt_ref[rows, 0:nk])
            abuf[:, h * HEAD_DIM:(h + 1) * HEAD_DIM] = (o / den).astype(BF16)

    spans = n_keys // CAUSAL_SPAN
    tiles_per_span = CAUSAL_SPAN // tq
    for s in range(spans):
        pl.when(t // tiles_per_span == s)(functools.partial(attend, (s + 1) * CAUSAL_SPAN))

    m = _dot(abuf[...], wout_ref[...])
    o_ref[...] = _ln(ALPHA * x_ref[...] + m, g_ref[...], b_ref[...])


def _dsa_prompt(x, q, qi, wi, kit, kt, vt, w_out, g, b, bsz, t, tq=256):
    nq = t // tq
    topk = min(TOPK_MAX, t // 4)
    kern = functools.partial(_dsa_prompt_kernel, tq=tq, n_keys=t, topk=topk)
    qmap = lambda i, j: (i * nq + j, 0)
    bmap = lambda i, j: (i, 0, 0)
    return pl.pallas_call(
        kern,
        grid=(bsz, nq),
        in_specs=[
            pl.BlockSpec((tq, D_MODEL), qmap),
            pl.BlockSpec((tq, Q_END), qmap),
            pl.BlockSpec((tq, IDX_HEADS * IDX_DIM), qmap),
            pl.BlockSpec((tq, LANES), qmap),
            pl.BlockSpec((None, IDX_DIM, t), bmap),
            pl.BlockSpec((None, KV_DIM, t), bmap),
            pl.BlockSpec((None, KV_DIM, t), bmap),
            _const_spec((Q_END, D_MODEL)),
            _const_spec((1, D_MODEL)),
            _const_spec((1, D_MODEL)),
        ],
        out_specs=pl.BlockSpec((tq, D_MODEL), qmap),
        out_shape=jax.ShapeDtypeStruct((bsz * t, D_MODEL), F32),
        scratch_shapes=[pltpu.VMEM((tq, t), jnp.int32), pltpu.VMEM((tq, t), F32), pltpu.VMEM((tq, Q_END), BF16)],
        compiler_params=_params(2),
        name="dsa_prompt",
    )(x, q, qi, wi, kit, kt, vt, w_out, g, b)


IDX_PAGES_PER_STEP = 32
KV_PAGES_PER_STEP = 16


def _page_specs(layer, n_rows, pages_per_step):
    def spec(j):
        return pl.BlockSpec((None, None, n_rows, PAGE_SIZE),
                            lambda i, s, pt: (layer, pt[i, s * pages_per_step + j], 0, 0))
    return [spec(j) for j in range(pages_per_step)]


def _dsa_step_score_kernel(pt_ref, qi_ref, wi_ref, kin_ref, *rest):
    pages = rest[:IDX_PAGES_PER_STEP]
    o_ref, on_ref, kbuf_hi, kbuf_lo = rest[IDX_PAGES_PER_STEP:]
    for j, p in enumerate(pages):
        hi, lo = _split(p[...])
        kbuf_hi[:, j * PAGE_SIZE:(j + 1) * PAGE_SIZE] = hi
        kbuf_lo[:, j * PAGE_SIZE:(j + 1) * PAGE_SIZE] = lo
    qi = qi_ref[0]
    wi = wi_ref[0]
    qh, ql = _split(qi)
    d = _dot(qh, kbuf_hi[...]) + (_dot(qh, kbuf_lo[...]) + _dot(ql, kbuf_hi[...]))
    o_ref[0] = jnp.sum(wi * jnp.maximum(d, 0.0), axis=0, keepdims=True)
    dn = jnp.sum(qi * kin_ref[0], axis=-1, keepdims=True)
    sn = jnp.sum(wi * jnp.maximum(dn, 0.0), axis=0, keepdims=True)
    on_ref[0] = jnp.broadcast_to(sn, (1, LANES))


def _dsa_step_score(page_table, qi3, wi3, kin3, cache_kidx_t, layer):
    n, n_pages = page_table.shape
    steps = n_pages // IDX_PAGES_PER_STEP
    step_keys = IDX_PAGES_PER_STEP * PAGE_SIZE
    row = lambda i, s, pt: (i, 0, 0)
    grid_spec = pltpu.PrefetchScalarGridSpec(
        num_scalar_prefetch=1,
        grid=(n, steps),
        in_specs=[
            pl.BlockSpec((1, IDX_HEADS, IDX_DIM), row),
            pl.BlockSpec((1, IDX_HEADS, 1), row),
            pl.BlockSpec((1, 1, IDX_DIM), row),
        ] + _page_specs(layer, IDX_DIM, IDX_PAGES_PER_STEP),
        out_specs=[
            pl.BlockSpec((1, 1, step_keys), lambda i, s, pt: (i, 0, s)),
            pl.BlockSpec((1, 1, LANES), row),
        ],
        scratch_shapes=[pltpu.VMEM((IDX_DIM, step_keys), BF16), pltpu.VMEM((IDX_DIM, step_keys), BF16)],
    )
    return pl.pallas_call(
        _dsa_step_score_kernel,
        grid_spec=grid_spec,
        out_shape=[jax.ShapeDtypeStruct((n, 1, n_pages * PAGE_SIZE), F32),
                   jax.ShapeDtypeStruct((n, 1, LANES), F32)],
        compiler_params=_params(2),
        name="dsa_step_score",
    )(page_table, qi3, wi3, kin3, *([cache_kidx_t] * IDX_PAGES_PER_STEP))


def _dsa_step_attend_kernel(pt_ref, sp_ref, sn_ref, qbd_ref, kn_ref, vn_ref, *rest, past, topk):
    kp = rest[:KV_PAGES_PER_STEP]
    vp = rest[KV_PAGES_PER_STEP:2 * KV_PAGES_PER_STEP]
    o_ref, bias_ref, biasn_ref, kbuf, vbuf, m_ref, l_ref, acc_ref = rest[2 * KV_PAGES_PER_STEP:]
    s = pl.program_id(1)
    n_steps = pl.num_programs(1)
    step_keys = KV_PAGES_PER_STEP * PAGE_SIZE

    @pl.when(s == 0)
    def _():
        lane = lax.broadcasted_iota(jnp.int32, (1, LANES), 1)
        key_p = _float_key(sp_ref[0])
        key_n = jnp.where(lane == 0, _float_key(sn_ref[0]), KEY_NEG_INF - 1)

        def count_ge(cand):
            return (jnp.sum((key_p >= cand).astype(F32), axis=-1, keepdims=True)
                    + jnp.sum((key_n >= cand).astype(F32), axis=-1, keepdims=True))

        tau = _kth_largest_key(count_ge, 1, float(topk))
        gt_p, eq_p = key_p > tau, key_p == tau
        need = (float(topk) - jnp.sum(gt_p.astype(F32), axis=-1, keepdims=True)
                - jnp.sum((key_n > tau).astype(F32), axis=-1, keepdims=True))
        idx = lax.broadcasted_iota(jnp.int32, (1, past), 1)

        def count_lt(p):
            return jnp.sum((eq_p & (idx < p)).astype(F32), axis=-1, keepdims=True)

        lim = _tie_limit(count_lt, need, 1, int(past).bit_length())
        keep_p = gt_p | (eq_p & (idx <= lim))
        n_keep = jnp.sum(keep_p.astype(F32), axis=-1, keepdims=True)
        keep_n = (key_n > tau) | ((key_n == tau) & (n_keep < float(topk)))
        bias_ref[...] = jnp.where(keep_p, 0.0, -jnp.inf)
        biasn_ref[...] = jnp.where(keep_n & (lane == 0), 0.0, -jnp.inf)
        m_ref[...] = jnp.full(m_ref.shape, -jnp.inf, F32)
        l_ref[...] = jnp.zeros(l_ref.shape, F32)
        acc_ref[...] = jnp.zeros(acc_ref.shape, F32)

    for j in range(KV_PAGES_PER_STEP):
        kbuf[:, j * PAGE_SIZE:(j + 1) * PAGE_SIZE] = kp[j][...].astype(BF16)
        vbuf[:, j * PAGE_SIZE:(j + 1) * PAGE_SIZE] = vp[j][...].astype(BF16)
    qbd = qbd_ref[0]
    bias = bias_ref[:, pl.ds(pl.multiple_of(s * step_keys, step_keys), step_keys)]
    logit = _dot(qbd, kbuf[...]) + bias
    m_new = jnp.maximum(m_ref[...], jnp.max(logit, axis=-1, keepdims=True))
    m_safe = jnp.where(m_new == -jnp.inf, 0.0, m_new)
    scale = jnp.exp(m_ref[...] - m_safe)
    p = jnp.exp(logit - m_safe)
    l_ref[...] = l_ref[...] * scale + jnp.sum(p, axis=-1, keepdims=True)
    acc_ref[...] = acc_ref[...] * scale + _dot_nt(p.astype(BF16), vbuf[...])
    m_ref[...] = m_new

    @pl.when(s == n_steps - 1)
    def _():
        kn = kn_ref[0].astype(BF16).astype(F32)
        vn = vn_ref[0].astype(BF16).astype(F32)
        ln_ = jnp.sum(qbd.astype(F32) * kn, axis=-1, keepdims=True) + biasn_ref[:, 0:1]
        m_old = m_ref[...]
        m_new2 = jnp.maximum(m_old, ln_)
        sc = jnp.exp(m_old - m_new2)
        pn = jnp.exp(ln_ - m_new2)
        den = l_ref[...] * sc + pn
        r = (acc_ref[...] * sc + pn.astype(BF16).astype(F32) * vn) / den
        pieces = [r[h:h + 1, (h // GROUP) * HEAD_DIM:(h // GROUP + 1) * HEAD_DIM] for h in range(N_HEADS)]
        o_ref[0] = jnp.concatenate(pieces, axis=-1)


def _dsa_step_attend(page_table, sp, sn, qbd, kn3, vn3, cache_k_t, cache_v_t, layer):
    n, n_pages = page_table.shape
    past = n_pages * PAGE_SIZE
    steps = n_pages // KV_PAGES_PER_STEP
    step_keys = KV_PAGES_PER_STEP * PAGE_SIZE
    topk = min(TOPK_MAX, (past + 1) // 4)
    page_specs = _page_specs(layer, KV_DIM, KV_PAGES_PER_STEP)
    row = lambda i, s, pt: (i, 0, 0)
    grid_spec = pltpu.PrefetchScalarGridSpec(
        num_scalar_prefetch=1,
        grid=(n, steps),
        in_specs=[
            pl.BlockSpec((1, 1, past), row),
            pl.BlockSpec((1, 1, LANES), row),
            pl.BlockSpec((1, N_HEADS, KV_DIM), row),
            pl.BlockSpec((1, 1, KV_DIM), row),
            pl.BlockSpec((1, 1, KV_DIM), row),
        ] + page_specs + page_specs,
        out_specs=pl.BlockSpec((1, 1, Q_END), row),
        scratch_shapes=[
            pltpu.VMEM((1, past), F32), pltpu.VMEM((1, LANES), F32),
            pltpu.VMEM((KV_DIM, step_keys), BF16), pltpu.VMEM((KV_DIM, step_keys), BF16),
            pltpu.VMEM((N_HEADS, 1), F32), pltpu.VMEM((N_HEADS, 1), F32), pltpu.VMEM((N_HEADS, KV_DIM), F32),
        ],
    )
    kern = functools.partial(_dsa_step_attend_kernel, past=past, topk=topk)
    return pl.pallas_call(
        kern,
        grid_spec=grid_spec,
        out_shape=jax.ShapeDtypeStruct((n, 1, Q_END), F32),
        compiler_params=_params(2),
        name="dsa_step_attend",
    )(page_table, sp, sn, qbd, kn3, vn3, *([cache_k_t] * KV_PAGES_PER_STEP), *([cache_v_t] * KV_PAGES_PER_STEP))


def _proj_ln_kernel(a_ref, x_ref, w_ref, g_ref, b_ref, o_ref):
    m = _dot3(a_ref[...], w_ref[...])
    o_ref[...] = _ln(ALPHA * x_ref[...] + m, g_ref[...], b_ref[...])


def _proj_ln(a, x, w, g, b):
    n = x.shape[0]
    shapes = [(n, a.shape[1]), (n, D_MODEL), w.shape, (1, D_MODEL), (1, D_MODEL)]
    return pl.pallas_call(
        _proj_ln_kernel,
        grid=(1,),
        in_specs=[_const_spec(s) for s in shapes],
        out_specs=_const_spec((n, D_MODEL)),
        out_shape=jax.ShapeDtypeStruct((n, D_MODEL), F32),
        compiler_params=_params(1),
        name="proj_ln",
    )(a, x, w, g, b)


def _moe_kernel(x_ref, wr_ref, br_ref, w1_ref, w3_ref, w2_ref, g_ref, b_ref, o_ref, acc, comb, xb, *, precise):
    e = pl.program_id(1)
    tm = x_ref.shape[0]
    lane = lax.broadcasted_iota(jnp.int32, (tm, LANES), 1)

    @pl.when(e == 0)
    def _():
        x = x_ref[...]
        logits = jnp.dot(x, wr_ref[...], preferred_element_type=F32, precision=lax.Precision.HIGHEST)
        logits = jnp.where(lane < N_EXPERTS, logits + br_ref[...], -jnp.inf)
        lane_f = lane.astype(F32)
        m1 = jnp.max(logits, axis=-1, keepdims=True)
        i1 = jnp.min(jnp.where(logits == m1, lane_f, float(LANES)), axis=-1, keepdims=True)
        rest = jnp.where(lane_f == i1, -jnp.inf, logits)
        m2 = jnp.max(rest, axis=-1, keepdims=True)
        i2 = jnp.min(jnp.where(rest == m2, lane_f, float(LANES)), axis=-1, keepdims=True)
        e2 = jnp.exp(m2 - m1)
        den = 1.0 + e2
        comb[...] = jnp.where(lane_f == i1, 1.0 / den, 0.0) + jnp.where(lane_f == i2, e2 / den, 0.0)
        xb[...] = x.astype(xb.dtype)
        acc[...] = jnp.zeros(acc.shape, F32)

    c = jnp.sum(jnp.where(lane == e, comb[...], 0.0), axis=-1, keepdims=True)
    if precise:
        h = _silu(_dot3(xb[...], w1_ref[0])) * _dot3(xb[...], w3_ref[0])
        acc[...] += _dot3(c * h, w2_ref[0])
    else:
        h = _silu(_dot(xb[...], w1_ref[0])) * _dot(xb[...], w3_ref[0])
        acc[...] += _dot((c * h).astype(BF16), w2_ref[0])

    @pl.when(e == N_EXPERTS - 1)
    def _():
        o_ref[...] = _ln(ALPHA * x_ref[...] + acc[...], g_ref[...], b_ref[...])


def _moe(x, wr, br, w1, w3, w2, g, b, tm, precise=False):
    m = x.shape[0]
    wspec = pl.BlockSpec((1, D_MODEL, D_MODEL), lambda i, e: (e, 0, 0))
    return pl.pallas_call(
        functools.partial(_moe_kernel, precise=precise),
        grid=(m // tm, N_EXPERTS),
        in_specs=[
            pl.BlockSpec((tm, D_MODEL), lambda i, e: (i, 0)),
            _const_spec((D_MODEL, LANES)),
            _const_spec((1, LANES)),
            wspec, wspec, wspec,
            _const_spec((1, D_MODEL)),
            _const_spec((1, D_MODEL)),
        ],
        out_specs=pl.BlockSpec((tm, D_MODEL), lambda i, e: (i, 0)),
        out_shape=jax.ShapeDtypeStruct((m, D_MODEL), F32),
        scratch_shapes=[pltpu.VMEM((tm, D_MODEL), F32), pltpu.VMEM((tm, LANES), F32),
                        pltpu.VMEM((tm, D_MODEL), F32 if precise else BF16)],
        compiler_params=_params(2),
        name="moe",
    )(x, wr, br, w1, w3, w2, g, b)


def kernel(x_prompt, x_sample, cache_k, cache_v, cache_kidx, state_conv, page_table, ln1_g, ln1_b, ln2_g, ln2_b,
           w_in_even, conv_w, sgu_ln_g, sgu_ln_b, sgu_w, sgu_b, w_out_even, ffn_w1, ffn_w3, ffn_w2, w_in_odd,
           w_out_odd, router_w, router_b, moe_w1, moe_w3, moe_w2):
    bp, tp, _ = x_prompt.shape
    ns = x_sample.shape[0]
    n_pool = cache_k.shape[1]
    past = page_table.shape[1] * PAGE_SIZE

    xp = x_prompt.reshape(bp * tp, D_MODEL)
    xs = x_sample.reshape(ns, D_MODEL)
    tab_p, tabt_p = _rope_tables(jnp.arange(tp))
    tab_s, tabt_s = _rope_tables(jnp.full((ns,), past, jnp.int32))
    ck_t = jnp.transpose(cache_k, (0, 1, 3, 4, 2)).reshape(-1, n_pool, KV_DIM, PAGE_SIZE)
    cv_t = jnp.transpose(cache_v, (0, 1, 3, 4, 2)).reshape(-1, n_pool, KV_DIM, PAGE_SIZE)
    cki_t = jnp.swapaxes(cache_kidx, 2, 3)
    row = lambda a: a.reshape(1, -1)
    head_kv = (jnp.arange(N_HEADS)[:, None] // GROUP == jnp.arange(N_KV_HEADS)[None, :]).astype(BF16)

    k_p, v_p, ki_p, conv_p = [], [], [], []
    k_s, v_s, ki_s, conv_s, chunk_s = [], [], [], [], []
    for layer in range(DEPTH):
        i = layer // 2
        g1, b1, g2, b2 = row(ln1_g[layer]), row(ln1_b[layer]), row(ln2_g[layer]), row(ln2_b[layer])
        if layer % 2 == 0:
            w_in = w_in_even[i].astype(BF16)
            w_out = w_out_even[i].astype(BF16)
            sg, sbn = row(sgu_ln_g[i]), row(sgu_ln_b[i])
            bs_b = jnp.broadcast_to(sgu_b[i][:, :, None], (SGU_GROUPS, CHUNK, CHUNK))
            xp, cp = _even_mixer(xp, bp, tp, w_in, conv_w[i], sg, sbn, sgu_w[i], bs_b, w_out, g1, b1)
            wd = row(jnp.repeat(sgu_w[i][:, 0, 0], CHUNK))
            b0 = row(jnp.repeat(sgu_b[i][:, 0], CHUNK))
            xs, zs, vs = _even_mixer_step(xs, state_conv[i][:, 0], state_conv[i][:, 1], w_in_even[i], conv_w[i],
                                          sg, sbn, wd, b0, w_out_even[i], g1, b1)
            conv_p.append(cp)
            conv_s.append(jnp.stack([state_conv[i][:, 1], zs], axis=1))
            chunk_s.append(vs.reshape(ns, 1, SGU_CH))
            w1, w3, w2 = ffn_w1[i].astype(BF16), ffn_w3[i].astype(BF16), ffn_w2[i].astype(BF16)
            xp = _ffn(xp, w1, w3, w2, g2, b2, tm=512)
            xs = _ffn_step(xs, ffn_w1[i], ffn_w3[i], ffn_w2[i], g2, b2)
        else:
            w = w_in_odd[i]
            wq32 = jnp.concatenate(
                [w[:, :Q_END], w[:, V_END:QI_END],
                 jnp.pad(w[:, QI_END + IDX_DIM:], ((0, 0), (0, LANES - IDX_HEADS)))], axis=1)
            wkt32 = jnp.concatenate([w[:, Q_END:V_END], w[:, QI_END:QI_END + IDX_DIM]], axis=1).T
            wq, wkt = wq32.astype(BF16), wkt32.astype(BF16)
            w_out = w_out_odd[i].astype(BF16)
            qp, qip, wip, ktp, vtp, kitp, ktb, vtb, kitb = _odd_proj(xp, wq, wkt, tab_p, tabt_p, bp, tp, tm=512)
            xp = _dsa_prompt(xp, qp, qip, wip, kitb, ktb, vtb, w_out, g1, b1, bp, tp)
            k_p.append(ktp)
            v_p.append(vtp)
            ki_p.append(kitp)

            qs, qis, wis, kts, vts, kits, _, _, _ = _odd_proj(xs, wq32, wkt32, tab_s, tabt_s, 1, ns, tm=ns,
                                                              precise=True)
            kn, vn, kis = kts[0].T, vts[0].T, kits[0].T
            sp, sn = _dsa_step_score(page_table, qis.reshape(ns, IDX_HEADS, IDX_DIM),
                                     wis[:, :IDX_HEADS].reshape(ns, IDX_HEADS, 1), kis.reshape(ns, 1, IDX_DIM),
                                     cki_t, i)
            qbd = (qs.astype(BF16).reshape(ns, N_HEADS, 1, HEAD_DIM)
                   * head_kv[None, :, :, None]).reshape(ns, N_HEADS, KV_DIM)
            a_s = _dsa_step_attend(page_table, sp, sn, qbd, kn.reshape(ns, 1, KV_DIM), vn.reshape(ns, 1, KV_DIM),
                                   ck_t, cv_t, i)
            xs = _proj_ln(a_s.reshape(ns, Q_END), xs, w_out_odd[i], g1, b1)
            k_s.append(kn.reshape(ns, 1, N_KV_HEADS, HEAD_DIM))
            v_s.append(vn.reshape(ns, 1, N_KV_HEADS, HEAD_DIM))
            ki_s.append(kis.reshape(ns, 1, IDX_DIM))

            wr = jnp.pad(router_w[i], ((0, 0), (0, LANES - N_EXPERTS)))
            br = jnp.pad(row(router_b[i]), ((0, 0), (0, LANES - N_EXPERTS)))
            w1, w3, w2 = moe_w1[i].astype(BF16), moe_w3[i].astype(BF16), moe_w2[i].astype(BF16)
            xp = _moe(xp, wr, br, w1, w3, w2, g2, b2, tm=1024)
            xs = _moe(xs, wr, br, moe_w1[i], moe_w3[i], moe_w2[i], g2, b2, tm=ns, precise=True)

    n_odd = len(k_p)
    kv_shape = (n_odd, bp, N_KV_HEADS, HEAD_DIM, tp)
    k_prompt = jnp.stack(k_p).reshape(kv_shape).transpose(0, 1, 4, 2, 3)
    v_prompt = jnp.stack(v_p).reshape(kv_shape).transpose(0, 1, 4, 2, 3)
    kidx_prompt = jnp.stack(ki_p).transpose(0, 1, 3, 2)
    return (xp.reshape(bp, tp, D_MODEL), xs.reshape(ns, 1, D_MODEL),
            k_prompt, v_prompt, kidx_prompt, jnp.stack(conv_p),
            jnp.stack(k_s), jnp.stack(v_s), jnp.stack(ki_s), jnp.stack(conv_s), jnp.stack(chunk_s))
```

```python
import functools

import jax
import jax.numpy as jnp
import numpy as np
from jax import lax
from jax.experimental import pallas as pl
from jax.experimental.pallas import tpu as pltpu

D_MODEL = 1024
DEPTH = 4
PAGE_SIZE = 128
CONV_CH = 512
CONV_W = 3
SGU_CH = 512
SGU_GROUPS = 4
CHUNK = 128
N_HEADS = 16
HEAD_DIM = 64
N_KV_HEADS = 4
GROUP = N_HEADS // N_KV_HEADS
KV_DIM = N_KV_HEADS * HEAD_DIM
ROPE_DIM = 16
ROPE_THETA = 500000.0
IDX_HEADS = 8
IDX_DIM = 64
TOPK_MAX = 256
D_FF = 2816
N_EXPERTS = 8
ALPHA = (2 * DEPTH) ** 0.25
LN_EPS = 1e-5

EVEN_IN = 3 * CONV_CH + 2 * SGU_CH
Q_END = N_HEADS * HEAD_DIM
K_END = Q_END + KV_DIM
V_END = K_END + KV_DIM
QI_END = V_END + IDX_HEADS * IDX_DIM

LANES = 128
VMEM_LIMIT = 56 * 1024 * 1024
KEY_NEG_INF = np.int32(-2139095041)
BF16 = jnp.bfloat16
F32 = jnp.float32


def _params(n_axes):
    return pltpu.CompilerParams(dimension_semantics=("arbitrary",) * n_axes, vmem_limit_bytes=VMEM_LIMIT)


def _const_spec(shape):
    nd = len(shape)
    return pl.BlockSpec(shape, lambda *_: (0,) * nd)


def _ln(x, g, b):
    mu = jnp.mean(x, axis=-1, keepdims=True)
    xc = x - mu
    var = jnp.mean(xc * xc, axis=-1, keepdims=True)
    return xc * lax.rsqrt(var + LN_EPS) * g + b


def _dot(a, b):
    return jnp.dot(a, b, preferred_element_type=F32)


def _dot_nt(a, b):
    return lax.dot_general(a, b, (((1,), (1,)), ((), ())), preferred_element_type=F32)


def _split(a):
    hi = a.astype(BF16)
    return hi, (a - hi.astype(F32)).astype(BF16)


def _dot3(a, b, dot=_dot):
    ah, al = _split(a)
    bh, bl = _split(b)
    return dot(ah, bh) + (dot(ah, bl) + dot(al, bh))


def _silu(x):
    return x * (1.0 / (1.0 + jnp.exp(-x)))


def _float_key(score):
    score = jnp.where(score == 0.0, 0.0, score)
    bits = pltpu.bitcast(score, jnp.int32)
    return jnp.where(bits < 0, bits ^ jnp.int32(0x7FFFFFFF), bits)


def _kth_largest_key(count_ge, rows, k):
    def body(i, lo):
        cand = lo + jnp.left_shift(jnp.int32(1), jnp.int32(31) - i)
        return jnp.where(count_ge(cand) >= k, cand, lo)
    lo0 = jnp.full((rows, 1), np.iinfo(np.int32).min, jnp.int32)
    return lax.fori_loop(0, 32, body, lo0)


def _tie_limit(count_lt, need, rows, n_bits):
    def body(i, p):
        cand = p + jnp.left_shift(jnp.int32(1), jnp.int32(n_bits - 1) - i)
        return jnp.where(count_lt(cand) < need, cand, p)
    return lax.fori_loop(0, n_bits, body, jnp.zeros((rows, 1), jnp.int32))


def _even_mixer_kernel(x_ref, win_ref, cw_ref, sg_ref, sb_ref, wm_ref, bs_ref, wout_ref, g_ref, b_ref,
                       o_ref, cs_ref, zbuf, ybuf, *, tm):
    t = pl.program_id(1)
    x = x_ref[...]
    h = _dot(x.astype(BF16), win_ref[...])
    gate_b = h[:, 0:CONV_CH]
    z = h[:, CONV_CH:2 * CONV_CH] * h[:, 2 * CONV_CH:3 * CONV_CH]
    u = h[:, 3 * CONV_CH:3 * CONV_CH + SGU_CH]
    v = h[:, 3 * CONV_CH + SGU_CH:]

    @pl.when(t == 0)
    def _():
        zbuf[0:8, :] = jnp.zeros((8, CONV_CH), F32)

    zbuf[8:8 + tm, :] = z
    conv = cw_ref[0:1, :] * zbuf[6:6 + tm, :] + cw_ref[1:2, :] * zbuf[7:7 + tm, :] + cw_ref[2:3, :] * z
    ybuf[:, 0:CONV_CH] = (gate_b * conv).astype(BF16)
    cs_ref[0] = zbuf[tm + 6:tm + 8, :]
    zbuf[0:8, :] = zbuf[tm:tm + 8, :]

    vn = _ln(v, sg_ref[...], sb_ref[...])
    row = lax.broadcasted_iota(jnp.int32, (CHUNK, CHUNK), 0)
    col = lax.broadcasted_iota(jnp.int32, (CHUNK, CHUNK), 1)
    for g in range(SGU_GROUPS):
        wm = jnp.where(row >= col, wm_ref[g], 0.0).astype(BF16)
        for c in range(tm // CHUNK):
            rs = slice(c * CHUNK, (c + 1) * CHUNK)
            cs = slice(g * CHUNK, (g + 1) * CHUNK)
            s = _dot(wm, vn[rs, cs].astype(BF16)) + bs_ref[g]
            ybuf[rs, CONV_CH + g * CHUNK:CONV_CH + (g + 1) * CHUNK] = (u[rs, cs] * s).astype(BF16)

    m = _dot(ybuf[...], wout_ref[...])
    o_ref[...] = _ln(ALPHA * x + m, g_ref[...], b_ref[...])


def _even_mixer(x, bsz, t, w_in, conv_w, sgu_g, sgu_bn, sgu_w, bs_b, w_out, g, b, tm=256):
    nt = t // tm
    kern = functools.partial(_even_mixer_kernel, tm=tm)
    return pl.pallas_call(
        kern,
        grid=(bsz, nt),
        in_specs=[
            pl.BlockSpec((tm, D_MODEL), lambda i, j: (i * nt + j, 0)),
            _const_spec((D_MODEL, EVEN_IN)),
            _const_spec((CONV_W, CONV_CH)),
            _const_spec((1, SGU_CH)),
            _const_spec((1, SGU_CH)),
            _const_spec((SGU_GROUPS, CHUNK, CHUNK)),
            _const_spec((SGU_GROUPS, CHUNK, CHUNK)),
            _const_spec((D_MODEL, D_MODEL)),
            _const_spec((1, D_MODEL)),
            _const_spec((1, D_MODEL)),
        ],
        out_specs=[
            pl.BlockSpec((tm, D_MODEL), lambda i, j: (i * nt + j, 0)),
            pl.BlockSpec((1, CONV_W - 1, CONV_CH), lambda i, j: (i, 0, 0)),
        ],
        out_shape=[
            jax.ShapeDtypeStruct((bsz * t, D_MODEL), F32),
            jax.ShapeDtypeStruct((bsz, CONV_W - 1, CONV_CH), F32),
        ],
        scratch_shapes=[pltpu.VMEM((tm + 8, CONV_CH), F32), pltpu.VMEM((tm, D_MODEL), BF16)],
        compiler_params=_params(2),
        name="even_mixer",
    )(x, w_in, conv_w, sgu_g, sgu_bn, sgu_w, bs_b, w_out, g, b)


def _even_mixer_step_kernel(x_ref, p0_ref, p1_ref, win_ref, cw_ref, sg_ref, sb_ref, wd_ref, b0_ref, wout_ref,
                            g_ref, b_ref, o_ref, z_ref, vn_ref):
    x = x_ref[...]
    gate_b, gate_c, h_in, u, v = [_dot3(x, win_ref[:, j * CONV_CH:(j + 1) * CONV_CH]) for j in range(5)]
    z = gate_c * h_in
    conv = cw_ref[0:1, :] * p0_ref[...] + cw_ref[1:2, :] * p1_ref[...] + cw_ref[2:3, :] * z
    vn = _ln(v, sg_ref[...], sb_ref[...])
    s = wd_ref[...] * vn + b0_ref[...]
    y = jnp.concatenate([gate_b * conv, u * s], axis=-1)
    m = _dot3(y, wout_ref[...])
    o_ref[...] = _ln(ALPHA * x + m, g_ref[...], b_ref[...])
    z_ref[...] = z
    vn_ref[...] = vn


def _even_mixer_step(x, p0, p1, w_in, conv_w, sgu_g, sgu_bn, wd, b0, w_out, g, b):
    n = x.shape[0]
    shapes = [(n, D_MODEL), (n, CONV_CH), (n, CONV_CH), (D_MODEL, EVEN_IN), (CONV_W, CONV_CH), (1, SGU_CH),
              (1, SGU_CH), (1, SGU_CH), (1, SGU_CH), (D_MODEL, D_MODEL), (1, D_MODEL), (1, D_MODEL)]
    return pl.pallas_call(
        _even_mixer_step_kernel,
        grid=(1,),
        in_specs=[_const_spec(s) for s in shapes],
        out_specs=[_const_spec((n, D_MODEL)), _const_spec((n, CONV_CH)), _const_spec((n, SGU_CH))],
        out_shape=[jax.ShapeDtypeStruct((n, D_MODEL), F32), jax.ShapeDtypeStruct((n, CONV_CH), F32),
                   jax.ShapeDtypeStruct((n, SGU_CH), F32)],
        compiler_params=_params(1),
        name="even_mixer_step",
    )(x, p0, p1, w_in, conv_w, sgu_g, sgu_bn, wd, b0, w_out, g, b)


FF_TILE = 256


def _ffn_kernel(x_ref, w1_ref, w3_ref, w2_ref, g_ref, b_ref, o_ref, hbuf):
    x = x_ref[...]
    xb = x.astype(BF16)
    for f in range(D_FF // FF_TILE):
        cs = slice(f * FF_TILE, (f + 1) * FF_TILE)
        hbuf[:, cs] = (_silu(_dot(xb, w1_ref[:, cs])) * _dot(xb, w3_ref[:, cs])).astype(BF16)
    y = _dot(hbuf[...], w2_ref[...])
    o_ref[...] = _ln(ALPHA * x + y, g_ref[...], b_ref[...])


def _ffn(x, w1, w3, w2, g, b, tm):
    m = x.shape[0]
    return pl.pallas_call(
        _ffn_kernel,
        grid=(m // tm,),
        in_specs=[
            pl.BlockSpec((tm, D_MODEL), lambda i: (i, 0)),
            _const_spec((D_MODEL, D_FF)),
            _const_spec((D_MODEL, D_FF)),
            _const_spec((D_FF, D_MODEL)),
            _const_spec((1, D_MODEL)),
            _const_spec((1, D_MODEL)),
        ],
        out_specs=pl.BlockSpec((tm, D_MODEL), lambda i: (i, 0)),
        out_shape=jax.ShapeDtypeStruct((m, D_MODEL), F32),
        scratch_shapes=[pltpu.VMEM((tm, D_FF), BF16)],
        compiler_params=_params(1),
        name="ffn",
    )(x, w1, w3, w2, g, b)


def _ffn_step_kernel(x_ref, w1_ref, w3_ref, w2_ref, g_ref, b_ref, o_ref, acc):
    f = pl.program_id(0)

    @pl.when(f == 0)
    def _():
        acc[...] = jnp.zeros(acc.shape, F32)

    x = x_ref[...]
    h = _silu(_dot3(x, w1_ref[...])) * _dot3(x, w3_ref[...])
    acc[...] += _dot3(h, w2_ref[...])

    @pl.when(f == pl.num_programs(0) - 1)
    def _():
        o_ref[...] = _ln(ALPHA * x + acc[...], g_ref[...], b_ref[...])


def _ffn_step(x, w1, w3, w2, g, b):
    n = x.shape[0]
    return pl.pallas_call(
        _ffn_step_kernel,
        grid=(D_FF // FF_TILE,),
        in_specs=[
            _const_spec((n, D_MODEL)),
            pl.BlockSpec((D_MODEL, FF_TILE), lambda f: (0, f)),
            pl.BlockSpec((D_MODEL, FF_TILE), lambda f: (0, f)),
            pl.BlockSpec((FF_TILE, D_MODEL), lambda f: (f, 0)),
            _const_spec((1, D_MODEL)),
            _const_spec((1, D_MODEL)),
        ],
        out_specs=_const_spec((n, D_MODEL)),
        out_shape=jax.ShapeDtypeStruct((n, D_MODEL), F32),
        scratch_shapes=[pltpu.VMEM((n, D_MODEL), F32)],
        compiler_params=_params(1),
        name="ffn_step",
    )(x, w1, w3, w2, g, b)


QQI = Q_END + IDX_HEADS * IDX_DIM
KVT_ROWS = 2 * KV_DIM + IDX_DIM


def _rope(xc, c, sa, sb):
    return xc * c + pltpu.roll(xc, 8, 1) * sa + pltpu.roll(xc, LANES - 8, 1) * sb


def _rope_rows(ht, r0, cos, sin, out_ref, o0, n_rows):
    x1, x2 = ht[r0:r0 + 8, :], ht[r0 + 8:r0 + 16, :]
    out_ref[o0:o0 + 8, :] = x1 * cos - x2 * sin
    out_ref[o0 + 8:o0 + 16, :] = x1 * sin + x2 * cos
    out_ref[o0 + 16:o0 + n_rows, :] = ht[r0 + 16:r0 + n_rows, :]


def _odd_proj_kernel(x_ref, wq_ref, wkt_ref, tab_ref, tabt_ref,
                     q_ref, qi_ref, wi_ref, kt_ref, vt_ref, kit_ref, ktb_ref, vtb_ref, kitb_ref, *, precise):
    if precise:
        x = x_ref[...]
        h = _dot3(x, wq_ref[...])
        ht = _dot3(wkt_ref[...], x, dot=_dot_nt)
    else:
        xb = x_ref[...].astype(BF16)
        h = _dot(xb, wq_ref[...])
        ht = _dot_nt(wkt_ref[...], xb)
    c, sa, sb = tab_ref[:, 0:LANES], tab_ref[:, LANES:2 * LANES], tab_ref[:, 2 * LANES:3 * LANES]
    for j in range(Q_END // LANES):
        cs = slice(j * LANES, (j + 1) * LANES)
        q_ref[:, cs] = (_rope(h[:, cs], c, sa, sb) * (HEAD_DIM ** -0.5)).astype(q_ref.dtype)
    for j in range(IDX_HEADS * IDX_DIM // LANES):
        cs = slice(j * LANES, (j + 1) * LANES)
        qi_ref[:, cs] = (_rope(h[:, Q_END + j * LANES:Q_END + (j + 1) * LANES], c, sa, sb)
                         * (IDX_DIM ** -0.5)).astype(qi_ref.dtype)
    wi_ref[...] = h[:, QQI:QQI + LANES] * (IDX_HEADS ** -0.5)

    cos, sin = tabt_ref[0:8, :], tabt_ref[8:16, :]
    for hd in range(N_KV_HEADS):
        _rope_rows(ht, hd * HEAD_DIM, cos, sin, kt_ref, hd * HEAD_DIM, HEAD_DIM)
    vt_ref[...] = ht[KV_DIM:2 * KV_DIM, :]
    _rope_rows(ht, 2 * KV_DIM, cos, sin, kit_ref, 0, IDX_DIM)
    ktb_ref[...] = kt_ref[...].astype(BF16)
    vtb_ref[...] = vt_ref[...].astype(BF16)
    kitb_ref[...] = kit_ref[...].astype(BF16)


def _odd_proj(x, wq, wkt, tab, tabt, bsz, t, tm, precise=False):
    nt = t // tm
    tok = lambda i: (i, 0)
    feat = lambda i: (i // nt, 0, i % nt)
    q_dtype = F32 if precise else BF16
    row_widths = [(Q_END, q_dtype), (IDX_HEADS * IDX_DIM, q_dtype), (LANES, F32)]
    feat_rows = [(KV_DIM, F32), (KV_DIM, F32), (IDX_DIM, F32), (KV_DIM, BF16), (KV_DIM, BF16), (IDX_DIM, BF16)]
    return pl.pallas_call(
        functools.partial(_odd_proj_kernel, precise=precise),
        grid=(bsz * nt,),
        in_specs=[
            pl.BlockSpec((tm, D_MODEL), tok),
            _const_spec((D_MODEL, QQI + LANES)),
            _const_spec((KVT_ROWS, D_MODEL)),
            pl.BlockSpec((tm, 3 * LANES), lambda i: (i % nt, 0)),
            pl.BlockSpec((ROPE_DIM, tm), lambda i: (0, i % nt)),
        ],
        out_specs=[pl.BlockSpec((tm, w_), tok) for w_, _ in row_widths]
        + [pl.BlockSpec((None, r_, tm), feat) for r_, _ in feat_rows],
        out_shape=[jax.ShapeDtypeStruct((bsz * t, w_), d_) for w_, d_ in row_widths]
        + [jax.ShapeDtypeStruct((bsz, r_, t), d_) for r_, d_ in feat_rows],
        compiler_params=_params(1),
        name="odd_proj",
    )(x, wq, wkt, tab, tabt)


def _rope_tables(pos):
    half = ROPE_DIM // 2
    inv = ROPE_THETA ** (-jnp.arange(half, dtype=F32) / half)
    ang = pos.astype(F32)[:, None] * inv[None, :]
    cos, sin = jnp.cos(ang), jnp.sin(ang)
    n = pos.shape[0]
    one = jnp.ones((n, HEAD_DIM - ROPE_DIM), F32)
    zero = jnp.zeros((n, HEAD_DIM - ROPE_DIM), F32)
    zh = jnp.zeros((n, half), F32)
    c64 = jnp.concatenate([cos, cos, one], axis=-1)
    sa64 = jnp.concatenate([zh, sin, zero], axis=-1)
    sb64 = jnp.concatenate([-sin, zh, zero], axis=-1)
    tab = jnp.concatenate([c64, c64, sa64, sa64, sb64, sb64], axis=-1)
    tabt = jnp.concatenate([cos.T, sin.T], axis=0)
    return tab, tabt


KEY_BLOCK = 512


def _lane_fold(a, op):
    parts = [a[:, c * LANES:(c + 1) * LANES] for c in range(a.shape[1] // LANES)]
    while len(parts) > 1:
        parts = [op(parts[i], parts[i + 1]) for i in range(0, len(parts), 2)]
    return parts[0]


def _dsa_prompt_kernel(x_ref, q_ref, qi_ref, wi_ref, kit_ref, kt_ref, vt_ref, wout_ref, g_ref, b_ref,
                       o_ref, key_ref, bias_ref, lbuf, wib, lim_ref, q3, a3, mbuf, sbuf, obuf, *, tq, topk):
    t = pl.program_id(1)
    n_blk = (t * tq) // KEY_BLOCK + 1

    def blk(kb):
        return pl.ds(pl.multiple_of(kb * KEY_BLOCK, KEY_BLOCK), KEY_BLOCK)

    def bcast(col):
        return jnp.broadcast_to(col, (tq, LANES))

    def chunks(a):
        return [a[:, c * LANES:(c + 1) * LANES] for c in range(KEY_BLOCK // LANES)]

    row = lax.broadcasted_iota(jnp.int32, (tq, KEY_BLOCK), 0)
    lane = lax.broadcasted_iota(jnp.int32, (tq, KEY_BLOCK), 1)
    lane1 = lax.broadcasted_iota(jnp.int32, (tq, LANES), 1)

    for h in range(IDX_HEADS):
        wib[h] = bcast(wi_ref[:, h:h + 1])
    for h in range(N_HEADS):
        q3[h // GROUP, (h % GROUP) * tq:(h % GROUP + 1) * tq, :] = q_ref[:, h * HEAD_DIM:(h + 1) * HEAD_DIM]

    def score_body(kb, carry):
        ki = kit_ref[:, blk(kb)]
        score = [jnp.zeros((tq, LANES), F32)] * (KEY_BLOCK // LANES)
        for h in range(IDX_HEADS):
            d = _dot(qi_ref[:, h * IDX_DIM:(h + 1) * IDX_DIM], ki)
            w = wib[h]
            score = [s_ + w * jnp.maximum(d_, 0.0) for s_, d_ in zip(score, chunks(d))]
        adm = kb * KEY_BLOCK + lane <= t * tq + row
        key_ref[:, blk(kb)] = _float_key(jnp.where(adm, jnp.concatenate(score, axis=-1), -jnp.inf))
        return carry

    lax.fori_loop(0, n_blk, score_body, 0)

    def count(pred):
        def body(kb, part):
            for c, k_ in enumerate(chunks(key_ref[:, blk(kb)])):
                part = part + pred(k_, kb * KEY_BLOCK + c * LANES).astype(F32)
            return part
        part = lax.fori_loop(0, n_blk, body, jnp.zeros((tq, LANES), F32))
        return jnp.sum(part, axis=-1, keepdims=True)

    def count_ge(cand):
        cb = bcast(cand)
        return count(lambda k_, i0: k_ >= cb)

    tau = _kth_largest_key(count_ge, tq, float(topk))
    taub = bcast(tau)
    need = float(topk) - count(lambda k_, i0: k_ > taub)
    n_eq = count(lambda k_, i0: k_ == taub)
    crowded = (n_eq > need) & (tau > KEY_NEG_INF)
    lim_ref[...] = jnp.full((tq, LANES), np.iinfo(np.int32).max, jnp.int32)

    @pl.when(jnp.max(crowded.astype(F32)) > 0.0)
    def _():
        def count_lt(p):
            pb = bcast(p)
            return count(lambda k_, i0: (k_ == taub) & (i0 + lane1 < pb))
        lim_ref[...] = bcast(_tie_limit(count_lt, need, tq, 16))

    def bias_body(kb, carry):
        limb = lim_ref[...]
        out = []
        for c, k_ in enumerate(chunks(key_ref[:, blk(kb)])):
            idx = kb * KEY_BLOCK + c * LANES + lane1
            keep = ((k_ > taub) | ((k_ == taub) & (idx <= limb))) & (k_ > KEY_NEG_INF)
            out.append(jnp.where(keep, 0.0, -jnp.inf))
        bias_ref[:, blk(kb)] = jnp.concatenate(out, axis=-1)
        return carry

    lax.fori_loop(0, n_blk, bias_body, 0)

    def group_body(kv, carry):
        qg = q3[kv]
        mbuf[...] = jnp.full(mbuf.shape, -jnp.inf, F32)
        sbuf[...] = jnp.zeros(sbuf.shape, F32)
        obuf[...] = jnp.zeros(obuf.shape, F32)

        def logit_body(kb, c_):
            logit = _dot(qg, kt_ref[kv, :, blk(kb)])
            bias = bias_ref[:, blk(kb)]
            for j in range(GROUP):
                rows = slice(j * tq, (j + 1) * tq)
                lj = logit[rows, :] + bias
                lbuf[rows, blk(kb)] = lj
                mbuf[rows, :] = jnp.maximum(mbuf[rows, :], _lane_fold(lj, jnp.maximum))
            return c_

        lax.fori_loop(0, n_blk, logit_body, 0)
        mbs = [bcast(jnp.max(mbuf[j * tq:(j + 1) * tq, :], axis=-1, keepdims=True)) for j in range(GROUP)]

        def pv_body(kb, c_):
            pj = []
            for j in range(GROUP):
                rows = slice(j * tq, (j + 1) * tq)
                ps = [jnp.exp(l_ - mbs[j]) for l_ in chunks(lbuf[rows, blk(kb)])]
                sbuf[rows, :] = sbuf[rows, :] + functools.reduce(jnp.add, ps)
                pj.append(jnp.concatenate(ps, axis=-1).astype(BF16))
            obuf[...] += _dot_nt(jnp.concatenate(pj, axis=0), vt_ref[kv, :, blk(kb)])
            return c_

        lax.fori_loop(0, n_blk, pv_body, 0)
        a3[kv] = (obuf[...] / jnp.sum(sbuf[...], axis=-1, keepdims=True)).astype(BF16)
        return carry

    lax.fori_loop(0, N_KV_HEADS, group_body, 0)

    a = jnp.concatenate([a3[h // GROUP, (h % GROUP) * tq:(h % GROUP + 1) * tq, :] for h in range(N_HEADS)], axis=-1)
    m = _dot(a, wout_ref[...])
    o_ref[...] = _ln(ALPHA * x_ref[...] + m, g_ref[...], b_ref[...])


def _dsa_prompt(x, q, qi, wi, kit, kt, vt, w_out, g, b, bsz, t, tq=256):
    nq = t // tq
    topk = min(TOPK_MAX, t // 4)
    kern = functools.partial(_dsa_prompt_kernel, tq=tq, topk=topk)
    qmap = lambda i, j: (i * nq + j, 0)
    return pl.pallas_call(
        kern,
        grid=(bsz, nq),
        in_specs=[
            pl.BlockSpec((tq, D_MODEL), qmap),
            pl.BlockSpec((tq, Q_END), qmap),
            pl.BlockSpec((tq, IDX_HEADS * IDX_DIM), qmap),
            pl.BlockSpec((tq, LANES), qmap),
            pl.BlockSpec((None, IDX_DIM, t), lambda i, j: (i, 0, 0)),
            pl.BlockSpec((None, N_KV_HEADS, HEAD_DIM, t), lambda i, j: (i, 0, 0, 0)),
            pl.BlockSpec((None, N_KV_HEADS, HEAD_DIM, t), lambda i, j: (i, 0, 0, 0)),
            _const_spec((Q_END, D_MODEL)),
            _const_spec((1, D_MODEL)),
            _const_spec((1, D_MODEL)),
        ],
        out_specs=pl.BlockSpec((tq, D_MODEL), qmap),
        out_shape=jax.ShapeDtypeStruct((bsz * t, D_MODEL), F32),
        scratch_shapes=[
            pltpu.VMEM((tq, t), jnp.int32),
            pltpu.VMEM((tq, t), F32),
            pltpu.VMEM((GROUP * tq, t), F32),
            pltpu.VMEM((IDX_HEADS, tq, LANES), F32),
            pltpu.VMEM((tq, LANES), jnp.int32),
            pltpu.VMEM((N_KV_HEADS, GROUP * tq, HEAD_DIM), BF16),
            pltpu.VMEM((N_KV_HEADS, GROUP * tq, HEAD_DIM), BF16),
            pltpu.VMEM((GROUP * tq, LANES), F32),
            pltpu.VMEM((GROUP * tq, LANES), F32),
            pltpu.VMEM((GROUP * tq, HEAD_DIM), F32),
        ],
        compiler_params=_params(2),
        name="dsa_prompt",
    )(x, q, qi, wi, kit, kt, vt, w_out, g, b)


IDX_PAGES_PER_STEP = 32
KV_PAGES_PER_STEP = 16


def _page_specs(layer, n_rows, pages_per_step):
    def spec(j):
        return pl.BlockSpec((None, None, n_rows, PAGE_SIZE),
                            lambda i, s, pt: (layer, pt[i, s * pages_per_step + j], 0, 0))
    return [spec(j) for j in range(pages_per_step)]


def _dsa_step_score_kernel(pt_ref, qi_ref, wi_ref, kin_ref, *rest):
    pages = rest[:IDX_PAGES_PER_STEP]
    o_ref, on_ref, kbuf_hi, kbuf_lo = rest[IDX_PAGES_PER_STEP:]
    for j, p in enumerate(pages):
        hi, lo = _split(p[...])
        kbuf_hi[:, j * PAGE_SIZE:(j + 1) * PAGE_SIZE] = hi
        kbuf_lo[:, j * PAGE_SIZE:(j + 1) * PAGE_SIZE] = lo
    qi = qi_ref[0]
    wi = wi_ref[0]
    qh, ql = _split(qi)
    d = _dot(qh, kbuf_hi[...]) + (_dot(qh, kbuf_lo[...]) + _dot(ql, kbuf_hi[...]))
    o_ref[0] = jnp.sum(wi * jnp.maximum(d, 0.0), axis=0, keepdims=True)
    dn = jnp.sum(qi * kin_ref[0], axis=-1, keepdims=True)
    sn = jnp.sum(wi * jnp.maximum(dn, 0.0), axis=0, keepdims=True)
    on_ref[0] = jnp.broadcast_to(sn, (1, LANES))


def _dsa_step_score(page_table, qi3, wi3, kin3, cache_kidx_t, layer):
    n, n_pages = page_table.shape
    steps = n_pages // IDX_PAGES_PER_STEP
    step_keys = IDX_PAGES_PER_STEP * PAGE_SIZE
    row = lambda i, s, pt: (i, 0, 0)
    grid_spec = pltpu.PrefetchScalarGridSpec(
        num_scalar_prefetch=1,
        grid=(n, steps),
        in_specs=[
            pl.BlockSpec((1, IDX_HEADS, IDX_DIM), row),
            pl.BlockSpec((1, IDX_HEADS, 1), row),
            pl.BlockSpec((1, 1, IDX_DIM), row),
        ] + _page_specs(layer, IDX_DIM, IDX_PAGES_PER_STEP),
        out_specs=[
            pl.BlockSpec((1, 1, step_keys), lambda i, s, pt: (i, 0, s)),
            pl.BlockSpec((1, 1, LANES), row),
        ],
        scratch_shapes=[pltpu.VMEM((IDX_DIM, step_keys), BF16), pltpu.VMEM((IDX_DIM, step_keys), BF16)],
    )
    return pl.pallas_call(
        _dsa_step_score_kernel,
        grid_spec=grid_spec,
        out_shape=[jax.ShapeDtypeStruct((n, 1, n_pages * PAGE_SIZE), F32),
                   jax.ShapeDtypeStruct((n, 1, LANES), F32)],
        compiler_params=_params(2),
        name="dsa_step_score",
    )(page_table, qi3, wi3, kin3, *([cache_kidx_t] * IDX_PAGES_PER_STEP))


def _dsa_step_attend_kernel(pt_ref, sp_ref, sn_ref, qbd_ref, kn_ref, vn_ref, *rest, past, topk):
    kp = rest[:KV_PAGES_PER_STEP]
    vp = rest[KV_PAGES_PER_STEP:2 * KV_PAGES_PER_STEP]
    o_ref, bias_ref, biasn_ref, kbuf, vbuf, m_ref, l_ref, acc_ref = rest[2 * KV_PAGES_PER_STEP:]
    s = pl.program_id(1)
    n_steps = pl.num_programs(1)
    step_keys = KV_PAGES_PER_STEP * PAGE_SIZE

    @pl.when(s == 0)
    def _():
        lane = lax.broadcasted_iota(jnp.int32, (1, LANES), 1)
        key_p = _float_key(sp_ref[0])
        key_n = jnp.where(lane == 0, _float_key(sn_ref[0]), KEY_NEG_INF - 1)

        def count_ge(cand):
            return (jnp.sum((key_p >= cand).astype(F32), axis=-1, keepdims=True)
                    + jnp.sum((key_n >= cand).astype(F32), axis=-1, keepdims=True))

        tau = _kth_largest_key(count_ge, 1, float(topk))
        gt_p, eq_p = key_p > tau, key_p == tau
        need = (float(topk) - jnp.sum(gt_p.astype(F32), axis=-1, keepdims=True)
                - jnp.sum((key_n > tau).astype(F32), axis=-1, keepdims=True))
        idx = lax.broadcasted_iota(jnp.int32, (1, past), 1)

        def count_lt(p):
            return jnp.sum((eq_p & (idx < p)).astype(F32), axis=-1, keepdims=True)

        lim = _tie_limit(count_lt, need, 1, int(past).bit_length())
        keep_p = gt_p | (eq_p & (idx <= lim))
        n_keep = jnp.sum(keep_p.astype(F32), axis=-1, keepdims=True)
        keep_n = (key_n > tau) | ((key_n == tau) & (n_keep < float(topk)))
        bias_ref[...] = jnp.where(keep_p, 0.0, -jnp.inf)
        biasn_ref[...] = jnp.where(keep_n & (lane == 0), 0.0, -jnp.inf)
        m_ref[...] = jnp.full(m_ref.shape, -jnp.inf, F32)
        l_ref[...] = jnp.zeros(l_ref.shape, F32)
        acc_ref[...] = jnp.zeros(acc_ref.shape, F32)

    for j in range(KV_PAGES_PER_STEP):
        kbuf[:, j * PAGE_SIZE:(j + 1) * PAGE_SIZE] = kp[j][...].astype(BF16)
        vbuf[:, j * PAGE_SIZE:(j + 1) * PAGE_SIZE] = vp[j][...].astype(BF16)
    qbd = qbd_ref[0]
    bias = bias_ref[:, pl.ds(pl.multiple_of(s * step_keys, step_keys), step_keys)]
    logit = _dot(qbd, kbuf[...]) + bias
    m_new = jnp.maximum(m_ref[...], jnp.max(logit, axis=-1, keepdims=True))
    m_safe = jnp.where(m_new == -jnp.inf, 0.0, m_new)
    scale = jnp.exp(m_ref[...] - m_safe)
    p = jnp.exp(logit - m_safe)
    l_ref[...] = l_ref[...] * scale + jnp.sum(p, axis=-1, keepdims=True)
    acc_ref[...] = acc_ref[...] * scale + _dot_nt(p.astype(BF16), vbuf[...])
    m_ref[...] = m_new

    @pl.when(s == n_steps - 1)
    def _():
        kn = kn_ref[0].astype(BF16).astype(F32)
        vn = vn_ref[0].astype(BF16).astype(F32)
        ln_ = jnp.sum(qbd.astype(F32) * kn, axis=-1, keepdims=True) + biasn_ref[:, 0:1]
        m_old = m_ref[...]
        m_new2 = jnp.maximum(m_old, ln_)
        sc = jnp.exp(m_old - m_new2)
        pn = jnp.exp(ln_ - m_new2)
        den = l_ref[...] * sc + pn
        r = (acc_ref[...] * sc + pn.astype(BF16).astype(F32) * vn) / den
        pieces = [r[h:h + 1, (h // GROUP) * HEAD_DIM:(h // GROUP + 1) * HEAD_DIM] for h in range(N_HEADS)]
        o_ref[0] = jnp.concatenate(pieces, axis=-1)


def _dsa_step_attend(page_table, sp, sn, qbd, kn3, vn3, cache_k_t, cache_v_t, layer):
    n, n_pages = page_table.shape
    past = n_pages * PAGE_SIZE
    steps = n_pages // KV_PAGES_PER_STEP
    step_keys = KV_PAGES_PER_STEP * PAGE_SIZE
    topk = min(TOPK_MAX, (past + 1) // 4)
    page_specs = _page_specs(layer, KV_DIM, KV_PAGES_PER_STEP)
    row = lambda i, s, pt: (i, 0, 0)
    grid_spec = pltpu.PrefetchScalarGridSpec(
        num_scalar_prefetch=1,
        grid=(n, steps),
        in_specs=[
            pl.BlockSpec((1, 1, past), row),
            pl.BlockSpec((1, 1, LANES), row),
            pl.BlockSpec((1, N_HEADS, KV_DIM), row),
            pl.BlockSpec((1, 1, KV_DIM), row),
            pl.BlockSpec((1, 1, KV_DIM), row),
        ] + page_specs + page_specs,
        out_specs=pl.BlockSpec((1, 1, Q_END), row),
        scratch_shapes=[
            pltpu.VMEM((1, past), F32), pltpu.VMEM((1, LANES), F32),
            pltpu.VMEM((KV_DIM, step_keys), BF16), pltpu.VMEM((KV_DIM, step_keys), BF16),
            pltpu.VMEM((N_HEADS, 1), F32), pltpu.VMEM((N_HEADS, 1), F32), pltpu.VMEM((N_HEADS, KV_DIM), F32),
        ],
    )
    kern = functools.partial(_dsa_step_attend_kernel, past=past, topk=topk)
    return pl.pallas_call(
        kern,
        grid_spec=grid_spec,
        out_shape=jax.ShapeDtypeStruct((n, 1, Q_END), F32),
        compiler_params=_params(2),
        name="dsa_step_attend",
    )(page_table, sp, sn, qbd, kn3, vn3, *([cache_k_t] * KV_PAGES_PER_STEP), *([cache_v_t] * KV_PAGES_PER_STEP))


def _proj_ln_kernel(a_ref, x_ref, w_ref, g_ref, b_ref, o_ref):
    m = _dot3(a_ref[...], w_ref[...])
    o_ref[...] = _ln(ALPHA * x_ref[...] + m, g_ref[...], b_ref[...])


def _proj_ln(a, x, w, g, b):
    n = x.shape[0]
    shapes = [(n, a.shape[1]), (n, D_MODEL), w.shape, (1, D_MODEL), (1, D_MODEL)]
    return pl.pallas_call(
        _proj_ln_kernel,
        grid=(1,),
        in_specs=[_const_spec(s) for s in shapes],
        out_specs=_const_spec((n, D_MODEL)),
        out_shape=jax.ShapeDtypeStruct((n, D_MODEL), F32),
        compiler_params=_params(1),
        name="proj_ln",
    )(a, x, w, g, b)


def _moe_kernel(x_ref, wr_ref, br_ref, w1_ref, w3_ref, w2_ref, g_ref, b_ref, o_ref, acc, comb, xb, *, precise):
    e = pl.program_id(1)
    tm = x_ref.shape[0]
    lane = lax.broadcasted_iota(jnp.int32, (tm, LANES), 1)

    @pl.when(e == 0)
    def _():
        x = x_ref[...]
        logits = jnp.dot(x, wr_ref[...], preferred_element_type=F32, precision=lax.Precision.HIGHEST)
        logits = jnp.where(lane < N_EXPERTS, logits + br_ref[...], -jnp.inf)
        lane_f = lane.astype(F32)
        m1 = jnp.max(logits, axis=-1, keepdims=True)
        i1 = jnp.min(jnp.where(logits == m1, lane_f, float(LANES)), axis=-1, keepdims=True)
        rest = jnp.where(lane_f == i1, -jnp.inf, logits)
        m2 = jnp.max(rest, axis=-1, keepdims=True)
        i2 = jnp.min(jnp.where(rest == m2, lane_f, float(LANES)), axis=-1, keepdims=True)
        e2 = jnp.exp(m2 - m1)
        den = 1.0 + e2
        comb[...] = jnp.where(lane_f == i1, 1.0 / den, 0.0) + jnp.where(lane_f == i2, e2 / den, 0.0)
        xb[...] = x.astype(xb.dtype)
        acc[...] = jnp.zeros(acc.shape, F32)

    c = jnp.sum(jnp.where(lane == e, comb[...], 0.0), axis=-1, keepdims=True)
    if precise:
        h = _silu(_dot3(xb[...], w1_ref[0])) * _dot3(xb[...], w3_ref[0])
        acc[...] += _dot3(c * h, w2_ref[0])
    else:
        h = _silu(_dot(xb[...], w1_ref[0])) * _dot(xb[...], w3_ref[0])
        acc[...] += _dot((c * h).astype(BF16), w2_ref[0])

    @pl.when(e == N_EXPERTS - 1)
    def _():
        o_ref[...] = _ln(ALPHA * x_ref[...] + acc[...], g_ref[...], b_ref[...])


def _moe(x, wr, br, w1, w3, w2, g, b, tm, precise=False):
    m = x.shape[0]
    wspec = pl.BlockSpec((1, D_MODEL, D_MODEL), lambda i, e: (e, 0, 0))
    return pl.pallas_call(
        functools.partial(_moe_kernel, precise=precise),
        grid=(m // tm, N_EXPERTS),
        in_specs=[
            pl.BlockSpec((tm, D_MODEL), lambda i, e: (i, 0)),
            _const_spec((D_MODEL, LANES)),
            _const_spec((1, LANES)),
            wspec, wspec, wspec,
            _const_spec((1, D_MODEL)),
            _const_spec((1, D_MODEL)),
        ],
        out_specs=pl.BlockSpec((tm, D_MODEL), lambda i, e: (i, 0)),
        out_shape=jax.ShapeDtypeStruct((m, D_MODEL), F32),
        scratch_shapes=[pltpu.VMEM((tm, D_MODEL), F32), pltpu.VMEM((tm, LANES), F32),
                        pltpu.VMEM((tm, D_MODEL), F32 if precise else BF16)],
        compiler_params=_params(2),
        name="moe",
    )(x, wr, br, w1, w3, w2, g, b)


def kernel(x_prompt, x_sample, cache_k, cache_v, cache_kidx, state_conv, page_table, ln1_g, ln1_b, ln2_g, ln2_b,
           w_in_even, conv_w, sgu_ln_g, sgu_ln_b, sgu_w, sgu_b, w_out_even, ffn_w1, ffn_w3, ffn_w2, w_in_odd,
           w_out_odd, router_w, router_b, moe_w1, moe_w3, moe_w2):
    bp, tp, _ = x_prompt.shape
    ns = x_sample.shape[0]
    n_pool = cache_k.shape[1]
    past = page_table.shape[1] * PAGE_SIZE

    xp = x_prompt.reshape(bp * tp, D_MODEL)
    xs = x_sample.reshape(ns, D_MODEL)
    tab_p, tabt_p = _rope_tables(jnp.arange(tp))
    tab_s, tabt_s = _rope_tables(jnp.full((ns,), past, jnp.int32))
    ck_t = jnp.transpose(cache_k, (0, 1, 3, 4, 2)).reshape(-1, n_pool, KV_DIM, PAGE_SIZE)
    cv_t = jnp.transpose(cache_v, (0, 1, 3, 4, 2)).reshape(-1, n_pool, KV_DIM, PAGE_SIZE)
    cki_t = jnp.swapaxes(cache_kidx, 2, 3)
    row = lambda a: a.reshape(1, -1)
    head_kv = (jnp.arange(N_HEADS)[:, None] // GROUP == jnp.arange(N_KV_HEADS)[None, :]).astype(BF16)

    k_p, v_p, ki_p, conv_p = [], [], [], []
    k_s, v_s, ki_s, conv_s, chunk_s = [], [], [], [], []
    for layer in range(DEPTH):
        i = layer // 2
        g1, b1, g2, b2 = row(ln1_g[layer]), row(ln1_b[layer]), row(ln2_g[layer]), row(ln2_b[layer])
        if layer % 2 == 0:
            w_in = w_in_even[i].astype(BF16)
            w_out = w_out_even[i].astype(BF16)
            sg, sbn = row(sgu_ln_g[i]), row(sgu_ln_b[i])
            bs_b = jnp.broadcast_to(sgu_b[i][:, :, None], (SGU_GROUPS, CHUNK, CHUNK))
            xp, cp = _even_mixer(xp, bp, tp, w_in, conv_w[i], sg, sbn, sgu_w[i], bs_b, w_out, g1, b1)
            wd = row(jnp.repeat(sgu_w[i][:, 0, 0], CHUNK))
            b0 = row(jnp.repeat(sgu_b[i][:, 0], CHUNK))
            xs, zs, vs = _even_mixer_step(xs, state_conv[i][:, 0], state_conv[i][:, 1], w_in_even[i], conv_w[i],
                                          sg, sbn, wd, b0, w_out_even[i], g1, b1)
            conv_p.append(cp)
            conv_s.append(jnp.stack([state_conv[i][:, 1], zs], axis=1))
            chunk_s.append(vs.reshape(ns, 1, SGU_CH))
            w1, w3, w2 = ffn_w1[i].astype(BF16), ffn_w3[i].astype(BF16), ffn_w2[i].astype(BF16)
            xp = _ffn(xp, w1, w3, w2, g2, b2, tm=512)
            xs = _ffn_step(xs, ffn_w1[i], ffn_w3[i], ffn_w2[i], g2, b2)
        else:
            w = w_in_odd[i]
            wq32 = jnp.concatenate(
                [w[:, :Q_END], w[:, V_END:QI_END],
                 jnp.pad(w[:, QI_END + IDX_DIM:], ((0, 0), (0, LANES - IDX_HEADS)))], axis=1)
            wkt32 = jnp.concatenate([w[:, Q_END:V_END], w[:, QI_END:QI_END + IDX_DIM]], axis=1).T
            wq, wkt = wq32.astype(BF16), wkt32.astype(BF16)
            w_out = w_out_odd[i].astype(BF16)
            qp, qip, wip, ktp, vtp, kitp, ktb, vtb, kitb = _odd_proj(xp, wq, wkt, tab_p, tabt_p, bp, tp, tm=512)
            head_major = (bp, N_KV_HEADS, HEAD_DIM, tp)
            xp = _dsa_prompt(xp, qp, qip, wip, kitb, ktb.reshape(head_major), vtb.reshape(head_major), w_out,
                             g1, b1, bp, tp)
            k_p.append(ktp)
            v_p.append(vtp)
            ki_p.append(kitp)

            qs, qis, wis, kts, vts, kits, _, _, _ = _odd_proj(xs, wq32, wkt32, tab_s, tabt_s, 1, ns, tm=ns,
                                                              precise=True)
            kn, vn, kis = kts[0].T, vts[0].T, kits[0].T
            sp, sn = _dsa_step_score(page_table, qis.reshape(ns, IDX_HEADS, IDX_DIM),
                                     wis[:, :IDX_HEADS].reshape(ns, IDX_HEADS, 1), kis.reshape(ns, 1, IDX_DIM),
                                     cki_t, i)
            qbd = (qs.astype(BF16).reshape(ns, N_HEADS, 1, HEAD_DIM)
                   * head_kv[None, :, :, None]).reshape(ns, N_HEADS, KV_DIM)
            a_s = _dsa_step_attend(page_table, sp, sn, qbd, kn.reshape(ns, 1, KV_DIM), vn.reshape(ns, 1, KV_DIM),
                                   ck_t, cv_t, i)
            xs = _proj_ln(a_s.reshape(ns, Q_END), xs, w_out_odd[i], g1, b1)
            k_s.append(kn.reshape(ns, 1, N_KV_HEADS, HEAD_DIM))
            v_s.append(vn.reshape(ns, 1, N_KV_HEADS, HEAD_DIM))
            ki_s.append(kis.reshape(ns, 1, IDX_DIM))

            wr = jnp.pad(router_w[i], ((0, 0), (0, LANES - N_EXPERTS)))
            br = jnp.pad(row(router_b[i]), ((0, 0), (0, LANES - N_EXPERTS)))
            w1, w3, w2 = moe_w1[i].astype(BF16), moe_w3[i].astype(BF16), moe_w2[i].astype(BF16)
            xp = _moe(xp, wr, br, w1, w3, w2, g2, b2, tm=1024)
            xs = _moe(xs, wr, br, moe_w1[i], moe_w3[i], moe_w2[i], g2, b2, tm=ns, precise=True)

    n_odd = len(k_p)
    kv_shape = (n_odd, bp, N_KV_HEADS, HEAD_DIM, tp)
    k_prompt = jnp.stack(k_p).reshape(kv_shape).transpose(0, 1, 4, 2, 3)
    v_prompt = jnp.stack(v_p).reshape(kv_shape).transpose(0, 1, 4, 2, 3)
    kidx_prompt = jnp.stack(ki_p).transpose(0, 1, 3, 2)
    return (xp.reshape(bp, tp, D_MODEL), xs.reshape(ns, 1, D_MODEL),
            k_prompt, v_prompt, kidx_prompt, jnp.stack(conv_p),
            jnp.stack(k_s), jnp.stack(v_s), jnp.stack(ki_s), jnp.stack(conv_s), jnp.stack(chunk_s))
```

```python
import functools

import jax
import jax.numpy as jnp
import numpy as np
from jax import lax
from jax.experimental import pallas as pl
from jax.experimental.pallas import tpu as pltpu

D_MODEL = 1024
DEPTH = 4
PAGE_SIZE = 128
CONV_CH = 512
CONV_W = 3
SGU_CH = 512
SGU_GROUPS = 4
CHUNK = 128
N_HEADS = 16
HEAD_DIM = 64
N_KV_HEADS = 4
GROUP = N_HEADS // N_KV_HEADS
KV_DIM = N_KV_HEADS * HEAD_DIM
ROPE_DIM = 16
ROPE_THETA = 500000.0
IDX_HEADS = 8
IDX_DIM = 64
TOPK_MAX = 256
D_FF = 2816
N_EXPERTS = 8
ALPHA = (2 * DEPTH) ** 0.25
LN_EPS = 1e-5

EVEN_IN = 3 * CONV_CH + 2 * SGU_CH
Q_END = N_HEADS * HEAD_DIM
K_END = Q_END + KV_DIM
V_END = K_END + KV_DIM
QI_END = V_END + IDX_HEADS * IDX_DIM

LANES = 128
VMEM_LIMIT = 56 * 1024 * 1024
KEY_NEG_INF = np.int32(-2139095041)
BF16 = jnp.bfloat16
F32 = jnp.float32


def _params(n_axes):
    return pltpu.CompilerParams(dimension_semantics=("arbitrary",) * n_axes, vmem_limit_bytes=VMEM_LIMIT)


def _const_spec(shape):
    nd = len(shape)
    return pl.BlockSpec(shape, lambda *_: (0,) * nd)


def _ln(x, g, b):
    mu = jnp.mean(x, axis=-1, keepdims=True)
    xc = x - mu
    var = jnp.mean(xc * xc, axis=-1, keepdims=True)
    return xc * lax.rsqrt(var + LN_EPS) * g + b


def _dot(a, b):
    return jnp.dot(a, b, preferred_element_type=F32)


def _dot_nt(a, b):
    return lax.dot_general(a, b, (((1,), (1,)), ((), ())), preferred_element_type=F32)


def _split(a):
    hi = a.astype(BF16)
    return hi, (a - hi.astype(F32)).astype(BF16)


def _dot3(a, b, dot=_dot):
    ah, al = _split(a)
    bh, bl = _split(b)
    return dot(ah, bh) + (dot(ah, bl) + dot(al, bh))


def _silu(x):
    return x * (1.0 / (1.0 + jnp.exp(-x)))


def _float_key(score):
    score = jnp.where(score == 0.0, 0.0, score)
    bits = pltpu.bitcast(score, jnp.int32)
    return jnp.where(bits < 0, bits ^ jnp.int32(0x7FFFFFFF), bits)


def _kth_largest_key(count_ge, rows, k):
    def body(i, lo):
        cand = lo + jnp.left_shift(jnp.int32(1), jnp.int32(31) - i)
        return jnp.where(count_ge(cand) >= k, cand, lo)
    lo0 = jnp.full((rows, 1), np.iinfo(np.int32).min, jnp.int32)
    return lax.fori_loop(0, 32, body, lo0)


def _tie_limit(count_lt, need, rows, n_bits):
    def body(i, p):
        cand = p + jnp.left_shift(jnp.int32(1), jnp.int32(n_bits - 1) - i)
        return jnp.where(count_lt(cand) < need, cand, p)
    return lax.fori_loop(0, n_bits, body, jnp.zeros((rows, 1), jnp.int32))


def _even_mixer_kernel(x_ref, win_ref, cw_ref, sg_ref, sb_ref, wm_ref, bs_ref, wout_ref, g_ref, b_ref,
                       o_ref, cs_ref, zbuf, ybuf, *, tm):
    t = pl.program_id(1)
    x = x_ref[...]
    h = _dot(x.astype(BF16), win_ref[...])
    gate_b = h[:, 0:CONV_CH]
    z = h[:, CONV_CH:2 * CONV_CH] * h[:, 2 * CONV_CH:3 * CONV_CH]
    u = h[:, 3 * CONV_CH:3 * CONV_CH + SGU_CH]
    v = h[:, 3 * CONV_CH + SGU_CH:]

    @pl.when(t == 0)
    def _():
        zbuf[0:8, :] = jnp.zeros((8, CONV_CH), F32)

    zbuf[8:8 + tm, :] = z
    conv = cw_ref[0:1, :] * zbuf[6:6 + tm, :] + cw_ref[1:2, :] * zbuf[7:7 + tm, :] + cw_ref[2:3, :] * z
    ybuf[:, 0:CONV_CH] = (gate_b * conv).astype(BF16)
    cs_ref[0] = zbuf[tm + 6:tm + 8, :]
    zbuf[0:8, :] = zbuf[tm:tm + 8, :]

    vn = _ln(v, sg_ref[...], sb_ref[...])
    row = lax.broadcasted_iota(jnp.int32, (CHUNK, CHUNK), 0)
    col = lax.broadcasted_iota(jnp.int32, (CHUNK, CHUNK), 1)
    for g in range(SGU_GROUPS):
        wm = jnp.where(row >= col, wm_ref[g], 0.0).astype(BF16)
        for c in range(tm // CHUNK):
            rs = slice(c * CHUNK, (c + 1) * CHUNK)
            cs = slice(g * CHUNK, (g + 1) * CHUNK)
            s = _dot(wm, vn[rs, cs].astype(BF16)) + bs_ref[g]
            ybuf[rs, CONV_CH + g * CHUNK:CONV_CH + (g + 1) * CHUNK] = (u[rs, cs] * s).astype(BF16)

    m = _dot(ybuf[...], wout_ref[...])
    o_ref[...] = _ln(ALPHA * x + m, g_ref[...], b_ref[...])


def _even_mixer(x, bsz, t, w_in, conv_w, sgu_g, sgu_bn, sgu_w, bs_b, w_out, g, b, tm=256):
    nt = t // tm
    kern = functools.partial(_even_mixer_kernel, tm=tm)
    return pl.pallas_call(
        kern,
        grid=(bsz, nt),
        in_specs=[
            pl.BlockSpec((tm, D_MODEL), lambda i, j: (i * nt + j, 0)),
            _const_spec((D_MODEL, EVEN_IN)),
            _const_spec((CONV_W, CONV_CH)),
            _const_spec((1, SGU_CH)),
            _const_spec((1, SGU_CH)),
            _const_spec((SGU_GROUPS, CHUNK, CHUNK)),
            _const_spec((SGU_GROUPS, CHUNK, CHUNK)),
            _const_spec((D_MODEL, D_MODEL)),
            _const_spec((1, D_MODEL)),
            _const_spec((1, D_MODEL)),
        ],
        out_specs=[
            pl.BlockSpec((tm, D_MODEL), lambda i, j: (i * nt + j, 0)),
            pl.BlockSpec((1, CONV_W - 1, CONV_CH), lambda i, j: (i, 0, 0)),
        ],
        out_shape=[
            jax.ShapeDtypeStruct((bsz * t, D_MODEL), F32),
            jax.ShapeDtypeStruct((bsz, CONV_W - 1, CONV_CH), F32),
        ],
        scratch_shapes=[pltpu.VMEM((tm + 8, CONV_CH), F32), pltpu.VMEM((tm, D_MODEL), BF16)],
        compiler_params=_params(2),
        name="even_mixer",
    )(x, w_in, conv_w, sgu_g, sgu_bn, sgu_w, bs_b, w_out, g, b)


def _even_mixer_step_kernel(x_ref, p0_ref, p1_ref, win_ref, cw_ref, sg_ref, sb_ref, wd_ref, b0_ref, wout_ref,
                            g_ref, b_ref, o_ref, z_ref, vn_ref):
    x = x_ref[...]
    gate_b, gate_c, h_in, u, v = [_dot3(x, win_ref[:, j * CONV_CH:(j + 1) * CONV_CH]) for j in range(5)]
    z = gate_c * h_in
    conv = cw_ref[0:1, :] * p0_ref[...] + cw_ref[1:2, :] * p1_ref[...] + cw_ref[2:3, :] * z
    vn = _ln(v, sg_ref[...], sb_ref[...])
    s = wd_ref[...] * vn + b0_ref[...]
    y = jnp.concatenate([gate_b * conv, u * s], axis=-1)
    m = _dot3(y, wout_ref[...])
    o_ref[...] = _ln(ALPHA * x + m, g_ref[...], b_ref[...])
    z_ref[...] = z
    vn_ref[...] = vn


def _even_mixer_step(x, p0, p1, w_in, conv_w, sgu_g, sgu_bn, wd, b0, w_out, g, b):
    n = x.shape[0]
    shapes = [(n, D_MODEL), (n, CONV_CH), (n, CONV_CH), (D_MODEL, EVEN_IN), (CONV_W, CONV_CH), (1, SGU_CH),
              (1, SGU_CH), (1, SGU_CH), (1, SGU_CH), (D_MODEL, D_MODEL), (1, D_MODEL), (1, D_MODEL)]
    return pl.pallas_call(
        _even_mixer_step_kernel,
        grid=(1,),
        in_specs=[_const_spec(s) for s in shapes],
        out_specs=[_const_spec((n, D_MODEL)), _const_spec((n, CONV_CH)), _const_spec((n, SGU_CH))],
        out_shape=[jax.ShapeDtypeStruct((n, D_MODEL), F32), jax.ShapeDtypeStruct((n, CONV_CH), F32),
                   jax.ShapeDtypeStruct((n, SGU_CH), F32)],
        compiler_params=_params(1),
        name="even_mixer_step",
    )(x, p0, p1, w_in, conv_w, sgu_g, sgu_bn, wd, b0, w_out, g, b)


FF_TILE = 256


def _ffn_kernel(x_ref, w1_ref, w3_ref, w2_ref, g_ref, b_ref, o_ref, hbuf):
    x = x_ref[...]
    xb = x.astype(BF16)
    for f in range(D_FF // FF_TILE):
        cs = slice(f * FF_TILE, (f + 1) * FF_TILE)
        hbuf[:, cs] = (_silu(_dot(xb, w1_ref[:, cs])) * _dot(xb, w3_ref[:, cs])).astype(BF16)
    y = _dot(hbuf[...], w2_ref[...])
    o_ref[...] = _ln(ALPHA * x + y, g_ref[...], b_ref[...])


def _ffn(x, w1, w3, w2, g, b, tm):
    m = x.shape[0]
    return pl.pallas_call(
        _ffn_kernel,
        grid=(m // tm,),
        in_specs=[
            pl.BlockSpec((tm, D_MODEL), lambda i: (i, 0)),
            _const_spec((D_MODEL, D_FF)),
            _const_spec((D_MODEL, D_FF)),
            _const_spec((D_FF, D_MODEL)),
            _const_spec((1, D_MODEL)),
            _const_spec((1, D_MODEL)),
        ],
        out_specs=pl.BlockSpec((tm, D_MODEL), lambda i: (i, 0)),
        out_shape=jax.ShapeDtypeStruct((m, D_MODEL), F32),
        scratch_shapes=[pltpu.VMEM((tm, D_FF), BF16)],
        compiler_params=_params(1),
        name="ffn",
    )(x, w1, w3, w2, g, b)


def _ffn_step_kernel(x_ref, w1_ref, w3_ref, w2_ref, g_ref, b_ref, o_ref, acc):
    f = pl.program_id(0)

    @pl.when(f == 0)
    def _():
        acc[...] = jnp.zeros(acc.shape, F32)

    x = x_ref[...]
    h = _silu(_dot3(x, w1_ref[...])) * _dot3(x, w3_ref[...])
    acc[...] += _dot3(h, w2_ref[...])

    @pl.when(f == pl.num_programs(0) - 1)
    def _():
        o_ref[...] = _ln(ALPHA * x + acc[...], g_ref[...], b_ref[...])


def _ffn_step(x, w1, w3, w2, g, b):
    n = x.shape[0]
    return pl.pallas_call(
        _ffn_step_kernel,
        grid=(D_FF // FF_TILE,),
        in_specs=[
            _const_spec((n, D_MODEL)),
            pl.BlockSpec((D_MODEL, FF_TILE), lambda f: (0, f)),
            pl.BlockSpec((D_MODEL, FF_TILE), lambda f: (0, f)),
            pl.BlockSpec((FF_TILE, D_MODEL), lambda f: (f, 0)),
            _const_spec((1, D_MODEL)),
            _const_spec((1, D_MODEL)),
        ],
        out_specs=_const_spec((n, D_MODEL)),
        out_shape=jax.ShapeDtypeStruct((n, D_MODEL), F32),
        scratch_shapes=[pltpu.VMEM((n, D_MODEL), F32)],
        compiler_params=_params(1),
        name="ffn_step",
    )(x, w1, w3, w2, g, b)


QQI = Q_END + IDX_HEADS * IDX_DIM
KVT_ROWS = 2 * KV_DIM + IDX_DIM


def _rope(xc, c, sa, sb):
    return xc * c + pltpu.roll(xc, 8, 1) * sa + pltpu.roll(xc, LANES - 8, 1) * sb


def _rope_rows(ht, r0, cos, sin, out_ref, o0, n_rows):
    x1, x2 = ht[r0:r0 + 8, :], ht[r0 + 8:r0 + 16, :]
    out_ref[o0:o0 + 8, :] = x1 * cos - x2 * sin
    out_ref[o0 + 8:o0 + 16, :] = x1 * sin + x2 * cos
    out_ref[o0 + 16:o0 + n_rows, :] = ht[r0 + 16:r0 + n_rows, :]


def _odd_proj_kernel(x_ref, wq_ref, wkt_ref, tab_ref, tabt_ref,
                     q_ref, qi_ref, wi_ref, kt_ref, vt_ref, kit_ref, ktb_ref, vtb_ref, kitb_ref, *, precise):
    if precise:
        x = x_ref[...]
        h = _dot3(x, wq_ref[...])
        ht = _dot3(wkt_ref[...], x, dot=_dot_nt)
    else:
        xb = x_ref[...].astype(BF16)
        h = _dot(xb, wq_ref[...])
        ht = _dot_nt(wkt_ref[...], xb)
    c, sa, sb = tab_ref[:, 0:LANES], tab_ref[:, LANES:2 * LANES], tab_ref[:, 2 * LANES:3 * LANES]
    for j in range(Q_END // LANES):
        cs = slice(j * LANES, (j + 1) * LANES)
        q_ref[:, cs] = (_rope(h[:, cs], c, sa, sb) * (HEAD_DIM ** -0.5)).astype(q_ref.dtype)
    for j in range(IDX_HEADS * IDX_DIM // LANES):
        cs = slice(j * LANES, (j + 1) * LANES)
        qi_ref[:, cs] = (_rope(h[:, Q_END + j * LANES:Q_END + (j + 1) * LANES], c, sa, sb)
                         * (IDX_DIM ** -0.5)).astype(qi_ref.dtype)
    wi_ref[...] = h[:, QQI:QQI + LANES] * (IDX_HEADS ** -0.5)

    cos, sin = tabt_ref[0:8, :], tabt_ref[8:16, :]
    for hd in range(N_KV_HEADS):
        _rope_rows(ht, hd * HEAD_DIM, cos, sin, kt_ref, hd * HEAD_DIM, HEAD_DIM)
    vt_ref[...] = ht[KV_DIM:2 * KV_DIM, :]
    _rope_rows(ht, 2 * KV_DIM, cos, sin, kit_ref, 0, IDX_DIM)
    ktb_ref[...] = kt_ref[...].astype(BF16)
    vtb_ref[...] = vt_ref[...].astype(BF16)
    kitb_ref[...] = kit_ref[...].astype(BF16)


def _odd_proj(x, wq, wkt, tab, tabt, bsz, t, tm, precise=False):
    nt = t // tm
    tok = lambda i: (i, 0)
    feat = lambda i: (i // nt, 0, i % nt)
    q_dtype = F32 if precise else BF16
    row_widths = [(Q_END, q_dtype), (IDX_HEADS * IDX_DIM, q_dtype), (LANES, F32)]
    feat_rows = [(KV_DIM, F32), (KV_DIM, F32), (IDX_DIM, F32), (KV_DIM, BF16), (KV_DIM, BF16), (IDX_DIM, BF16)]
    return pl.pallas_call(
        functools.partial(_odd_proj_kernel, precise=precise),
        grid=(bsz * nt,),
        in_specs=[
            pl.BlockSpec((tm, D_MODEL), tok),
            _const_spec((D_MODEL, QQI + LANES)),
            _const_spec((KVT_ROWS, D_MODEL)),
            pl.BlockSpec((tm, 3 * LANES), lambda i: (i % nt, 0)),
            pl.BlockSpec((ROPE_DIM, tm), lambda i: (0, i % nt)),
        ],
        out_specs=[pl.BlockSpec((tm, w_), tok) for w_, _ in row_widths]
        + [pl.BlockSpec((None, r_, tm), feat) for r_, _ in feat_rows],
        out_shape=[jax.ShapeDtypeStruct((bsz * t, w_), d_) for w_, d_ in row_widths]
        + [jax.ShapeDtypeStruct((bsz, r_, t), d_) for r_, d_ in feat_rows],
        compiler_params=_params(1),
        name="odd_proj",
    )(x, wq, wkt, tab, tabt)


def _rope_tables(pos):
    half = ROPE_DIM // 2
    inv = ROPE_THETA ** (-jnp.arange(half, dtype=F32) / half)
    ang = pos.astype(F32)[:, None] * inv[None, :]
    cos, sin = jnp.cos(ang), jnp.sin(ang)
    n = pos.shape[0]
    one = jnp.ones((n, HEAD_DIM - ROPE_DIM), F32)
    zero = jnp.zeros((n, HEAD_DIM - ROPE_DIM), F32)
    zh = jnp.zeros((n, half), F32)
    c64 = jnp.concatenate([cos, cos, one], axis=-1)
    sa64 = jnp.concatenate([zh, sin, zero], axis=-1)
    sb64 = jnp.concatenate([-sin, zh, zero], axis=-1)
    tab = jnp.concatenate([c64, c64, sa64, sa64, sb64, sb64], axis=-1)
    tabt = jnp.concatenate([cos.T, sin.T], axis=0)
    return tab, tabt


KEY_BLOCK = 512


def _lane_fold(a, op):
    parts = [a[:, c * LANES:(c + 1) * LANES] for c in range(a.shape[1] // LANES)]
    while len(parts) > 1:
        parts = [op(parts[i], parts[i + 1]) for i in range(0, len(parts), 2)]
    return parts[0]


def _dsa_prompt_kernel(x_ref, q_ref, qi_ref, wi_ref, kit_ref, kt_ref, vt_ref, wout_ref, g_ref, b_ref,
                       o_ref, key_ref, bias_ref, lbuf, wib, q3, a3, mbuf, sbuf, obuf, *, tq, topk):
    t = pl.program_id(1)
    n_blk = (t * tq) // KEY_BLOCK + 1

    def blk(kb):
        return pl.ds(pl.multiple_of(kb * KEY_BLOCK, KEY_BLOCK), KEY_BLOCK)

    def bcast(col):
        return jnp.broadcast_to(col, (tq, LANES))

    def chunks(a):
        return [a[:, c * LANES:(c + 1) * LANES] for c in range(KEY_BLOCK // LANES)]

    row = lax.broadcasted_iota(jnp.int32, (tq, KEY_BLOCK), 0)
    lane = lax.broadcasted_iota(jnp.int32, (tq, KEY_BLOCK), 1)

    for h in range(IDX_HEADS):
        wib[h] = bcast(wi_ref[:, h:h + 1])
    for h in range(N_HEADS):
        q3[h // GROUP, (h % GROUP) * tq:(h % GROUP + 1) * tq, :] = q_ref[:, h * HEAD_DIM:(h + 1) * HEAD_DIM]

    def score_body(kb, carry):
        ki = kit_ref[:, blk(kb)]
        score = [jnp.zeros((tq, LANES), F32)] * (KEY_BLOCK // LANES)
        for h in range(IDX_HEADS):
            d = _dot(qi_ref[:, h * IDX_DIM:(h + 1) * IDX_DIM], ki)
            w = wib[h]
            score = [s_ + w * jnp.maximum(d_, 0.0) for s_, d_ in zip(score, chunks(d))]
        adm = kb * KEY_BLOCK + lane <= t * tq + row
        key_ref[:, blk(kb)] = _float_key(jnp.where(adm, jnp.concatenate(score, axis=-1), -jnp.inf))
        return carry

    lax.fori_loop(0, n_blk, score_body, 0)

    def count(pred):
        def body(kb, part):
            for c, k_ in enumerate(chunks(key_ref[:, blk(kb)])):
                part = part + pred(k_, kb * KEY_BLOCK + c * LANES).astype(F32)
            return part
        part = lax.fori_loop(0, n_blk, body, jnp.zeros((tq, LANES), F32))
        return jnp.sum(part, axis=-1, keepdims=True)

    def count_ge(cand):
        cb = bcast(cand)
        return count(lambda k_, i0: k_ >= cb)

    tau = _kth_largest_key(count_ge, tq, float(topk))
    taub = bcast(tau)
    need = float(topk) - count(lambda k_, i0: k_ > taub)
    n_eq = count(lambda k_, i0: k_ == taub)
    crowded = jnp.max(((n_eq > need) & (tau > KEY_NEG_INF)).astype(F32)) > 0.0

    @pl.when(jnp.logical_not(crowded))
    def _():
        def bias_body(kb, carry):
            out = [jnp.where((k_ >= taub) & (k_ > KEY_NEG_INF), 0.0, -jnp.inf)
                   for k_ in chunks(key_ref[:, blk(kb)])]
            bias_ref[:, blk(kb)] = jnp.concatenate(out, axis=-1)
            return carry
        lax.fori_loop(0, n_blk, bias_body, 0)

    @pl.when(crowded)
    def _():
        r_ = lax.broadcasted_iota(jnp.int32, (KEY_BLOCK, KEY_BLOCK), 0)
        c_ = lax.broadcasted_iota(jnp.int32, (KEY_BLOCK, KEY_BLOCK), 1)
        before = jnp.where(r_ < c_, 1.0, 0.0).astype(BF16)
        needb = bcast(need)

        def bias_body(kb, seen):
            ks = chunks(key_ref[:, blk(kb)])
            tie = [jnp.where(k_ == taub, 1.0, 0.0) for k_ in ks]
            rank = chunks(_dot(jnp.concatenate(tie, axis=-1).astype(BF16), before))
            out = []
            for k_, t_, r in zip(ks, tie, rank):
                keep = ((k_ > taub) | ((t_ > 0.0) & (seen + r < needb))) & (k_ > KEY_NEG_INF)
                out.append(jnp.where(keep, 0.0, -jnp.inf))
            bias_ref[:, blk(kb)] = jnp.concatenate(out, axis=-1)
            return seen + bcast(jnp.sum(functools.reduce(jnp.add, tie), axis=-1, keepdims=True))

        lax.fori_loop(0, n_blk, bias_body, jnp.zeros((tq, LANES), F32))

    def group_body(kv, carry):
        qg = q3[kv]
        mbuf[...] = jnp.full(mbuf.shape, -jnp.inf, F32)
        sbuf[...] = jnp.zeros(sbuf.shape, F32)
        obuf[...] = jnp.zeros(obuf.shape, F32)

        def logit_body(kb, c_):
            logit = _dot(qg, kt_ref[kv, :, blk(kb)])
            bias = bias_ref[:, blk(kb)]
            for j in range(GROUP):
                rows = slice(j * tq, (j + 1) * tq)
                lj = logit[rows, :] + bias
                lbuf[rows, blk(kb)] = lj
                mbuf[rows, :] = jnp.maximum(mbuf[rows, :], _lane_fold(lj, jnp.maximum))
            return c_

        lax.fori_loop(0, n_blk, logit_body, 0)
        mbs = [bcast(jnp.max(mbuf[j * tq:(j + 1) * tq, :], axis=-1, keepdims=True)) for j in range(GROUP)]

        def pv_body(kb, c_):
            pj = []
            for j in range(GROUP):
                rows = slice(j * tq, (j + 1) * tq)
                ps = [jnp.exp(l_ - mbs[j]) for l_ in chunks(lbuf[rows, blk(kb)])]
                sbuf[rows, :] = sbuf[rows, :] + functools.reduce(jnp.add, ps)
                pj.append(jnp.concatenate(ps, axis=-1).astype(BF16))
            obuf[...] += _dot_nt(jnp.concatenate(pj, axis=0), vt_ref[kv, :, blk(kb)])
            return c_

        lax.fori_loop(0, n_blk, pv_body, 0)
        a3[kv] = (obuf[...] / jnp.sum(sbuf[...], axis=-1, keepdims=True)).astype(BF16)
        return carry

    lax.fori_loop(0, N_KV_HEADS, group_body, 0)

    a = jnp.concatenate([a3[h // GROUP, (h % GROUP) * tq:(h % GROUP + 1) * tq, :] for h in range(N_HEADS)], axis=-1)
    m = _dot(a, wout_ref[...])
    o_ref[...] = _ln(ALPHA * x_ref[...] + m, g_ref[...], b_ref[...])


def _dsa_prompt(x, q, qi, wi, kit, kt, vt, w_out, g, b, bsz, t, tq=256):
    nq = t // tq
    topk = min(TOPK_MAX, t // 4)
    kern = functools.partial(_dsa_prompt_kernel, tq=tq, topk=topk)
    qmap = lambda i, j: (i * nq + j, 0)
    return pl.pallas_call(
        kern,
        grid=(bsz, nq),
        in_specs=[
            pl.BlockSpec((tq, D_MODEL), qmap),
            pl.BlockSpec((tq, Q_END), qmap),
            pl.BlockSpec((tq, IDX_HEADS * IDX_DIM), qmap),
            pl.BlockSpec((tq, LANES), qmap),
            pl.BlockSpec((None, IDX_DIM, t), lambda i, j: (i, 0, 0)),
            pl.BlockSpec((None, N_KV_HEADS, HEAD_DIM, t), lambda i, j: (i, 0, 0, 0)),
            pl.BlockSpec((None, N_KV_HEADS, HEAD_DIM, t), lambda i, j: (i, 0, 0, 0)),
            _const_spec((Q_END, D_MODEL)),
            _const_spec((1, D_MODEL)),
            _const_spec((1, D_MODEL)),
        ],
        out_specs=pl.BlockSpec((tq, D_MODEL), qmap),
        out_shape=jax.ShapeDtypeStruct((bsz * t, D_MODEL), F32),
        scratch_shapes=[
            pltpu.VMEM((tq, t), jnp.int32),
            pltpu.VMEM((tq, t), F32),
            pltpu.VMEM((GROUP * tq, t), F32),
            pltpu.VMEM((IDX_HEADS, tq, LANES), F32),
            pltpu.VMEM((N_KV_HEADS, GROUP * tq, HEAD_DIM), BF16),
            pltpu.VMEM((N_KV_HEADS, GROUP * tq, HEAD_DIM), BF16),
            pltpu.VMEM((GROUP * tq, LANES), F32),
            pltpu.VMEM((GROUP * tq, LANES), F32),
            pltpu.VMEM((GROUP * tq, HEAD_DIM), F32),
        ],
        compiler_params=_params(2),
        name="dsa_prompt",
    )(x, q, qi, wi, kit, kt, vt, w_out, g, b)


IDX_PAGES_PER_STEP = 32
KV_PAGES_PER_STEP = 16


def _page_specs(layer, n_rows, pages_per_step):
    def spec(j):
        return pl.BlockSpec((None, None, n_rows, PAGE_SIZE),
                            lambda i, s, pt: (layer, pt[i, s * pages_per_step + j], 0, 0))
    return [spec(j) for j in range(pages_per_step)]


def _dsa_step_score_kernel(pt_ref, qi_ref, wi_ref, kin_ref, *rest):
    pages = rest[:IDX_PAGES_PER_STEP]
    o_ref, on_ref, kbuf_hi, kbuf_lo = rest[IDX_PAGES_PER_STEP:]
    for j, p in enumerate(pages):
        hi, lo = _split(p[...])
        kbuf_hi[:, j * PAGE_SIZE:(j + 1) * PAGE_SIZE] = hi
        kbuf_lo[:, j * PAGE_SIZE:(j + 1) * PAGE_SIZE] = lo
    qi = qi_ref[0]
    wi = wi_ref[0]
    qh, ql = _split(qi)
    d = _dot(qh, kbuf_hi[...]) + (_dot(qh, kbuf_lo[...]) + _dot(ql, kbuf_hi[...]))
    o_ref[0] = jnp.sum(wi * jnp.maximum(d, 0.0), axis=0, keepdims=True)
    dn = jnp.sum(qi * kin_ref[0], axis=-1, keepdims=True)
    sn = jnp.sum(wi * jnp.maximum(dn, 0.0), axis=0, keepdims=True)
    on_ref[0] = jnp.broadcast_to(sn, (1, LANES))


def _dsa_step_score(page_table, qi3, wi3, kin3, cache_kidx_t, layer):
    n, n_pages = page_table.shape
    steps = n_pages // IDX_PAGES_PER_STEP
    step_keys = IDX_PAGES_PER_STEP * PAGE_SIZE
    row = lambda i, s, pt: (i, 0, 0)
    grid_spec = pltpu.PrefetchScalarGridSpec(
        num_scalar_prefetch=1,
        grid=(n, steps),
        in_specs=[
            pl.BlockSpec((1, IDX_HEADS, IDX_DIM), row),
            pl.BlockSpec((1, IDX_HEADS, 1), row),
            pl.BlockSpec((1, 1, IDX_DIM), row),
        ] + _page_specs(layer, IDX_DIM, IDX_PAGES_PER_STEP),
        out_specs=[
            pl.BlockSpec((1, 1, step_keys), lambda i, s, pt: (i, 0, s)),
            pl.BlockSpec((1, 1, LANES), row),
        ],
        scratch_shapes=[pltpu.VMEM((IDX_DIM, step_keys), BF16), pltpu.VMEM((IDX_DIM, step_keys), BF16)],
    )
    return pl.pallas_call(
        _dsa_step_score_kernel,
        grid_spec=grid_spec,
        out_shape=[jax.ShapeDtypeStruct((n, 1, n_pages * PAGE_SIZE), F32),
                   jax.ShapeDtypeStruct((n, 1, LANES), F32)],
        compiler_params=_params(2),
        name="dsa_step_score",
    )(page_table, qi3, wi3, kin3, *([cache_kidx_t] * IDX_PAGES_PER_STEP))


def _dsa_step_select_kernel(sp_ref, sn_ref, bias_ref, biasn_ref, key_ref, *, topk):
    n, past = sp_ref.shape
    lane = lax.broadcasted_iota(jnp.int32, (n, LANES), 1)
    key_ref[...] = _float_key(sp_ref[...])
    key_n = jnp.where(lane == 0, _float_key(sn_ref[...]), KEY_NEG_INF - 1)

    def count_ge(cand):
        return (jnp.sum((key_ref[...] >= cand).astype(F32), axis=-1, keepdims=True)
                + jnp.sum((key_n >= cand).astype(F32), axis=-1, keepdims=True))

    tau = _kth_largest_key(count_ge, n, float(topk))
    need = (float(topk) - jnp.sum((key_ref[...] > tau).astype(F32), axis=-1, keepdims=True)
            - jnp.sum((key_n > tau).astype(F32), axis=-1, keepdims=True))

    def count_lt(p):
        idx = lax.broadcasted_iota(jnp.int32, (n, past), 1)
        return jnp.sum(((key_ref[...] == tau) & (idx < p)).astype(F32), axis=-1, keepdims=True)

    lim = _tie_limit(count_lt, need, n, int(past).bit_length())
    key_p = key_ref[...]
    idx = lax.broadcasted_iota(jnp.int32, (n, past), 1)
    keep_p = (key_p > tau) | ((key_p == tau) & (idx <= lim))
    n_keep = jnp.sum(keep_p.astype(F32), axis=-1, keepdims=True)
    keep_n = (key_n > tau) | ((key_n == tau) & (n_keep < float(topk)))
    bias_ref[...] = jnp.where(keep_p, 0.0, -jnp.inf)
    biasn_ref[...] = jnp.where(keep_n & (lane == 0), 0.0, -jnp.inf)


def _dsa_step_select(sp, sn):
    n, past = sp.shape
    topk = min(TOPK_MAX, (past + 1) // 4)
    return pl.pallas_call(
        functools.partial(_dsa_step_select_kernel, topk=topk),
        grid=(1,),
        in_specs=[_const_spec((n, past)), _const_spec((n, LANES))],
        out_specs=[_const_spec((n, past)), _const_spec((n, LANES))],
        out_shape=[jax.ShapeDtypeStruct((n, past), F32), jax.ShapeDtypeStruct((n, LANES), F32)],
        scratch_shapes=[pltpu.VMEM((n, past), jnp.int32)],
        compiler_params=_params(1),
        name="dsa_step_select",
    )(sp, sn)


def _dsa_step_attend_kernel(pt_ref, bias_ref, biasn_ref, qbd_ref, kn_ref, vn_ref, *rest):
    kp = rest[:KV_PAGES_PER_STEP]
    vp = rest[KV_PAGES_PER_STEP:2 * KV_PAGES_PER_STEP]
    o_ref, kbuf, vbuf, m_ref, l_ref, acc_ref = rest[2 * KV_PAGES_PER_STEP:]
    s = pl.program_id(1)
    n_steps = pl.num_programs(1)
    step_keys = KV_PAGES_PER_STEP * PAGE_SIZE

    @pl.when(s == 0)
    def _():
        m_ref[...] = jnp.full(m_ref.shape, -jnp.inf, F32)
        l_ref[...] = jnp.zeros(l_ref.shape, F32)
        acc_ref[...] = jnp.zeros(acc_ref.shape, F32)

    for j in range(KV_PAGES_PER_STEP):
        kbuf[:, j * PAGE_SIZE:(j + 1) * PAGE_SIZE] = kp[j][...].astype(BF16)
        vbuf[:, j * PAGE_SIZE:(j + 1) * PAGE_SIZE] = vp[j][...].astype(BF16)
    qbd = qbd_ref[0]
    logit = _dot(qbd, kbuf[...]) + bias_ref[0]
    m_new = jnp.maximum(m_ref[...], jnp.max(logit, axis=-1, keepdims=True))
    m_safe = jnp.where(m_new == -jnp.inf, 0.0, m_new)
    scale = jnp.exp(m_ref[...] - m_safe)
    p = jnp.exp(logit - m_safe)
    l_ref[...] = l_ref[...] * scale + jnp.sum(p, axis=-1, keepdims=True)
    acc_ref[...] = acc_ref[...] * scale + _dot_nt(p.astype(BF16), vbuf[...])
    m_ref[...] = m_new

    @pl.when(s == n_steps - 1)
    def _():
        kn = kn_ref[0].astype(BF16).astype(F32)
        vn = vn_ref[0].astype(BF16).astype(F32)
        ln_ = jnp.sum(qbd.astype(F32) * kn, axis=-1, keepdims=True) + biasn_ref[0][:, 0:1]
        m_old = m_ref[...]
        m_new2 = jnp.maximum(m_old, ln_)
        sc = jnp.exp(m_old - m_new2)
        pn = jnp.exp(ln_ - m_new2)
        den = l_ref[...] * sc + pn
        r = (acc_ref[...] * sc + pn.astype(BF16).astype(F32) * vn) / den
        pieces = [r[h:h + 1, (h // GROUP) * HEAD_DIM:(h // GROUP + 1) * HEAD_DIM] for h in range(N_HEADS)]
        o_ref[0] = jnp.concatenate(pieces, axis=-1)


def _dsa_step_attend(page_table, bias3, biasn3, qbd, kn3, vn3, cache_k_t, cache_v_t, layer):
    n, n_pages = page_table.shape
    steps = n_pages // KV_PAGES_PER_STEP
    step_keys = KV_PAGES_PER_STEP * PAGE_SIZE
    page_specs = _page_specs(layer, KV_DIM, KV_PAGES_PER_STEP)
    row = lambda i, s, pt: (i, 0, 0)
    grid_spec = pltpu.PrefetchScalarGridSpec(
        num_scalar_prefetch=1,
        grid=(n, steps),
        in_specs=[
            pl.BlockSpec((1, 1, step_keys), lambda i, s, pt: (i, 0, s)),
            pl.BlockSpec((1, 1, LANES), row),
            pl.BlockSpec((1, N_HEADS, KV_DIM), row),
            pl.BlockSpec((1, 1, KV_DIM), row),
            pl.BlockSpec((1, 1, KV_DIM), row),
        ] + page_specs + page_specs,
        out_specs=pl.BlockSpec((1, 1, Q_END), row),
        scratch_shapes=[
            pltpu.VMEM((KV_DIM, step_keys), BF16), pltpu.VMEM((KV_DIM, step_keys), BF16),
            pltpu.VMEM((N_HEADS, 1), F32), pltpu.VMEM((N_HEADS, 1), F32), pltpu.VMEM((N_HEADS, KV_DIM), F32),
        ],
    )
    return pl.pallas_call(
        _dsa_step_attend_kernel,
        grid_spec=grid_spec,
        out_shape=jax.ShapeDtypeStruct((n, 1, Q_END), F32),
        compiler_params=_params(2),
        name="dsa_step_attend",
    )(page_table, bias3, biasn3, qbd, kn3, vn3,
      *([cache_k_t] * KV_PAGES_PER_STEP), *([cache_v_t] * KV_PAGES_PER_STEP))


def _proj_ln_kernel(a_ref, x_ref, w_ref, g_ref, b_ref, o_ref):
    m = _dot3(a_ref[...], w_ref[...])
    o_ref[...] = _ln(ALPHA * x_ref[...] + m, g_ref[...], b_ref[...])


def _proj_ln(a, x, w, g, b):
    n = x.shape[0]
    shapes = [(n, a.shape[1]), (n, D_MODEL), w.shape, (1, D_MODEL), (1, D_MODEL)]
    return pl.pallas_call(
        _proj_ln_kernel,
        grid=(1,),
        in_specs=[_const_spec(s) for s in shapes],
        out_specs=_const_spec((n, D_MODEL)),
        out_shape=jax.ShapeDtypeStruct((n, D_MODEL), F32),
        compiler_params=_params(1),
        name="proj_ln",
    )(a, x, w, g, b)


def _moe_kernel(x_ref, wr_ref, br_ref, w1_ref, w3_ref, w2_ref, g_ref, b_ref, o_ref, acc, comb, xb, *, precise):
    e = pl.program_id(1)
    tm = x_ref.shape[0]
    lane = lax.broadcasted_iota(jnp.int32, (tm, LANES), 1)

    @pl.when(e == 0)
    def _():
        x = x_ref[...]
        logits = jnp.dot(x, wr_ref[...], preferred_element_type=F32, precision=lax.Precision.HIGHEST)
        logits = jnp.where(lane < N_EXPERTS, logits + br_ref[...], -jnp.inf)
        lane_f = lane.astype(F32)
        m1 = jnp.max(logits, axis=-1, keepdims=True)
        i1 = jnp.min(jnp.where(logits == m1, lane_f, float(LANES)), axis=-1, keepdims=True)
        rest = jnp.where(lane_f == i1, -jnp.inf, logits)
        m2 = jnp.max(rest, axis=-1, keepdims=True)
        i2 = jnp.min(jnp.where(rest == m2, lane_f, float(LANES)), axis=-1, keepdims=True)
        e2 = jnp.exp(m2 - m1)
        den = 1.0 + e2
        comb[...] = jnp.where(lane_f == i1, 1.0 / den, 0.0) + jnp.where(lane_f == i2, e2 / den, 0.0)
        xb[...] = x.astype(xb.dtype)
        acc[...] = jnp.zeros(acc.shape, F32)

    c = jnp.sum(jnp.where(lane == e, comb[...], 0.0), axis=-1, keepdims=True)
    if precise:
        h = _silu(_dot3(xb[...], w1_ref[0])) * _dot3(xb[...], w3_ref[0])
        acc[...] += _dot3(c * h, w2_ref[0])
    else:
        h = _silu(_dot(xb[...], w1_ref[0])) * _dot(xb[...], w3_ref[0])
        acc[...] += _dot((c * h).astype(BF16), w2_ref[0])

    @pl.when(e == N_EXPERTS - 1)
    def _():
        o_ref[...] = _ln(ALPHA * x_ref[...] + acc[...], g_ref[...], b_ref[...])


def _moe(x, wr, br, w1, w3, w2, g, b, tm, precise=False):
    m = x.shape[0]
    wspec = pl.BlockSpec((1, D_MODEL, D_MODEL), lambda i, e: (e, 0, 0))
    return pl.pallas_call(
        functools.partial(_moe_kernel, precise=precise),
        grid=(m // tm, N_EXPERTS),
        in_specs=[
            pl.BlockSpec((tm, D_MODEL), lambda i, e: (i, 0)),
            _const_spec((D_MODEL, LANES)),
            _const_spec((1, LANES)),
            wspec, wspec, wspec,
            _const_spec((1, D_MODEL)),
            _const_spec((1, D_MODEL)),
        ],
        out_specs=pl.BlockSpec((tm, D_MODEL), lambda i, e: (i, 0)),
        out_shape=jax.ShapeDtypeStruct((m, D_MODEL), F32),
        scratch_shapes=[pltpu.VMEM((tm, D_MODEL), F32), pltpu.VMEM((tm, LANES), F32),
                        pltpu.VMEM((tm, D_MODEL), F32 if precise else BF16)],
        compiler_params=_params(2),
        name="moe",
    )(x, wr, br, w1, w3, w2, g, b)


def kernel(x_prompt, x_sample, cache_k, cache_v, cache_kidx, state_conv, page_table, ln1_g, ln1_b, ln2_g, ln2_b,
           w_in_even, conv_w, sgu_ln_g, sgu_ln_b, sgu_w, sgu_b, w_out_even, ffn_w1, ffn_w3, ffn_w2, w_in_odd,
           w_out_odd, router_w, router_b, moe_w1, moe_w3, moe_w2):
    bp, tp, _ = x_prompt.shape
    ns = x_sample.shape[0]
    n_pool = cache_k.shape[1]
    past = page_table.shape[1] * PAGE_SIZE

    xp = x_prompt.reshape(bp * tp, D_MODEL)
    xs = x_sample.reshape(ns, D_MODEL)
    tab_p, tabt_p = _rope_tables(jnp.arange(tp))
    tab_s, tabt_s = _rope_tables(jnp.full((ns,), past, jnp.int32))
    ck_t = jnp.transpose(cache_k, (0, 1, 3, 4, 2)).reshape(-1, n_pool, KV_DIM, PAGE_SIZE)
    cv_t = jnp.transpose(cache_v, (0, 1, 3, 4, 2)).reshape(-1, n_pool, KV_DIM, PAGE_SIZE)
    cki_t = jnp.swapaxes(cache_kidx, 2, 3)
    row = lambda a: a.reshape(1, -1)
    head_kv = (jnp.arange(N_HEADS)[:, None] // GROUP == jnp.arange(N_KV_HEADS)[None, :]).astype(BF16)

    k_p, v_p, ki_p, conv_p = [], [], [], []
    k_s, v_s, ki_s, conv_s, chunk_s = [], [], [], [], []
    for layer in range(DEPTH):
        i = layer // 2
        g1, b1, g2, b2 = row(ln1_g[layer]), row(ln1_b[layer]), row(ln2_g[layer]), row(ln2_b[layer])
        if layer % 2 == 0:
            w_in = w_in_even[i].astype(BF16)
            w_out = w_out_even[i].astype(BF16)
            sg, sbn = row(sgu_ln_g[i]), row(sgu_ln_b[i])
            bs_b = jnp.broadcast_to(sgu_b[i][:, :, None], (SGU_GROUPS, CHUNK, CHUNK))
            xp, cp = _even_mixer(xp, bp, tp, w_in, conv_w[i], sg, sbn, sgu_w[i], bs_b, w_out, g1, b1)
            wd = row(jnp.repeat(sgu_w[i][:, 0, 0], CHUNK))
            b0 = row(jnp.repeat(sgu_b[i][:, 0], CHUNK))
            xs, zs, vs = _even_mixer_step(xs, state_conv[i][:, 0], state_conv[i][:, 1], w_in_even[i], conv_w[i],
                                          sg, sbn, wd, b0, w_out_even[i], g1, b1)
            conv_p.append(cp)
            conv_s.append(jnp.stack([state_conv[i][:, 1], zs], axis=1))
            chunk_s.append(vs.reshape(ns, 1, SGU_CH))
            w1, w3, w2 = ffn_w1[i].astype(BF16), ffn_w3[i].astype(BF16), ffn_w2[i].astype(BF16)
            xp = _ffn(xp, w1, w3, w2, g2, b2, tm=512)
            xs = _ffn_step(xs, ffn_w1[i], ffn_w3[i], ffn_w2[i], g2, b2)
        else:
            w = w_in_odd[i]
            wq32 = jnp.concatenate(
                [w[:, :Q_END], w[:, V_END:QI_END],
                 jnp.pad(w[:, QI_END + IDX_DIM:], ((0, 0), (0, LANES - IDX_HEADS)))], axis=1)
            wkt32 = jnp.concatenate([w[:, Q_END:V_END], w[:, QI_END:QI_END + IDX_DIM]], axis=1).T
            wq, wkt = wq32.astype(BF16), wkt32.astype(BF16)
            w_out = w_out_odd[i].astype(BF16)
            qp, qip, wip, ktp, vtp, kitp, ktb, vtb, kitb = _odd_proj(xp, wq, wkt, tab_p, tabt_p, bp, tp, tm=512)
            head_major = (bp, N_KV_HEADS, HEAD_DIM, tp)
            xp = _dsa_prompt(xp, qp, qip, wip, kitb, ktb.reshape(head_major), vtb.reshape(head_major), w_out,
                             g1, b1, bp, tp)
            k_p.append(ktp)
            v_p.append(vtp)
            ki_p.append(kitp)

            qs, qis, wis, kts, vts, kits, _, _, _ = _odd_proj(xs, wq32, wkt32, tab_s, tabt_s, 1, ns, tm=ns,
                                                              precise=True)
            kn, vn, kis = kts[0].T, vts[0].T, kits[0].T
            sp, sn = _dsa_step_score(page_table, qis.reshape(ns, IDX_HEADS, IDX_DIM),
                                     wis[:, :IDX_HEADS].reshape(ns, IDX_HEADS, 1), kis.reshape(ns, 1, IDX_DIM),
                                     cki_t, i)
            qbd = (qs.astype(BF16).reshape(ns, N_HEADS, 1, HEAD_DIM)
                   * head_kv[None, :, :, None]).reshape(ns, N_HEADS, KV_DIM)
            bias_s, biasn_s = _dsa_step_select(sp.reshape(ns, past), sn.reshape(ns, LANES))
            a_s = _dsa_step_attend(page_table, bias_s.reshape(ns, 1, past), biasn_s.reshape(ns, 1, LANES), qbd,
                                   kn.reshape(ns, 1, KV_DIM), vn.reshape(ns, 1, KV_DIM), ck_t, cv_t, i)
            xs = _proj_ln(a_s.reshape(ns, Q_END), xs, w_out_odd[i], g1, b1)
            k_s.append(kn.reshape(ns, 1, N_KV_HEADS, HEAD_DIM))
            v_s.append(vn.reshape(ns, 1, N_KV_HEADS, HEAD_DIM))
            ki_s.append(kis.reshape(ns, 1, IDX_DIM))

            wr = jnp.pad(router_w[i], ((0, 0), (0, LANES - N_EXPERTS)))
            br = jnp.pad(row(router_b[i]), ((0, 0), (0, LANES - N_EXPERTS)))
            w1, w3, w2 = moe_w1[i].astype(BF16), moe_w3[i].astype(BF16), moe_w2[i].astype(BF16)
            xp = _moe(xp, wr, br, w1, w3, w2, g2, b2, tm=1024)
            xs = _moe(xs, wr, br, moe_w1[i], moe_w3[i], moe_w2[i], g2, b2, tm=ns, precise=True)

    n_odd = len(k_p)
    kv_shape = (n_odd, bp, N_KV_HEADS, HEAD_DIM, tp)
    k_prompt = jnp.stack(k_p).reshape(kv_shape).transpose(0, 1, 4, 2, 3)
    v_prompt = jnp.stack(v_p).reshape(kv_shape).transpose(0, 1, 4, 2, 3)
    kidx_prompt = jnp.stack(ki_p).transpose(0, 1, 3, 2)
    return (xp.reshape(bp, tp, D_MODEL), xs.reshape(ns, 1, D_MODEL),
            k_prompt, v_prompt, kidx_prompt, jnp.stack(conv_p),
            jnp.stack(k_s), jnp.stack(v_s), jnp.stack(ki_s), jnp.stack(conv_s), jnp.stack(chunk_s))
```

```python
import functools

import jax
import jax.numpy as jnp
import numpy as np
from jax import lax
from jax.experimental import pallas as pl
from jax.experimental.pallas import tpu as pltpu

D_MODEL = 1024
DEPTH = 4
PAGE_SIZE = 128
CONV_CH = 512
CONV_W = 3
SGU_CH = 512
SGU_GROUPS = 4
CHUNK = 128
N_HEADS = 16
HEAD_DIM = 64
N_KV_HEADS = 4
GROUP = N_HEADS // N_KV_HEADS
KV_DIM = N_KV_HEADS * HEAD_DIM
ROPE_DIM = 16
ROPE_THETA = 500000.0
IDX_HEADS = 8
IDX_DIM = 64
TOPK_MAX = 256
D_FF = 2816
N_EXPERTS = 8
ALPHA = (2 * DEPTH) ** 0.25
LN_EPS = 1e-5

EVEN_IN = 3 * CONV_CH + 2 * SGU_CH
Q_END = N_HEADS * HEAD_DIM
K_END = Q_END + KV_DIM
V_END = K_END + KV_DIM
QI_END = V_END + IDX_HEADS * IDX_DIM

LANES = 128
VMEM_LIMIT = 56 * 1024 * 1024
KEY_NEG_INF = np.int32(-2139095041)
BF16 = jnp.bfloat16
F32 = jnp.float32


def _params(n_axes):
    return pltpu.CompilerParams(dimension_semantics=("arbitrary",) * n_axes, vmem_limit_bytes=VMEM_LIMIT)


def _const_spec(shape):
    nd = len(shape)
    return pl.BlockSpec(shape, lambda *_: (0,) * nd)


def _ln(x, g, b):
    mu = jnp.mean(x, axis=-1, keepdims=True)
    xc = x - mu
    var = jnp.mean(xc * xc, axis=-1, keepdims=True)
    return xc * lax.rsqrt(var + LN_EPS) * g + b


def _dot(a, b):
    return jnp.dot(a, b, preferred_element_type=F32)


def _dot_nt(a, b):
    return lax.dot_general(a, b, (((1,), (1,)), ((), ())), preferred_element_type=F32)


def _split(a):
    hi = a.astype(BF16)
    return hi, (a - hi.astype(F32)).astype(BF16)


def _dot3(a, b, dot=_dot):
    ah, al = _split(a)
    bh, bl = _split(b)
    return dot(ah, bh) + (dot(ah, bl) + dot(al, bh))


def _silu(x):
    return x * (1.0 / (1.0 + jnp.exp(-x)))


def _float_key(score):
    score = jnp.where(score == 0.0, 0.0, score)
    bits = pltpu.bitcast(score, jnp.int32)
    return jnp.where(bits < 0, bits ^ jnp.int32(0x7FFFFFFF), bits)


def _kth_largest_key(count_ge, rows, k):
    def body(i, lo):
        cand = lo + jnp.left_shift(jnp.int32(1), jnp.int32(31) - i)
        return jnp.where(count_ge(cand) >= k, cand, lo)
    lo0 = jnp.full((rows, 1), np.iinfo(np.int32).min, jnp.int32)
    return lax.fori_loop(0, 32, body, lo0)


def _tie_limit(count_lt, need, rows, n_bits):
    def body(i, p):
        cand = p + jnp.left_shift(jnp.int32(1), jnp.int32(n_bits - 1) - i)
        return jnp.where(count_lt(cand) < need, cand, p)
    return lax.fori_loop(0, n_bits, body, jnp.zeros((rows, 1), jnp.int32))


def _even_mixer_kernel(x_ref, win_ref, cw_ref, sg_ref, sb_ref, wm_ref, bs_ref, wout_ref, g_ref, b_ref,
                       o_ref, cs_ref, zbuf, ybuf, *, tm):
    t = pl.program_id(1)
    x = x_ref[...]
    h = _dot(x.astype(BF16), win_ref[...])
    gate_b = h[:, 0:CONV_CH]
    z = h[:, CONV_CH:2 * CONV_CH] * h[:, 2 * CONV_CH:3 * CONV_CH]
    u = h[:, 3 * CONV_CH:3 * CONV_CH + SGU_CH]
    v = h[:, 3 * CONV_CH + SGU_CH:]

    @pl.when(t == 0)
    def _():
        zbuf[0:8, :] = jnp.zeros((8, CONV_CH), F32)

    zbuf[8:8 + tm, :] = z
    conv = cw_ref[0:1, :] * zbuf[6:6 + tm, :] + cw_ref[1:2, :] * zbuf[7:7 + tm, :] + cw_ref[2:3, :] * z
    ybuf[:, 0:CONV_CH] = (gate_b * conv).astype(BF16)
    cs_ref[0] = zbuf[tm + 6:tm + 8, :]
    zbuf[0:8, :] = zbuf[tm:tm + 8, :]

    vn = _ln(v, sg_ref[...], sb_ref[...])
    row = lax.broadcasted_iota(jnp.int32, (CHUNK, CHUNK), 0)
    col = lax.broadcasted_iota(jnp.int32, (CHUNK, CHUNK), 1)
    for g in range(SGU_GROUPS):
        wm = jnp.where(row >= col, wm_ref[g], 0.0).astype(BF16)
        for c in range(tm // CHUNK):
            rs = slice(c * CHUNK, (c + 1) * CHUNK)
            cs = slice(g * CHUNK, (g + 1) * CHUNK)
            s = _dot(wm, vn[rs, cs].astype(BF16)) + bs_ref[g]
            ybuf[rs, CONV_CH + g * CHUNK:CONV_CH + (g + 1) * CHUNK] = (u[rs, cs] * s).astype(BF16)

    m = _dot(ybuf[...], wout_ref[...])
    o_ref[...] = _ln(ALPHA * x + m, g_ref[...], b_ref[...])


def _even_mixer(x, bsz, t, w_in, conv_w, sgu_g, sgu_bn, sgu_w, bs_b, w_out, g, b, tm=256):
    nt = t // tm
    kern = functools.partial(_even_mixer_kernel, tm=tm)
    return pl.pallas_call(
        kern,
        grid=(bsz, nt),
        in_specs=[
            pl.BlockSpec((tm, D_MODEL), lambda i, j: (i * nt + j, 0)),
            _const_spec((D_MODEL, EVEN_IN)),
            _const_spec((CONV_W, CONV_CH)),
            _const_spec((1, SGU_CH)),
            _const_spec((1, SGU_CH)),
            _const_spec((SGU_GROUPS, CHUNK, CHUNK)),
            _const_spec((SGU_GROUPS, CHUNK, CHUNK)),
            _const_spec((D_MODEL, D_MODEL)),
            _const_spec((1, D_MODEL)),
            _const_spec((1, D_MODEL)),
        ],
        out_specs=[
            pl.BlockSpec((tm, D_MODEL), lambda i, j: (i * nt + j, 0)),
            pl.BlockSpec((1, CONV_W - 1, CONV_CH), lambda i, j: (i, 0, 0)),
        ],
        out_shape=[
            jax.ShapeDtypeStruct((bsz * t, D_MODEL), F32),
            jax.ShapeDtypeStruct((bsz, CONV_W - 1, CONV_CH), F32),
        ],
        scratch_shapes=[pltpu.VMEM((tm + 8, CONV_CH), F32), pltpu.VMEM((tm, D_MODEL), BF16)],
        compiler_params=_params(2),
        name="even_mixer",
    )(x, w_in, conv_w, sgu_g, sgu_bn, sgu_w, bs_b, w_out, g, b)


def _even_mixer_step_kernel(x_ref, p0_ref, p1_ref, win_ref, cw_ref, sg_ref, sb_ref, wd_ref, b0_ref, wout_ref,
                            g_ref, b_ref, o_ref, z_ref, vn_ref):
    x = x_ref[...]
    gate_b, gate_c, h_in, u, v = [_dot3(x, win_ref[:, j * CONV_CH:(j + 1) * CONV_CH]) for j in range(5)]
    z = gate_c * h_in
    conv = cw_ref[0:1, :] * p0_ref[...] + cw_ref[1:2, :] * p1_ref[...] + cw_ref[2:3, :] * z
    vn = _ln(v, sg_ref[...], sb_ref[...])
    s = wd_ref[...] * vn + b0_ref[...]
    y = jnp.concatenate([gate_b * conv, u * s], axis=-1)
    m = _dot3(y, wout_ref[...])
    o_ref[...] = _ln(ALPHA * x + m, g_ref[...], b_ref[...])
    z_ref[...] = z
    vn_ref[...] = vn


def _even_mixer_step(x, p0, p1, w_in, conv_w, sgu_g, sgu_bn, wd, b0, w_out, g, b):
    n = x.shape[0]
    shapes = [(n, D_MODEL), (n, CONV_CH), (n, CONV_CH), (D_MODEL, EVEN_IN), (CONV_W, CONV_CH), (1, SGU_CH),
              (1, SGU_CH), (1, SGU_CH), (1, SGU_CH), (D_MODEL, D_MODEL), (1, D_MODEL), (1, D_MODEL)]
    return pl.pallas_call(
        _even_mixer_step_kernel,
        grid=(1,),
        in_specs=[_const_spec(s) for s in shapes],
        out_specs=[_const_spec((n, D_MODEL)), _const_spec((n, CONV_CH)), _const_spec((n, SGU_CH))],
        out_shape=[jax.ShapeDtypeStruct((n, D_MODEL), F32), jax.ShapeDtypeStruct((n, CONV_CH), F32),
                   jax.ShapeDtypeStruct((n, SGU_CH), F32)],
        compiler_params=_params(1),
        name="even_mixer_step",
    )(x, p0, p1, w_in, conv_w, sgu_g, sgu_bn, wd, b0, w_out, g, b)


FF_TILE = 256


def _ffn_kernel(x_ref, w1_ref, w3_ref, w2_ref, g_ref, b_ref, o_ref, hbuf):
    x = x_ref[...]
    xb = x.astype(BF16)
    for f in range(D_FF // FF_TILE):
        cs = slice(f * FF_TILE, (f + 1) * FF_TILE)
        hbuf[:, cs] = (_silu(_dot(xb, w1_ref[:, cs])) * _dot(xb, w3_ref[:, cs])).astype(BF16)
    y = _dot(hbuf[...], w2_ref[...])
    o_ref[...] = _ln(ALPHA * x + y, g_ref[...], b_ref[...])


def _ffn(x, w1, w3, w2, g, b, tm):
    m = x.shape[0]
    return pl.pallas_call(
        _ffn_kernel,
        grid=(m // tm,),
        in_specs=[
            pl.BlockSpec((tm, D_MODEL), lambda i: (i, 0)),
            _const_spec((D_MODEL, D_FF)),
            _const_spec((D_MODEL, D_FF)),
            _const_spec((D_FF, D_MODEL)),
            _const_spec((1, D_MODEL)),
            _const_spec((1, D_MODEL)),
        ],
        out_specs=pl.BlockSpec((tm, D_MODEL), lambda i: (i, 0)),
        out_shape=jax.ShapeDtypeStruct((m, D_MODEL), F32),
        scratch_shapes=[pltpu.VMEM((tm, D_FF), BF16)],
        compiler_params=_params(1),
        name="ffn",
    )(x, w1, w3, w2, g, b)


def _ffn_step_kernel(x_ref, w1_ref, w3_ref, w2_ref, g_ref, b_ref, o_ref, acc):
    f = pl.program_id(0)

    @pl.when(f == 0)
    def _():
        acc[...] = jnp.zeros(acc.shape, F32)

    x = x_ref[...]
    h = _silu(_dot3(x, w1_ref[...])) * _dot3(x, w3_ref[...])
    acc[...] += _dot3(h, w2_ref[...])

    @pl.when(f == pl.num_programs(0) - 1)
    def _():
        o_ref[...] = _ln(ALPHA * x + acc[...], g_ref[...], b_ref[...])


def _ffn_step(x, w1, w3, w2, g, b):
    n = x.shape[0]
    return pl.pallas_call(
        _ffn_step_kernel,
        grid=(D_FF // FF_TILE,),
        in_specs=[
            _const_spec((n, D_MODEL)),
            pl.BlockSpec((D_MODEL, FF_TILE), lambda f: (0, f)),
            pl.BlockSpec((D_MODEL, FF_TILE), lambda f: (0, f)),
            pl.BlockSpec((FF_TILE, D_MODEL), lambda f: (f, 0)),
            _const_spec((1, D_MODEL)),
            _const_spec((1, D_MODEL)),
        ],
        out_specs=_const_spec((n, D_MODEL)),
        out_shape=jax.ShapeDtypeStruct((n, D_MODEL), F32),
        scratch_shapes=[pltpu.VMEM((n, D_MODEL), F32)],
        compiler_params=_params(1),
        name="ffn_step",
    )(x, w1, w3, w2, g, b)


QQI = Q_END + IDX_HEADS * IDX_DIM
KVT_ROWS = 2 * KV_DIM + IDX_DIM


def _rope(xc, c, sa, sb):
    return xc * c + pltpu.roll(xc, 8, 1) * sa + pltpu.roll(xc, LANES - 8, 1) * sb


def _rope_rows(ht, r0, cos, sin, out_ref, o0, n_rows):
    x1, x2 = ht[r0:r0 + 8, :], ht[r0 + 8:r0 + 16, :]
    out_ref[o0:o0 + 8, :] = x1 * cos - x2 * sin
    out_ref[o0 + 8:o0 + 16, :] = x1 * sin + x2 * cos
    out_ref[o0 + 16:o0 + n_rows, :] = ht[r0 + 16:r0 + n_rows, :]


def _odd_proj_kernel(x_ref, wq_ref, wkt_ref, tab_ref, tabt_ref,
                     q_ref, qi_ref, wi_ref, kt_ref, vt_ref, kit_ref, ktb_ref, vtb_ref, kitb_ref, *, precise):
    if precise:
        x = x_ref[...]
        h = _dot3(x, wq_ref[...])
        ht = _dot3(wkt_ref[...], x, dot=_dot_nt)
    else:
        xb = x_ref[...].astype(BF16)
        h = _dot(xb, wq_ref[...])
        ht = _dot_nt(wkt_ref[...], xb)
    c, sa, sb = tab_ref[:, 0:LANES], tab_ref[:, LANES:2 * LANES], tab_ref[:, 2 * LANES:3 * LANES]
    q_scale = HEAD_DIM ** -0.5 if precise else float(HEAD_DIM ** -0.5 * np.log2(np.e))
    for j in range(Q_END // LANES):
        cs = slice(j * LANES, (j + 1) * LANES)
        q_ref[:, cs] = (_rope(h[:, cs], c, sa, sb) * q_scale).astype(q_ref.dtype)
    for j in range(IDX_HEADS * IDX_DIM // LANES):
        cs = slice(j * LANES, (j + 1) * LANES)
        qi_ref[:, cs] = (_rope(h[:, Q_END + j * LANES:Q_END + (j + 1) * LANES], c, sa, sb)
                         * (IDX_DIM ** -0.5)).astype(qi_ref.dtype)
    wi_ref[...] = h[:, QQI:QQI + LANES] * (IDX_HEADS ** -0.5)

    cos, sin = tabt_ref[0:8, :], tabt_ref[8:16, :]
    for hd in range(N_KV_HEADS):
        _rope_rows(ht, hd * HEAD_DIM, cos, sin, kt_ref, hd * HEAD_DIM, HEAD_DIM)
    vt_ref[...] = ht[KV_DIM:2 * KV_DIM, :]
    _rope_rows(ht, 2 * KV_DIM, cos, sin, kit_ref, 0, IDX_DIM)
    ktb_ref[...] = kt_ref[...].astype(BF16)
    vtb_ref[...] = vt_ref[...].astype(BF16)
    kitb_ref[...] = kit_ref[...].astype(BF16)


def _odd_proj(x, wq, wkt, tab, tabt, bsz, t, tm, precise=False):
    nt = t // tm
    tok = lambda i: (i, 0)
    feat = lambda i: (i // nt, 0, i % nt)
    q_dtype = F32 if precise else BF16
    row_widths = [(Q_END, q_dtype), (IDX_HEADS * IDX_DIM, q_dtype), (LANES, F32)]
    feat_rows = [(KV_DIM, F32), (KV_DIM, F32), (IDX_DIM, F32), (KV_DIM, BF16), (KV_DIM, BF16), (IDX_DIM, BF16)]
    return pl.pallas_call(
        functools.partial(_odd_proj_kernel, precise=precise),
        grid=(bsz * nt,),
        in_specs=[
            pl.BlockSpec((tm, D_MODEL), tok),
            _const_spec((D_MODEL, QQI + LANES)),
            _const_spec((KVT_ROWS, D_MODEL)),
            pl.BlockSpec((tm, 3 * LANES), lambda i: (i % nt, 0)),
            pl.BlockSpec((ROPE_DIM, tm), lambda i: (0, i % nt)),
        ],
        out_specs=[pl.BlockSpec((tm, w_), tok) for w_, _ in row_widths]
        + [pl.BlockSpec((None, r_, tm), feat) for r_, _ in feat_rows],
        out_shape=[jax.ShapeDtypeStruct((bsz * t, w_), d_) for w_, d_ in row_widths]
        + [jax.ShapeDtypeStruct((bsz, r_, t), d_) for r_, d_ in feat_rows],
        compiler_params=_params(1),
        name="odd_proj",
    )(x, wq, wkt, tab, tabt)


def _rope_tables(pos):
    half = ROPE_DIM // 2
    inv = ROPE_THETA ** (-jnp.arange(half, dtype=F32) / half)
    ang = pos.astype(F32)[:, None] * inv[None, :]
    cos, sin = jnp.cos(ang), jnp.sin(ang)
    n = pos.shape[0]
    one = jnp.ones((n, HEAD_DIM - ROPE_DIM), F32)
    zero = jnp.zeros((n, HEAD_DIM - ROPE_DIM), F32)
    zh = jnp.zeros((n, half), F32)
    c64 = jnp.concatenate([cos, cos, one], axis=-1)
    sa64 = jnp.concatenate([zh, sin, zero], axis=-1)
    sb64 = jnp.concatenate([-sin, zh, zero], axis=-1)
    tab = jnp.concatenate([c64, c64, sa64, sa64, sb64, sb64], axis=-1)
    tabt = jnp.concatenate([cos.T, sin.T], axis=0)
    return tab, tabt


KEY_BLOCK = 512
COUNT_ROWS = 64


def _lane_fold(a, op):
    parts = [a[:, c * LANES:(c + 1) * LANES] for c in range(a.shape[1] // LANES)]
    while len(parts) > 1:
        parts = [op(parts[i], parts[i + 1]) for i in range(0, len(parts), 2)]
    return parts[0]


def _dsa_prompt_kernel(x_ref, q_ref, qi_ref, wi_ref, kit_ref, kt_ref, vt_ref, wout_ref, g_ref, b_ref,
                       o_ref, key_ref, bias_ref, lbuf, wib, q3, a3, mbuf, obuf, thr_ref, cnt_ref, *, tq, topk):
    t = pl.program_id(1)
    n_blk = (t * tq) // KEY_BLOCK + 1

    def blk(kb):
        return pl.ds(pl.multiple_of(kb * KEY_BLOCK, KEY_BLOCK), KEY_BLOCK)

    def bcast(col):
        return jnp.broadcast_to(col, (tq, LANES))

    def chunks(a):
        return [a[:, c * LANES:(c + 1) * LANES] for c in range(KEY_BLOCK // LANES)]

    row = lax.broadcasted_iota(jnp.int32, (tq, KEY_BLOCK), 0)
    lane = lax.broadcasted_iota(jnp.int32, (tq, KEY_BLOCK), 1)

    for h in range(IDX_HEADS):
        wib[h] = bcast(wi_ref[:, h:h + 1])
    for h in range(N_HEADS):
        q3[h // GROUP, (h % GROUP) * tq:(h % GROUP + 1) * tq, :] = q_ref[:, h * HEAD_DIM:(h + 1) * HEAD_DIM]

    def score_body(kb, carry):
        ki = kit_ref[:, blk(kb)]
        score = [jnp.zeros((tq, LANES), F32)] * (KEY_BLOCK // LANES)
        for h in range(IDX_HEADS):
            d = _dot(qi_ref[:, h * IDX_DIM:(h + 1) * IDX_DIM], ki)
            w = wib[h]
            score = [s_ + w * jnp.maximum(d_, 0.0) for s_, d_ in zip(score, chunks(d))]
        adm = kb * KEY_BLOCK + lane <= t * tq + row
        key_ref[:, blk(kb)] = _float_key(jnp.where(adm, jnp.concatenate(score, axis=-1), -jnp.inf))
        return carry

    lax.fori_loop(0, n_blk, score_body, 0)

    def count(cmp, thr):
        thr_ref[...] = bcast(thr)
        cnt_ref[...] = jnp.zeros((tq, LANES), F32)

        def body(kb, c_):
            for r in range(tq // COUNT_ROWS):
                rs = slice(r * COUNT_ROWS, (r + 1) * COUNT_ROWS)
                th = thr_ref[rs, :]
                acc = cnt_ref[rs, :]
                for k_ in chunks(key_ref[rs, blk(kb)]):
                    acc = acc + cmp(k_, th).astype(F32)
                cnt_ref[rs, :] = acc
            return c_

        lax.fori_loop(0, n_blk, body, 0)
        return jnp.sum(cnt_ref[...], axis=-1, keepdims=True)

    tau = _kth_largest_key(lambda cand: count(jnp.greater_equal, cand), tq, float(topk))
    taub = bcast(tau)
    need = float(topk) - count(jnp.greater, tau)
    n_eq = count(jnp.equal, tau)
    crowded = jnp.max(((n_eq > need) & (tau > KEY_NEG_INF)).astype(F32)) > 0.0

    @pl.when(jnp.logical_not(crowded))
    def _():
        def bias_body(kb, carry):
            out = [jnp.where((k_ >= taub) & (k_ > KEY_NEG_INF), 0.0, -jnp.inf)
                   for k_ in chunks(key_ref[:, blk(kb)])]
            bias_ref[:, blk(kb)] = jnp.concatenate(out, axis=-1)
            return carry
        lax.fori_loop(0, n_blk, bias_body, 0)

    @pl.when(crowded)
    def _():
        r_ = lax.broadcasted_iota(jnp.int32, (KEY_BLOCK, KEY_BLOCK), 0)
        c_ = lax.broadcasted_iota(jnp.int32, (KEY_BLOCK, KEY_BLOCK), 1)
        before = jnp.where(r_ < c_, 1.0, 0.0).astype(BF16)
        needb = bcast(need)

        def bias_body(kb, seen):
            ks = chunks(key_ref[:, blk(kb)])
            tie = [jnp.where(k_ == taub, 1.0, 0.0) for k_ in ks]
            rank = chunks(_dot(jnp.concatenate(tie, axis=-1).astype(BF16), before))
            out = []
            for k_, t_, r in zip(ks, tie, rank):
                keep = ((k_ > taub) | ((t_ > 0.0) & (seen + r < needb))) & (k_ > KEY_NEG_INF)
                out.append(jnp.where(keep, 0.0, -jnp.inf))
            bias_ref[:, blk(kb)] = jnp.concatenate(out, axis=-1)
            return seen + bcast(jnp.sum(functools.reduce(jnp.add, tie), axis=-1, keepdims=True))

        lax.fori_loop(0, n_blk, bias_body, jnp.zeros((tq, LANES), F32))

    def group_body(kv, carry):
        qg = q3[kv]
        mbuf[...] = jnp.full(mbuf.shape, -jnp.inf, F32)
        obuf[...] = jnp.zeros(obuf.shape, F32)

        def logit_body(kb, c_):
            logit = _dot(qg, kt_ref[kv, :, blk(kb)])
            bias = bias_ref[:, blk(kb)]
            for j in range(GROUP):
                rows = slice(j * tq, (j + 1) * tq)
                lj = logit[rows, :] + bias
                lbuf[rows, blk(kb)] = lj
                mbuf[rows, :] = jnp.maximum(mbuf[rows, :], _lane_fold(lj, jnp.maximum))
            return c_

        lax.fori_loop(0, n_blk, logit_body, 0)
        mbs = [bcast(jnp.max(mbuf[j * tq:(j + 1) * tq, :], axis=-1, keepdims=True)) for j in range(GROUP)]

        def pv_body(kb, c_):
            pj = []
            for j in range(GROUP):
                rows = slice(j * tq, (j + 1) * tq)
                ps = [jnp.exp2(l_ - mbs[j]) for l_ in chunks(lbuf[rows, blk(kb)])]
                pj.append(jnp.concatenate(ps, axis=-1).astype(BF16))
            v1 = jnp.concatenate([vt_ref[kv, :, blk(kb)], jnp.ones((LANES - HEAD_DIM, KEY_BLOCK), BF16)], axis=0)
            obuf[...] += _dot_nt(jnp.concatenate(pj, axis=0), v1)
            return c_

        lax.fori_loop(0, n_blk, pv_body, 0)
        a3[kv] = (obuf[:, 0:HEAD_DIM] / obuf[:, HEAD_DIM:HEAD_DIM + 1]).astype(BF16)
        return carry

    lax.fori_loop(0, N_KV_HEADS, group_body, 0)

    a = jnp.concatenate([a3[h // GROUP, (h % GROUP) * tq:(h % GROUP + 1) * tq, :] for h in range(N_HEADS)], axis=-1)
    m = _dot(a, wout_ref[...])
    o_ref[...] = _ln(ALPHA * x_ref[...] + m, g_ref[...], b_ref[...])


def _dsa_prompt(x, q, qi, wi, kit, kt, vt, w_out, g, b, bsz, t, tq=256):
    nq = t // tq
    topk = min(TOPK_MAX, t // 4)
    kern = functools.partial(_dsa_prompt_kernel, tq=tq, topk=topk)
    qmap = lambda i, j: (i * nq + j, 0)
    return pl.pallas_call(
        kern,
        grid=(bsz, nq),
        in_specs=[
            pl.BlockSpec((tq, D_MODEL), qmap),
            pl.BlockSpec((tq, Q_END), qmap),
            pl.BlockSpec((tq, IDX_HEADS * IDX_DIM), qmap),
            pl.BlockSpec((tq, LANES), qmap),
            pl.BlockSpec((None, IDX_DIM, t), lambda i, j: (i, 0, 0)),
            pl.BlockSpec((None, N_KV_HEADS, HEAD_DIM, t), lambda i, j: (i, 0, 0, 0)),
            pl.BlockSpec((None, N_KV_HEADS, HEAD_DIM, t), lambda i, j: (i, 0, 0, 0)),
            _const_spec((Q_END, D_MODEL)),
            _const_spec((1, D_MODEL)),
            _const_spec((1, D_MODEL)),
        ],
        out_specs=pl.BlockSpec((tq, D_MODEL), qmap),
        out_shape=jax.ShapeDtypeStruct((bsz * t, D_MODEL), F32),
        scratch_shapes=[
            pltpu.VMEM((tq, t), jnp.int32),
            pltpu.VMEM((tq, t), F32),
            pltpu.VMEM((GROUP * tq, t), F32),
            pltpu.VMEM((IDX_HEADS, tq, LANES), F32),
            pltpu.VMEM((N_KV_HEADS, GROUP * tq, HEAD_DIM), BF16),
            pltpu.VMEM((N_KV_HEADS, GROUP * tq, HEAD_DIM), BF16),
            pltpu.VMEM((GROUP * tq, LANES), F32),
            pltpu.VMEM((GROUP * tq, LANES), F32),
            pltpu.VMEM((tq, LANES), jnp.int32),
            pltpu.VMEM((tq, LANES), F32),
        ],
        compiler_params=_params(2),
        name="dsa_prompt",
    )(x, q, qi, wi, kit, kt, vt, w_out, g, b)


IDX_PAGES_PER_STEP = 32
KV_PAGES_PER_STEP = 16


def _page_specs(layer, n_rows, pages_per_step):
    def spec(j):
        return pl.BlockSpec((None, None, n_rows, PAGE_SIZE),
                            lambda i, s, pt: (layer, pt[i, s * pages_per_step + j], 0, 0))
    return [spec(j) for j in range(pages_per_step)]


def _dsa_step_score_kernel(pt_ref, qi_ref, wi_ref, kin_ref, *rest):
    pages = rest[:IDX_PAGES_PER_STEP]
    o_ref, on_ref, kbuf_hi, kbuf_lo = rest[IDX_PAGES_PER_STEP:]
    for j, p in enumerate(pages):
        hi, lo = _split(p[...])
        kbuf_hi[:, j * PAGE_SIZE:(j + 1) * PAGE_SIZE] = hi
        kbuf_lo[:, j * PAGE_SIZE:(j + 1) * PAGE_SIZE] = lo
    qi = qi_ref[0]
    wi = wi_ref[0]
    qh, ql = _split(qi)
    d = _dot(qh, kbuf_hi[...]) + (_dot(qh, kbuf_lo[...]) + _dot(ql, kbuf_hi[...]))
    o_ref[0] = jnp.sum(wi * jnp.maximum(d, 0.0), axis=0, keepdims=True)
    dn = jnp.sum(qi * kin_ref[0], axis=-1, keepdims=True)
    sn = jnp.sum(wi * jnp.maximum(dn, 0.0), axis=0, keepdims=True)
    on_ref[0] = jnp.broadcast_to(sn, (1, LANES))


def _dsa_step_score(page_table, qi3, wi3, kin3, cache_kidx_t, layer):
    n, n_pages = page_table.shape
    steps = n_pages // IDX_PAGES_PER_STEP
    step_keys = IDX_PAGES_PER_STEP * PAGE_SIZE
    row = lambda i, s, pt: (i, 0, 0)
    grid_spec = pltpu.PrefetchScalarGridSpec(
        num_scalar_prefetch=1,
        grid=(n, steps),
        in_specs=[
            pl.BlockSpec((1, IDX_HEADS, IDX_DIM), row),
            pl.BlockSpec((1, IDX_HEADS, 1), row),
            pl.BlockSpec((1, 1, IDX_DIM), row),
        ] + _page_specs(layer, IDX_DIM, IDX_PAGES_PER_STEP),
        out_specs=[
            pl.BlockSpec((1, 1, step_keys), lambda i, s, pt: (i, 0, s)),
            pl.BlockSpec((1, 1, LANES), row),
        ],
        scratch_shapes=[pltpu.VMEM((IDX_DIM, step_keys), BF16), pltpu.VMEM((IDX_DIM, step_keys), BF16)],
    )
    return pl.pallas_call(
        _dsa_step_score_kernel,
        grid_spec=grid_spec,
        out_shape=[jax.ShapeDtypeStruct((n, 1, n_pages * PAGE_SIZE), F32),
                   jax.ShapeDtypeStruct((n, 1, LANES), F32)],
        compiler_params=_params(2),
        name="dsa_step_score",
    )(page_table, qi3, wi3, kin3, *([cache_kidx_t] * IDX_PAGES_PER_STEP))


def _dsa_step_select_kernel(sp_ref, sn_ref, bias_ref, biasn_ref, key_ref, *, topk):
    n, past = sp_ref.shape
    lane = lax.broadcasted_iota(jnp.int32, (n, LANES), 1)
    key_ref[...] = _float_key(sp_ref[...])
    key_n = jnp.where(lane == 0, _float_key(sn_ref[...]), KEY_NEG_INF - 1)

    def count_ge(cand):
        return (jnp.sum((key_ref[...] >= cand).astype(F32), axis=-1, keepdims=True)
                + jnp.sum((key_n >= cand).astype(F32), axis=-1, keepdims=True))

    tau = _kth_largest_key(count_ge, n, float(topk))
    need = (float(topk) - jnp.sum((key_ref[...] > tau).astype(F32), axis=-1, keepdims=True)
            - jnp.sum((key_n > tau).astype(F32), axis=-1, keepdims=True))

    def count_lt(p):
        idx = lax.broadcasted_iota(jnp.int32, (n, past), 1)
        return jnp.sum(((key_ref[...] == tau) & (idx < p)).astype(F32), axis=-1, keepdims=True)

    lim = _tie_limit(count_lt, need, n, int(past).bit_length())
    key_p = key_ref[...]
    idx = lax.broadcasted_iota(jnp.int32, (n, past), 1)
    keep_p = (key_p > tau) | ((key_p == tau) & (idx <= lim))
    n_keep = jnp.sum(keep_p.astype(F32), axis=-1, keepdims=True)
    keep_n = (key_n > tau) | ((key_n == tau) & (n_keep < float(topk)))
    bias_ref[...] = jnp.where(keep_p, 0.0, -jnp.inf)
    biasn_ref[...] = jnp.where(keep_n & (lane == 0), 0.0, -jnp.inf)


def _dsa_step_select(sp, sn):
    n, past = sp.shape
    topk = min(TOPK_MAX, (past + 1) // 4)
    return pl.pallas_call(
        functools.partial(_dsa_step_select_kernel, topk=topk),
        grid=(1,),
        in_specs=[_const_spec((n, past)), _const_spec((n, LANES))],
        out_specs=[_const_spec((n, past)), _const_spec((n, LANES))],
        out_shape=[jax.ShapeDtypeStruct((n, past), F32), jax.ShapeDtypeStruct((n, LANES), F32)],
        scratch_shapes=[pltpu.VMEM((n, past), jnp.int32)],
        compiler_params=_params(1),
        name="dsa_step_select",
    )(sp, sn)


def _dsa_step_attend_kernel(pt_ref, bias_ref, biasn_ref, qbd_ref, kn_ref, vn_ref, *rest):
    kp = rest[:KV_PAGES_PER_STEP]
    vp = rest[KV_PAGES_PER_STEP:2 * KV_PAGES_PER_STEP]
    o_ref, kbuf, vbuf, m_ref, l_ref, acc_ref = rest[2 * KV_PAGES_PER_STEP:]
    s = pl.program_id(1)
    n_steps = pl.num_programs(1)
    step_keys = KV_PAGES_PER_STEP * PAGE_SIZE

    @pl.when(s == 0)
    def _():
        m_ref[...] = jnp.full(m_ref.shape, -jnp.inf, F32)
        l_ref[...] = jnp.zeros(l_ref.shape, F32)
        acc_ref[...] = jnp.zeros(acc_ref.shape, F32)

    for j in range(KV_PAGES_PER_STEP):
        kbuf[:, j * PAGE_SIZE:(j + 1) * PAGE_SIZE] = kp[j][...].astype(BF16)
        vbuf[:, j * PAGE_SIZE:(j + 1) * PAGE_SIZE] = vp[j][...].astype(BF16)
    qbd = qbd_ref[0]
    logit = _dot(qbd, kbuf[...]) + bias_ref[0]
    m_new = jnp.maximum(m_ref[...], jnp.max(logit, axis=-1, keepdims=True))
    m_safe = jnp.where(m_new == -jnp.inf, 0.0, m_new)
    scale = jnp.exp(m_ref[...] - m_safe)
    p = jnp.exp(logit - m_safe)
    l_ref[...] = l_ref[...] * scale + jnp.sum(p, axis=-1, keepdims=True)
    acc_ref[...] = acc_ref[...] * scale + _dot_nt(p.astype(BF16), vbuf[...])
    m_ref[...] = m_new

    @pl.when(s == n_steps - 1)
    def _():
        kn = kn_ref[0].astype(BF16).astype(F32)
        vn = vn_ref[0].astype(BF16).astype(F32)
        ln_ = jnp.sum(qbd.astype(F32) * kn, axis=-1, keepdims=True) + biasn_ref[0][:, 0:1]
        m_old = m_ref[...]
        m_new2 = jnp.maximum(m_old, ln_)
        sc = jnp.exp(m_old - m_new2)
        pn = jnp.exp(ln_ - m_new2)
        den = l_ref[...] * sc + pn
        r = (acc_ref[...] * sc + pn.astype(BF16).astype(F32) * vn) / den
        pieces = [r[h:h + 1, (h // GROUP) * HEAD_DIM:(h // GROUP + 1) * HEAD_DIM] for h in range(N_HEADS)]
        o_ref[0] = jnp.concatenate(pieces, axis=-1)


def _dsa_step_attend(page_table, bias3, biasn3, qbd, kn3, vn3, cache_k_t, cache_v_t, layer):
    n, n_pages = page_table.shape
    steps = n_pages // KV_PAGES_PER_STEP
    step_keys = KV_PAGES_PER_STEP * PAGE_SIZE
    page_specs = _page_specs(layer, KV_DIM, KV_PAGES_PER_STEP)
    row = lambda i, s, pt: (i, 0, 0)
    grid_spec = pltpu.PrefetchScalarGridSpec(
        num_scalar_prefetch=1,
        grid=(n, steps),
        in_specs=[
            pl.BlockSpec((1, 1, step_keys), lambda i, s, pt: (i, 0, s)),
            pl.BlockSpec((1, 1, LANES), row),
            pl.BlockSpec((1, N_HEADS, KV_DIM), row),
            pl.BlockSpec((1, 1, KV_DIM), row),
            pl.BlockSpec((1, 1, KV_DIM), row),
        ] + page_specs + page_specs,
        out_specs=pl.BlockSpec((1, 1, Q_END), row),
        scratch_shapes=[
            pltpu.VMEM((KV_DIM, step_keys), BF16), pltpu.VMEM((KV_DIM, step_keys), BF16),
            pltpu.VMEM((N_HEADS, 1), F32), pltpu.VMEM((N_HEADS, 1), F32), pltpu.VMEM((N_HEADS, KV_DIM), F32),
        ],
    )
    return pl.pallas_call(
        _dsa_step_attend_kernel,
        grid_spec=grid_spec,
        out_shape=jax.ShapeDtypeStruct((n, 1, Q_END), F32),
        compiler_params=_params(2),
        name="dsa_step_attend",
    )(page_table, bias3, biasn3, qbd, kn3, vn3,
      *([cache_k_t] * KV_PAGES_PER_STEP), *([cache_v_t] * KV_PAGES_PER_STEP))


def _proj_ln_kernel(a_ref, x_ref, w_ref, g_ref, b_ref, o_ref):
    m = _dot3(a_ref[...], w_ref[...])
    o_ref[...] = _ln(ALPHA * x_ref[...] + m, g_ref[...], b_ref[...])


def _proj_ln(a, x, w, g, b):
    n = x.shape[0]
    shapes = [(n, a.shape[1]), (n, D_MODEL), w.shape, (1, D_MODEL), (1, D_MODEL)]
    return pl.pallas_call(
        _proj_ln_kernel,
        grid=(1,),
        in_specs=[_const_spec(s) for s in shapes],
        out_specs=_const_spec((n, D_MODEL)),
        out_shape=jax.ShapeDtypeStruct((n, D_MODEL), F32),
        compiler_params=_params(1),
        name="proj_ln",
    )(a, x, w, g, b)


def _moe_kernel(x_ref, wr_ref, br_ref, w1_ref, w3_ref, w2_ref, g_ref, b_ref, o_ref, acc, comb, xb, *, precise):
    e = pl.program_id(1)
    tm = x_ref.shape[0]
    lane = lax.broadcasted_iota(jnp.int32, (tm, LANES), 1)

    @pl.when(e == 0)
    def _():
        x = x_ref[...]
        logits = jnp.dot(x, wr_ref[...], preferred_element_type=F32, precision=lax.Precision.HIGHEST)
        logits = jnp.where(lane < N_EXPERTS, logits + br_ref[...], -jnp.inf)
        lane_f = lane.astype(F32)
        m1 = jnp.max(logits, axis=-1, keepdims=True)
        i1 = jnp.min(jnp.where(logits == m1, lane_f, float(LANES)), axis=-1, keepdims=True)
        rest = jnp.where(lane_f == i1, -jnp.inf, logits)
        m2 = jnp.max(rest, axis=-1, keepdims=True)
        i2 = jnp.min(jnp.where(rest == m2, lane_f, float(LANES)), axis=-1, keepdims=True)
        e2 = jnp.exp(m2 - m1)
        den = 1.0 + e2
        comb[...] = jnp.where(lane_f == i1, 1.0 / den, 0.0) + jnp.where(lane_f == i2, e2 / den, 0.0)
        xb[...] = x.astype(xb.dtype)
        acc[...] = jnp.zeros(acc.shape, F32)

    c = jnp.sum(jnp.where(lane == e, comb[...], 0.0), axis=-1, keepdims=True)
    if precise:
        h = _silu(_dot3(xb[...], w1_ref[0])) * _dot3(xb[...], w3_ref[0])
        acc[...] += _dot3(c * h, w2_ref[0])
    else:
        h = _silu(_dot(xb[...], w1_ref[0])) * _dot(xb[...], w3_ref[0])
        acc[...] += _dot((c * h).astype(BF16), w2_ref[0])

    @pl.when(e == N_EXPERTS - 1)
    def _():
        o_ref[...] = _ln(ALPHA * x_ref[...] + acc[...], g_ref[...], b_ref[...])


def _moe(x, wr, br, w1, w3, w2, g, b, tm, precise=False):
    m = x.shape[0]
    wspec = pl.BlockSpec((1, D_MODEL, D_MODEL), lambda i, e: (e, 0, 0))
    return pl.pallas_call(
        functools.partial(_moe_kernel, precise=precise),
        grid=(m // tm, N_EXPERTS),
        in_specs=[
            pl.BlockSpec((tm, D_MODEL), lambda i, e: (i, 0)),
            _const_spec((D_MODEL, LANES)),
            _const_spec((1, LANES)),
            wspec, wspec, wspec,
            _const_spec((1, D_MODEL)),
            _const_spec((1, D_MODEL)),
        ],
        out_specs=pl.BlockSpec((tm, D_MODEL), lambda i, e: (i, 0)),
        out_shape=jax.ShapeDtypeStruct((m, D_MODEL), F32),
        scratch_shapes=[pltpu.VMEM((tm, D_MODEL), F32), pltpu.VMEM((tm, LANES), F32),
                        pltpu.VMEM((tm, D_MODEL), F32 if precise else BF16)],
        compiler_params=_params(2),
        name="moe",
    )(x, wr, br, w1, w3, w2, g, b)


def kernel(x_prompt, x_sample, cache_k, cache_v, cache_kidx, state_conv, page_table, ln1_g, ln1_b, ln2_g, ln2_b,
           w_in_even, conv_w, sgu_ln_g, sgu_ln_b, sgu_w, sgu_b, w_out_even, ffn_w1, ffn_w3, ffn_w2, w_in_odd,
           w_out_odd, router_w, router_b, moe_w1, moe_w3, moe_w2):
    bp, tp, _ = x_prompt.shape
    ns = x_sample.shape[0]
    n_pool = cache_k.shape[1]
    past = page_table.shape[1] * PAGE_SIZE

    xp = x_prompt.reshape(bp * tp, D_MODEL)
    xs = x_sample.reshape(ns, D_MODEL)
    tab_p, tabt_p = _rope_tables(jnp.arange(tp))
    tab_s, tabt_s = _rope_tables(jnp.full((ns,), past, jnp.int32))
    ck_t = jnp.transpose(cache_k, (0, 1, 3, 4, 2)).reshape(-1, n_pool, KV_DIM, PAGE_SIZE)
    cv_t = jnp.transpose(cache_v, (0, 1, 3, 4, 2)).reshape(-1, n_pool, KV_DIM, PAGE_SIZE)
    cki_t = jnp.swapaxes(cache_kidx, 2, 3)
    row = lambda a: a.reshape(1, -1)
    head_kv = (jnp.arange(N_HEADS)[:, None] // GROUP == jnp.arange(N_KV_HEADS)[None, :]).astype(BF16)

    k_p, v_p, ki_p, conv_p = [], [], [], []
    k_s, v_s, ki_s, conv_s, chunk_s = [], [], [], [], []
    for layer in range(DEPTH):
        i = layer // 2
        g1, b1, g2, b2 = row(ln1_g[layer]), row(ln1_b[layer]), row(ln2_g[layer]), row(ln2_b[layer])
        if layer % 2 == 0:
            w_in = w_in_even[i].astype(BF16)
            w_out = w_out_even[i].astype(BF16)
            sg, sbn = row(sgu_ln_g[i]), row(sgu_ln_b[i])
            bs_b = jnp.broadcast_to(sgu_b[i][:, :, None], (SGU_GROUPS, CHUNK, CHUNK))
            xp, cp = _even_mixer(xp, bp, tp, w_in, conv_w[i], sg, sbn, sgu_w[i], bs_b, w_out, g1, b1)
            wd = row(jnp.repeat(sgu_w[i][:, 0, 0], CHUNK))
            b0 = row(jnp.repeat(sgu_b[i][:, 0], CHUNK))
            xs, zs, vs = _even_mixer_step(xs, state_conv[i][:, 0], state_conv[i][:, 1], w_in_even[i], conv_w[i],
                                          sg, sbn, wd, b0, w_out_even[i], g1, b1)
            conv_p.append(cp)
            conv_s.append(jnp.stack([state_conv[i][:, 1], zs], axis=1))
            chunk_s.append(vs.reshape(ns, 1, SGU_CH))
            w1, w3, w2 = ffn_w1[i].astype(BF16), ffn_w3[i].astype(BF16), ffn_w2[i].astype(BF16)
            xp = _ffn(xp, w1, w3, w2, g2, b2, tm=512)
            xs = _ffn_step(xs, ffn_w1[i], ffn_w3[i], ffn_w2[i], g2, b2)
        else:
            w = w_in_odd[i]
            wq32 = jnp.concatenate(
                [w[:, :Q_END], w[:, V_END:QI_END],
                 jnp.pad(w[:, QI_END + IDX_DIM:], ((0, 0), (0, LANES - IDX_HEADS)))], axis=1)
            wkt32 = jnp.concatenate([w[:, Q_END:V_END], w[:, QI_END:QI_END + IDX_DIM]], axis=1).T
            wq, wkt = wq32.astype(BF16), wkt32.astype(BF16)
            w_out = w_out_odd[i].astype(BF16)
            qp, qip, wip, ktp, vtp, kitp, ktb, vtb, kitb = _odd_proj(xp, wq, wkt, tab_p, tabt_p, bp, tp, tm=512)
            head_major = (bp, N_KV_HEADS, HEAD_DIM, tp)
            xp = _dsa_prompt(xp, qp, qip, wip, kitb, ktb.reshape(head_major), vtb.reshape(head_major), w_out,
                             g1, b1, bp, tp)
            k_p.append(ktp)
            v_p.append(vtp)
            ki_p.append(kitp)

            qs, qis, wis, kts, vts, kits, _, _, _ = _odd_proj(xs, wq32, wkt32, tab_s, tabt_s, 1, ns, tm=ns,
                                                              precise=True)
            kn, vn, kis = kts[0].T, vts[0].T, kits[0].T
            sp, sn = _dsa_step_score(page_table, qis.reshape(ns, IDX_HEADS, IDX_DIM),
                                     wis[:, :IDX_HEADS].reshape(ns, IDX_HEADS, 1), kis.reshape(ns, 1, IDX_DIM),
                                     cki_t, i)
            qbd = (qs.astype(BF16).reshape(ns, N_HEADS, 1, HEAD_DIM)
                   * head_kv[None, :, :, None]).reshape(ns, N_HEADS, KV_DIM)
            bias_s, biasn_s = _dsa_step_select(sp.reshape(ns, past), sn.reshape(ns, LANES))
            a_s = _dsa_step_attend(page_table, bias_s.reshape(ns, 1, past), biasn_s.reshape(ns, 1, LANES), qbd,
                                   kn.reshape(ns, 1, KV_DIM), vn.reshape(ns, 1, KV_DIM), ck_t, cv_t, i)
            xs = _proj_ln(a_s.reshape(ns, Q_END), xs, w_out_odd[i], g1, b1)
            k_s.append(kn.reshape(ns, 1, N_KV_HEADS, HEAD_DIM))
            v_s.append(vn.reshape(ns, 1, N_KV_HEADS, HEAD_DIM))
            ki_s.append(kis.reshape(ns, 1, IDX_DIM))

            wr = jnp.pad(router_w[i], ((0, 0), (0, LANES - N_EXPERTS)))
            br = jnp.pad(row(router_b[i]), ((0, 0), (0, LANES - N_EXPERTS)))
            w1, w3, w2 = moe_w1[i].astype(BF16), moe_w3[i].astype(BF16), moe_w2[i].astype(BF16)
            xp = _moe(xp, wr, br, w1, w3, w2, g2, b2, tm=1024)
            xs = _moe(xs, wr, br, moe_w1[i], moe_w3[i], moe_w2[i], g2, b2, tm=ns, precise=True)

    n_odd = len(k_p)
    kv_shape = (n_odd, bp, N_KV_HEADS, HEAD_DIM, tp)
    k_prompt = jnp.stack(k_p).reshape(kv_shape).transpose(0, 1, 4, 2, 3)
    v_prompt = jnp.stack(v_p).reshape(kv_shape).transpose(0, 1, 4, 2, 3)
    kidx_prompt = jnp.stack(ki_p).transpose(0, 1, 3, 2)
    return (xp.reshape(bp, tp, D_MODEL), xs.reshape(ns, 1, D_MODEL),
            k_prompt, v_prompt, kidx_prompt, jnp.stack(conv_p),
            jnp.stack(k_s), jnp.stack(v_s), jnp.stack(ki_s), jnp.stack(conv_s), jnp.stack(chunk_s))
```

```python
import functools

import jax
import jax.numpy as jnp
import numpy as np
from jax import lax
from jax.experimental import pallas as pl
from jax.experimental.pallas import tpu as pltpu

D_MODEL = 1024
DEPTH = 4
PAGE_SIZE = 128
CONV_CH = 512
CONV_W = 3
SGU_CH = 512
SGU_GROUPS = 4
CHUNK = 128
N_HEADS = 16
HEAD_DIM = 64
N_KV_HEADS = 4
GROUP = N_HEADS // N_KV_HEADS
KV_DIM = N_KV_HEADS * HEAD_DIM
ROPE_DIM = 16
ROPE_THETA = 500000.0
IDX_HEADS = 8
IDX_DIM = 64
TOPK_MAX = 256
D_FF = 2816
N_EXPERTS = 8
ALPHA = (2 * DEPTH) ** 0.25
LN_EPS = 1e-5

EVEN_IN = 3 * CONV_CH + 2 * SGU_CH
Q_END = N_HEADS * HEAD_DIM
K_END = Q_END + KV_DIM
V_END = K_END + KV_DIM
QI_END = V_END + IDX_HEADS * IDX_DIM

LANES = 128
VMEM_LIMIT = 56 * 1024 * 1024
BF16 = jnp.bfloat16
F32 = jnp.float32


def _params(n_axes):
    return pltpu.CompilerParams(dimension_semantics=("arbitrary",) * n_axes, vmem_limit_bytes=VMEM_LIMIT)


def _const_spec(shape, buffers=None):
    nd = len(shape)
    mode = None if buffers is None else pl.Buffered(buffers)
    return pl.BlockSpec(shape, lambda *_: (0,) * nd, pipeline_mode=mode)


def _layer_spec(shape, layer, buffers=None):
    nd = len(shape)
    mode = None if buffers is None else pl.Buffered(buffers)
    return pl.BlockSpec((None,) + tuple(shape), lambda *_: (layer,) + (0,) * nd, pipeline_mode=mode)


def _ln(x, g, b):
    mu = jnp.mean(x, axis=-1, keepdims=True)
    xc = x - mu
    var = jnp.mean(xc * xc, axis=-1, keepdims=True)
    return xc * lax.rsqrt(var + LN_EPS) * g + b


def _dot(a, b):
    return jnp.dot(a, b, preferred_element_type=F32)


def _dot_nt(a, b):
    return lax.dot_general(a, b, (((1,), (1,)), ((), ())), preferred_element_type=F32)


def _split(a):
    hi = a.astype(BF16)
    return hi, (a - hi.astype(F32)).astype(BF16)


def _dot3(a, b, dot=_dot):
    ah, al = _split(a)
    bh, bl = _split(b)
    return dot(ah, bh) + (dot(ah, bl) + dot(al, bh))


def _silu(x):
    return x * (1.0 / (1.0 + jnp.exp(-x)))


BISECT_STEPS = 24


def _kth_largest(count_gt, min_above, row_min, row_max, k):
    lo0 = row_min - jnp.maximum(jnp.abs(row_min), 1.0)

    def halve(i, c):
        lo, hi = c
        mid = lo + (hi - lo) * 0.5
        below = count_gt(mid) < k
        return jnp.where(below, lo, mid), jnp.where(below, mid, hi)

    lo, _ = lax.fori_loop(0, BISECT_STEPS, halve, (lo0, row_max))
    tau = min_above(lo)
    above = count_gt(tau)

    def unsettled(c):
        return jnp.max((c[2] >= k).astype(F32)) > 0.0

    def advance(c):
        lo, tau, above = c
        lo = jnp.where(above >= k, tau, lo)
        tau = min_above(lo)
        return lo, tau, count_gt(tau)

    _, tau, above = lax.while_loop(unsettled, advance, (lo, tau, above))
    return tau, above


def _tie_limit(count_lt, need, rows, n_bits):
    def body(i, p):
        cand = p + jnp.left_shift(jnp.int32(1), jnp.int32(n_bits - 1) - i)
        return jnp.where(count_lt(cand) < need, cand, p)
    return lax.fori_loop(0, n_bits, body, jnp.zeros((rows, 1), jnp.int32))


def _even_mixer_kernel(x_ref, win_ref, cw_ref, sg_ref, sb_ref, wm_ref, bs_ref, wout_ref, g_ref, b_ref,
                       o_ref, cs_ref, zbuf, ybuf, *, tm):
    t = pl.program_id(1)
    x = x_ref[...]
    h = _dot(x.astype(BF16), win_ref[...])
    gate_b = h[:, 0:CONV_CH]
    z = h[:, CONV_CH:2 * CONV_CH] * h[:, 2 * CONV_CH:3 * CONV_CH]
    u = h[:, 3 * CONV_CH:3 * CONV_CH + SGU_CH]
    v = h[:, 3 * CONV_CH + SGU_CH:]

    @pl.when(t == 0)
    def _():
        zbuf[0:8, :] = jnp.zeros((8, CONV_CH), F32)

    zbuf[8:8 + tm, :] = z
    conv = cw_ref[0:1, :] * zbuf[6:6 + tm, :] + cw_ref[1:2, :] * zbuf[7:7 + tm, :] + cw_ref[2:3, :] * z
    ybuf[:, 0:CONV_CH] = (gate_b * conv).astype(BF16)
    cs_ref[0] = zbuf[tm + 6:tm + 8, :]
    zbuf[0:8, :] = zbuf[tm:tm + 8, :]

    vn = _ln(v, sg_ref[...], sb_ref[...])
    row = lax.broadcasted_iota(jnp.int32, (CHUNK, CHUNK), 0)
    col = lax.broadcasted_iota(jnp.int32, (CHUNK, CHUNK), 1)
    for g in range(SGU_GROUPS):
        wm = jnp.where(row >= col, wm_ref[g], 0.0).astype(BF16)
        for c in range(tm // CHUNK):
            rs = slice(c * CHUNK, (c + 1) * CHUNK)
            cs = slice(g * CHUNK, (g + 1) * CHUNK)
            s = _dot(wm, vn[rs, cs].astype(BF16)) + bs_ref[g]
            ybuf[rs, CONV_CH + g * CHUNK:CONV_CH + (g + 1) * CHUNK] = (u[rs, cs] * s).astype(BF16)

    m = _dot(ybuf[...], wout_ref[...])
    o_ref[...] = _ln(ALPHA * x + m, g_ref[...], b_ref[...])


def _even_mixer(x, bsz, t, w_in, conv_w, sgu_g, sgu_bn, sgu_w, bs_b, w_out, g, b, layer, tm=512):
    nt = t // tm
    kern = functools.partial(_even_mixer_kernel, tm=tm)
    return pl.pallas_call(
        kern,
        grid=(bsz, nt),
        in_specs=[
            pl.BlockSpec((tm, D_MODEL), lambda i, j: (i * nt + j, 0)),
            _layer_spec((D_MODEL, EVEN_IN), layer),
            _const_spec((CONV_W, CONV_CH)),
            _const_spec((1, SGU_CH)),
            _const_spec((1, SGU_CH)),
            _const_spec((SGU_GROUPS, CHUNK, CHUNK)),
            _const_spec((SGU_GROUPS, CHUNK, CHUNK)),
            _layer_spec((D_MODEL, D_MODEL), layer),
            _const_spec((1, D_MODEL)),
            _const_spec((1, D_MODEL)),
        ],
        out_specs=[
            pl.BlockSpec((tm, D_MODEL), lambda i, j: (i * nt + j, 0)),
            pl.BlockSpec((1, CONV_W - 1, CONV_CH), lambda i, j: (i, 0, 0)),
        ],
        out_shape=[
            jax.ShapeDtypeStruct((bsz * t, D_MODEL), F32),
            jax.ShapeDtypeStruct((bsz, CONV_W - 1, CONV_CH), F32),
        ],
        scratch_shapes=[pltpu.VMEM((tm + 8, CONV_CH), F32), pltpu.VMEM((tm, D_MODEL), BF16)],
        compiler_params=_params(2),
        name="even_mixer",
    )(x, w_in, conv_w, sgu_g, sgu_bn, sgu_w, bs_b, w_out, g, b)


def _even_mixer_step_kernel(x_ref, p0_ref, p1_ref, win_ref, cw_ref, sg_ref, sb_ref, wd_ref, b0_ref, wout_ref,
                            g_ref, b_ref, o_ref, z_ref, vn_ref):
    x = x_ref[...]
    gate_b, gate_c, h_in, u, v = [_dot3(x, win_ref[:, j * CONV_CH:(j + 1) * CONV_CH]) for j in range(5)]
    z = gate_c * h_in
    conv = cw_ref[0:1, :] * p0_ref[...] + cw_ref[1:2, :] * p1_ref[...] + cw_ref[2:3, :] * z
    vn = _ln(v, sg_ref[...], sb_ref[...])
    s = wd_ref[...] * vn + b0_ref[...]
    y = jnp.concatenate([gate_b * conv, u * s], axis=-1)
    m = _dot3(y, wout_ref[...])
    o_ref[...] = _ln(ALPHA * x + m, g_ref[...], b_ref[...])
    z_ref[...] = z
    vn_ref[...] = vn


def _even_mixer_step(x, p0, p1, w_in, conv_w, sgu_g, sgu_bn, wd, b0, w_out, g, b, layer):
    n = x.shape[0]
    c = _const_spec
    in_specs = [c((n, D_MODEL)), c((n, CONV_CH)), c((n, CONV_CH)), _layer_spec((D_MODEL, EVEN_IN), layer),
                c((CONV_W, CONV_CH)), c((1, SGU_CH)), c((1, SGU_CH)), c((1, SGU_CH)), c((1, SGU_CH)),
                _layer_spec((D_MODEL, D_MODEL), layer), c((1, D_MODEL)), c((1, D_MODEL))]
    return pl.pallas_call(
        _even_mixer_step_kernel,
        grid=(1,),
        in_specs=in_specs,
        out_specs=[_const_spec((n, D_MODEL)), _const_spec((n, CONV_CH)), _const_spec((n, SGU_CH))],
        out_shape=[jax.ShapeDtypeStruct((n, D_MODEL), F32), jax.ShapeDtypeStruct((n, CONV_CH), F32),
                   jax.ShapeDtypeStruct((n, SGU_CH), F32)],
        compiler_params=_params(1),
        name="even_mixer_step",
    )(x, p0, p1, w_in, conv_w, sgu_g, sgu_bn, wd, b0, w_out, g, b)


FF_TILE = 256


def _ffn_kernel(x_ref, w1_ref, w3_ref, w2_ref, g_ref, b_ref, o_ref, hbuf):
    x = x_ref[...]
    xb = x.astype(BF16)
    for f in range(D_FF // FF_TILE):
        cs = slice(f * FF_TILE, (f + 1) * FF_TILE)
        hbuf[:, cs] = (_silu(_dot(xb, w1_ref[:, cs])) * _dot(xb, w3_ref[:, cs])).astype(BF16)
    y = _dot(hbuf[...], w2_ref[...])
    o_ref[...] = _ln(ALPHA * x + y, g_ref[...], b_ref[...])


def _ffn(x, w1, w3, w2, g, b, layer, tm):
    m = x.shape[0]
    return pl.pallas_call(
        _ffn_kernel,
        grid=(m // tm,),
        in_specs=[
            pl.BlockSpec((tm, D_MODEL), lambda i: (i, 0)),
            _layer_spec((D_MODEL, D_FF), layer, buffers=1),
            _layer_spec((D_MODEL, D_FF), layer, buffers=1),
            _layer_spec((D_FF, D_MODEL), layer, buffers=1),
            _const_spec((1, D_MODEL)),
            _const_spec((1, D_MODEL)),
        ],
        out_specs=pl.BlockSpec((tm, D_MODEL), lambda i: (i, 0)),
        out_shape=jax.ShapeDtypeStruct((m, D_MODEL), F32),
        scratch_shapes=[pltpu.VMEM((tm, D_FF), BF16)],
        compiler_params=_params(1),
        name="ffn",
    )(x, w1, w3, w2, g, b)


def _ffn_step_kernel(x_ref, w1_ref, w3_ref, w2_ref, g_ref, b_ref, o_ref, acc):
    f = pl.program_id(0)

    @pl.when(f == 0)
    def _():
        acc[...] = jnp.zeros(acc.shape, F32)

    x = x_ref[...]
    h = _silu(_dot3(x, w1_ref[...])) * _dot3(x, w3_ref[...])
    acc[...] += _dot3(h, w2_ref[...])

    @pl.when(f == pl.num_programs(0) - 1)
    def _():
        o_ref[...] = _ln(ALPHA * x + acc[...], g_ref[...], b_ref[...])


def _ffn_step(x, w1, w3, w2, g, b, layer):
    n = x.shape[0]
    return pl.pallas_call(
        _ffn_step_kernel,
        grid=(D_FF // FF_TILE,),
        in_specs=[
            _const_spec((n, D_MODEL)),
            pl.BlockSpec((None, D_MODEL, FF_TILE), lambda f: (layer, 0, f)),
            pl.BlockSpec((None, D_MODEL, FF_TILE), lambda f: (layer, 0, f)),
            pl.BlockSpec((None, FF_TILE, D_MODEL), lambda f: (layer, f, 0)),
            _const_spec((1, D_MODEL)),
            _const_spec((1, D_MODEL)),
        ],
        out_specs=_const_spec((n, D_MODEL)),
        out_shape=jax.ShapeDtypeStruct((n, D_MODEL), F32),
        scratch_shapes=[pltpu.VMEM((n, D_MODEL), F32)],
        compiler_params=_params(1),
        name="ffn_step",
    )(x, w1, w3, w2, g, b)


QQI = Q_END + IDX_HEADS * IDX_DIM
KVT_ROWS = 2 * KV_DIM + IDX_DIM


def _rope(xc, c, sa, sb):
    return xc * c + pltpu.roll(xc, 8, 1) * sa + pltpu.roll(xc, LANES - 8, 1) * sb


def _rope_rows(ht, r0, cos, sin, out_ref, o0, n_rows):
    x1, x2 = ht[r0:r0 + 8, :], ht[r0 + 8:r0 + 16, :]
    out_ref[o0:o0 + 8, :] = x1 * cos - x2 * sin
    out_ref[o0 + 8:o0 + 16, :] = x1 * sin + x2 * cos
    out_ref[o0 + 16:o0 + n_rows, :] = ht[r0 + 16:r0 + n_rows, :]


def _odd_proj_kernel(x_ref, wq_ref, wkt_ref, tab_ref, tabt_ref,
                     q_ref, qi_ref, wi_ref, kt_ref, vt_ref, kit_ref, ktb_ref, vtb_ref, kitb_ref, *, precise):
    if precise:
        x = x_ref[...]
        h = _dot3(x, wq_ref[...])
        ht = _dot3(wkt_ref[...], x, dot=_dot_nt)
    else:
        xb = x_ref[...].astype(BF16)
        h = _dot(xb, wq_ref[...])
        ht = _dot_nt(wkt_ref[...], xb)
    c, sa, sb = tab_ref[:, 0:LANES], tab_ref[:, LANES:2 * LANES], tab_ref[:, 2 * LANES:3 * LANES]
    q_scale = HEAD_DIM ** -0.5 if precise else float(HEAD_DIM ** -0.5 * np.log2(np.e))
    for j in range(Q_END // LANES):
        cs = slice(j * LANES, (j + 1) * LANES)
        q_ref[:, cs] = (_rope(h[:, cs], c, sa, sb) * q_scale).astype(q_ref.dtype)
    for j in range(IDX_HEADS * IDX_DIM // LANES):
        cs = slice(j * LANES, (j + 1) * LANES)
        qi_ref[:, cs] = (_rope(h[:, Q_END + j * LANES:Q_END + (j + 1) * LANES], c, sa, sb)
                         * (IDX_DIM ** -0.5)).astype(qi_ref.dtype)
    wi_ref[...] = h[:, QQI:QQI + LANES] * (IDX_HEADS ** -0.5)

    cos, sin = tabt_ref[0:8, :], tabt_ref[8:16, :]
    for hd in range(N_KV_HEADS):
        _rope_rows(ht, hd * HEAD_DIM, cos, sin, kt_ref, hd * HEAD_DIM, HEAD_DIM)
    vt_ref[...] = ht[KV_DIM:2 * KV_DIM, :]
    _rope_rows(ht, 2 * KV_DIM, cos, sin, kit_ref, 0, IDX_DIM)
    ktb_ref[...] = kt_ref[...].astype(BF16)
    vtb_ref[...] = vt_ref[...].astype(BF16)
    kitb_ref[...] = kit_ref[...].astype(BF16)


def _odd_proj(x, wq, wkt, tab, tabt, bsz, t, tm, precise=False):
    nt = t // tm
    tok = lambda i: (i, 0)
    feat = lambda i: (i // nt, 0, i % nt)
    q_dtype = F32 if precise else BF16
    row_widths = [(Q_END, q_dtype), (IDX_HEADS * IDX_DIM, q_dtype), (LANES, F32)]
    feat_rows = [(KV_DIM, F32), (KV_DIM, F32), (IDX_DIM, F32), (KV_DIM, BF16), (KV_DIM, BF16), (IDX_DIM, BF16)]
    return pl.pallas_call(
        functools.partial(_odd_proj_kernel, precise=precise),
        grid=(bsz * nt,),
        in_specs=[
            pl.BlockSpec((tm, D_MODEL), tok),
            _const_spec((D_MODEL, QQI + LANES)),
            _const_spec((KVT_ROWS, D_MODEL)),
            pl.BlockSpec((tm, 3 * LANES), lambda i: (i % nt, 0)),
            pl.BlockSpec((ROPE_DIM, tm), lambda i: (0, i % nt)),
        ],
        out_specs=[pl.BlockSpec((tm, w_), tok) for w_, _ in row_widths]
        + [pl.BlockSpec((None, r_, tm), feat) for r_, _ in feat_rows],
        out_shape=[jax.ShapeDtypeStruct((bsz * t, w_), d_) for w_, d_ in row_widths]
        + [jax.ShapeDtypeStruct((bsz, r_, t), d_) for r_, d_ in feat_rows],
        compiler_params=_params(1),
        name="odd_proj",
    )(x, wq, wkt, tab, tabt)


def _rope_tables(pos):
    half = ROPE_DIM // 2
    inv = ROPE_THETA ** (-jnp.arange(half, dtype=F32) / half)
    ang = pos.astype(F32)[:, None] * inv[None, :]
    cos, sin = jnp.cos(ang), jnp.sin(ang)
    n = pos.shape[0]
    one = jnp.ones((n, HEAD_DIM - ROPE_DIM), F32)
    zero = jnp.zeros((n, HEAD_DIM - ROPE_DIM), F32)
    zh = jnp.zeros((n, half), F32)
    c64 = jnp.concatenate([cos, cos, one], axis=-1)
    sa64 = jnp.concatenate([zh, sin, zero], axis=-1)
    sb64 = jnp.concatenate([-sin, zh, zero], axis=-1)
    tab = jnp.concatenate([c64, c64, sa64, sa64, sb64, sb64], axis=-1)
    tabt = jnp.concatenate([cos.T, sin.T], axis=0)
    return tab, tabt


KEY_BLOCK = 512
COUNT_ROWS = 64


def _lane_fold(a, op):
    parts = [a[:, c * LANES:(c + 1) * LANES] for c in range(a.shape[1] // LANES)]
    while len(parts) > 1:
        parts = [op(parts[i], parts[i + 1]) for i in range(0, len(parts), 2)]
    return parts[0]


def _dsa_prompt_kernel(x_ref, q_ref, qi_ref, wi_ref, kit_ref, kt_ref, vt_ref, wout_ref, g_ref, b_ref,
                       o_ref, key_ref, bias_ref, lbuf, wib, q3, a3, mbuf, obuf, thr_ref, cnt_ref, ext_ref,
                       *, tq, topk):
    t = pl.program_id(1)
    n_blk = (t * tq) // KEY_BLOCK + 1

    def blk(kb):
        return pl.ds(pl.multiple_of(kb * KEY_BLOCK, KEY_BLOCK), KEY_BLOCK)

    def bcast(col):
        return jnp.broadcast_to(col, (tq, LANES))

    def chunks(a):
        return [a[:, c * LANES:(c + 1) * LANES] for c in range(KEY_BLOCK // LANES)]

    row = lax.broadcasted_iota(jnp.int32, (tq, KEY_BLOCK), 0)
    lane = lax.broadcasted_iota(jnp.int32, (tq, KEY_BLOCK), 1)

    for h in range(IDX_HEADS):
        wib[h] = bcast(wi_ref[:, h:h + 1])
    for h in range(N_HEADS):
        q3[h // GROUP, (h % GROUP) * tq:(h % GROUP + 1) * tq, :] = q_ref[:, h * HEAD_DIM:(h + 1) * HEAD_DIM]

    ext_ref[0] = jnp.full((tq, LANES), -jnp.inf, F32)
    ext_ref[1] = jnp.full((tq, LANES), jnp.inf, F32)

    def score_body(kb, carry):
        ki = kit_ref[:, blk(kb)]
        score = [jnp.zeros((tq, LANES), F32)] * (KEY_BLOCK // LANES)
        for h in range(IDX_HEADS):
            d = _dot(qi_ref[:, h * IDX_DIM:(h + 1) * IDX_DIM], ki)
            w = wib[h]
            score = [s_ + w * jnp.maximum(d_, 0.0) for s_, d_ in zip(score, chunks(d))]
        score = jnp.concatenate(score, axis=-1)
        adm = kb * KEY_BLOCK + lane <= t * tq + row
        key_ref[:, blk(kb)] = jnp.where(adm, score, -jnp.inf)
        ext_ref[0] = jnp.maximum(ext_ref[0], _lane_fold(jnp.where(adm, score, -jnp.inf), jnp.maximum))
        ext_ref[1] = jnp.minimum(ext_ref[1], _lane_fold(jnp.where(adm, score, jnp.inf), jnp.minimum))
        return carry

    lax.fori_loop(0, n_blk, score_body, 0)
    row_max = jnp.max(ext_ref[0], axis=-1, keepdims=True)
    row_min = jnp.min(ext_ref[1], axis=-1, keepdims=True)

    def sweep(thr, init, step, finish):
        thr_ref[...] = bcast(thr)
        cnt_ref[...] = jnp.full((tq, LANES), init, F32)

        def body(kb, c_):
            for r in range(tq // COUNT_ROWS):
                rs = slice(r * COUNT_ROWS, (r + 1) * COUNT_ROWS)
                th = thr_ref[rs, :]
                acc = cnt_ref[rs, :]
                for k_ in chunks(key_ref[rs, blk(kb)]):
                    acc = step(acc, k_, th)
                cnt_ref[rs, :] = acc
            return c_

        lax.fori_loop(0, n_blk, body, 0)
        return finish(cnt_ref[...], axis=-1, keepdims=True)

    def count(cmp, thr):
        return sweep(thr, 0.0, lambda acc, k_, th: acc + cmp(k_, th).astype(F32), jnp.sum)

    def min_above(thr):
        return sweep(thr, jnp.inf, lambda acc, k_, th: jnp.minimum(acc, jnp.where(k_ > th, k_, jnp.inf)), jnp.min)

    tau, above = _kth_largest(functools.partial(count, jnp.greater), min_above, row_min, row_max, float(topk))
    taub = bcast(tau)
    need = float(topk) - above
    n_eq = count(jnp.equal, tau)
    crowded = jnp.max((n_eq > need).astype(F32)) > 0.0

    @pl.when(jnp.logical_not(crowded))
    def _():
        def bias_body(kb, carry):
            out = [jnp.where((k_ >= taub) & (k_ > -jnp.inf), 0.0, -jnp.inf)
                   for k_ in chunks(key_ref[:, blk(kb)])]
            bias_ref[:, blk(kb)] = jnp.concatenate(out, axis=-1)
            return carry
        lax.fori_loop(0, n_blk, bias_body, 0)

    @pl.when(crowded)
    def _():
        r_ = lax.broadcasted_iota(jnp.int32, (KEY_BLOCK, KEY_BLOCK), 0)
        c_ = lax.broadcasted_iota(jnp.int32, (KEY_BLOCK, KEY_BLOCK), 1)
        before = jnp.where(r_ < c_, 1.0, 0.0).astype(BF16)
        needb = bcast(need)

        def bias_body(kb, seen):
            ks = chunks(key_ref[:, blk(kb)])
            tie = [jnp.where(k_ == taub, 1.0, 0.0) for k_ in ks]
            rank = chunks(_dot(jnp.concatenate(tie, axis=-1).astype(BF16), before))
            out = []
            for k_, t_, r in zip(ks, tie, rank):
                keep = ((k_ > taub) | ((t_ > 0.0) & (seen + r < needb))) & (k_ > -jnp.inf)
                out.append(jnp.where(keep, 0.0, -jnp.inf))
            bias_ref[:, blk(kb)] = jnp.concatenate(out, axis=-1)
            return seen + bcast(jnp.sum(functools.reduce(jnp.add, tie), axis=-1, keepdims=True))

        lax.fori_loop(0, n_blk, bias_body, jnp.zeros((tq, LANES), F32))

    def group_body(kv, carry):
        qg = q3[kv]
        mbuf[...] = jnp.full(mbuf.shape, -jnp.inf, F32)
        obuf[...] = jnp.zeros(obuf.shape, F32)

        def logit_body(kb, c_):
            logit = _dot(qg, kt_ref[kv, :, blk(kb)])
            bias = bias_ref[:, blk(kb)]
            for j in range(GROUP):
                rows = slice(j * tq, (j + 1) * tq)
                lj = logit[rows, :] + bias
                lbuf[rows, blk(kb)] = lj
                mbuf[rows, :] = jnp.maximum(mbuf[rows, :], _lane_fold(lj, jnp.maximum))
            return c_

        lax.fori_loop(0, n_blk, logit_body, 0)
        mbs = [bcast(jnp.max(mbuf[j * tq:(j + 1) * tq, :], axis=-1, keepdims=True)) for j in range(GROUP)]

        def pv_body(kb, c_):
            pj = []
            for j in range(GROUP):
                rows = slice(j * tq, (j + 1) * tq)
                ps = [jnp.exp2(l_ - mbs[j]) for l_ in chunks(lbuf[rows, blk(kb)])]
                pj.append(jnp.concatenate(ps, axis=-1).astype(BF16))
            v1 = jnp.concatenate([vt_ref[kv, :, blk(kb)], jnp.ones((LANES - HEAD_DIM, KEY_BLOCK), BF16)], axis=0)
            obuf[...] += _dot_nt(jnp.concatenate(pj, axis=0), v1)
            return c_

        lax.fori_loop(0, n_blk, pv_body, 0)
        a3[kv] = (obuf[:, 0:HEAD_DIM] / obuf[:, HEAD_DIM:HEAD_DIM + 1]).astype(BF16)
        return carry

    lax.fori_loop(0, N_KV_HEADS, group_body, 0)

    a = jnp.concatenate([a3[h // GROUP, (h % GROUP) * tq:(h % GROUP + 1) * tq, :] for h in range(N_HEADS)], axis=-1)
    m = _dot(a, wout_ref[...])
    o_ref[...] = _ln(ALPHA * x_ref[...] + m, g_ref[...], b_ref[...])


def _dsa_prompt(x, q, qi, wi, kit, kt, vt, w_out, g, b, bsz, t, layer, tq=256):
    nq = t // tq
    topk = min(TOPK_MAX, t // 4)
    kern = functools.partial(_dsa_prompt_kernel, tq=tq, topk=topk)
    qmap = lambda i, j: (i * nq + j, 0)
    return pl.pallas_call(
        kern,
        grid=(bsz, nq),
        in_specs=[
            pl.BlockSpec((tq, D_MODEL), qmap),
            pl.BlockSpec((tq, Q_END), qmap),
            pl.BlockSpec((tq, IDX_HEADS * IDX_DIM), qmap),
            pl.BlockSpec((tq, LANES), qmap),
            pl.BlockSpec((None, IDX_DIM, t), lambda i, j: (i, 0, 0)),
            pl.BlockSpec((None, N_KV_HEADS, HEAD_DIM, t), lambda i, j: (i, 0, 0, 0)),
            pl.BlockSpec((None, N_KV_HEADS, HEAD_DIM, t), lambda i, j: (i, 0, 0, 0)),
            _layer_spec((Q_END, D_MODEL), layer),
            _const_spec((1, D_MODEL)),
            _const_spec((1, D_MODEL)),
        ],
        out_specs=pl.BlockSpec((tq, D_MODEL), qmap),
        out_shape=jax.ShapeDtypeStruct((bsz * t, D_MODEL), F32),
        scratch_shapes=[
            pltpu.VMEM((tq, t), F32),
            pltpu.VMEM((tq, t), F32),
            pltpu.VMEM((GROUP * tq, t), F32),
            pltpu.VMEM((IDX_HEADS, tq, LANES), F32),
            pltpu.VMEM((N_KV_HEADS, GROUP * tq, HEAD_DIM), BF16),
            pltpu.VMEM((N_KV_HEADS, GROUP * tq, HEAD_DIM), BF16),
            pltpu.VMEM((GROUP * tq, LANES), F32),
            pltpu.VMEM((GROUP * tq, LANES), F32),
            pltpu.VMEM((tq, LANES), F32),
            pltpu.VMEM((tq, LANES), F32),
            pltpu.VMEM((2, tq, LANES), F32),
        ],
        compiler_params=_params(2),
        name="dsa_prompt",
    )(x, q, qi, wi, kit, kt, vt, w_out, g, b)


IDX_PAGES_PER_STEP = 32
KV_PAGES_PER_STEP = 16


def _page_specs(layer, n_rows, pages_per_step):
    def spec(j):
        return pl.BlockSpec((None, None, n_rows, PAGE_SIZE),
                            lambda i, s, pt: (layer, pt[i, s * pages_per_step + j], 0, 0))
    return [spec(j) for j in range(pages_per_step)]


def _dsa_step_score_kernel(pt_ref, qi_ref, wi_ref, kin_ref, *rest):
    pages = rest[:IDX_PAGES_PER_STEP]
    o_ref, on_ref, kbuf_hi, kbuf_lo = rest[IDX_PAGES_PER_STEP:]
    for j, p in enumerate(pages):
        hi, lo = _split(p[...])
        kbuf_hi[:, j * PAGE_SIZE:(j + 1) * PAGE_SIZE] = hi
        kbuf_lo[:, j * PAGE_SIZE:(j + 1) * PAGE_SIZE] = lo
    qi = qi_ref[0]
    wi = wi_ref[0]
    qh, ql = _split(qi)
    d = _dot(qh, kbuf_hi[...]) + (_dot(qh, kbuf_lo[...]) + _dot(ql, kbuf_hi[...]))
    o_ref[0] = jnp.sum(wi * jnp.maximum(d, 0.0), axis=0, keepdims=True)
    dn = jnp.sum(qi * kin_ref[0], axis=-1, keepdims=True)
    sn = jnp.sum(wi * jnp.maximum(dn, 0.0), axis=0, keepdims=True)
    on_ref[0] = jnp.broadcast_to(sn, (1, LANES))


def _dsa_step_score(page_table, qi3, wi3, kin3, cache_kidx_t, layer):
    n, n_pages = page_table.shape
    steps = n_pages // IDX_PAGES_PER_STEP
    step_keys = IDX_PAGES_PER_STEP * PAGE_SIZE
    row = lambda i, s, pt: (i, 0, 0)
    grid_spec = pltpu.PrefetchScalarGridSpec(
        num_scalar_prefetch=1,
        grid=(n, steps),
        in_specs=[
            pl.BlockSpec((1, IDX_HEADS, IDX_DIM), row),
            pl.BlockSpec((1, IDX_HEADS, 1), row),
            pl.BlockSpec((1, 1, IDX_DIM), row),
        ] + _page_specs(layer, IDX_DIM, IDX_PAGES_PER_STEP),
        out_specs=[
            pl.BlockSpec((1, 1, step_keys), lambda i, s, pt: (i, 0, s)),
            pl.BlockSpec((1, 1, LANES), row),
        ],
        scratch_shapes=[pltpu.VMEM((IDX_DIM, step_keys), BF16), pltpu.VMEM((IDX_DIM, step_keys), BF16)],
    )
    return pl.pallas_call(
        _dsa_step_score_kernel,
        grid_spec=grid_spec,
        out_shape=[jax.ShapeDtypeStruct((n, 1, n_pages * PAGE_SIZE), F32),
                   jax.ShapeDtypeStruct((n, 1, LANES), F32)],
        compiler_params=_params(2),
        name="dsa_step_score",
    )(page_table, qi3, wi3, kin3, *([cache_kidx_t] * IDX_PAGES_PER_STEP))


def _dsa_step_select_kernel(sp_ref, sn_ref, bias_ref, biasn_ref, *, topk):
    n, past = sp_ref.shape
    lane = lax.broadcasted_iota(jnp.int32, (n, LANES), 1)
    s_new = sn_ref[:, 0:1]

    def count_gt(x):
        return jnp.sum((sp_ref[...] > x).astype(F32), axis=-1, keepdims=True) + (s_new > x).astype(F32)

    def min_above(x):
        sp = sp_ref[...]
        past_min = jnp.min(jnp.where(sp > x, sp, jnp.inf), axis=-1, keepdims=True)
        return jnp.minimum(past_min, jnp.where(s_new > x, s_new, jnp.inf))

    row_max = jnp.maximum(jnp.max(sp_ref[...], axis=-1, keepdims=True), s_new)
    row_min = jnp.minimum(jnp.min(sp_ref[...], axis=-1, keepdims=True), s_new)
    tau, above = _kth_largest(count_gt, min_above, row_min, row_max, float(topk))
    need = float(topk) - above

    def count_lt(p):
        idx = lax.broadcasted_iota(jnp.int32, (n, past), 1)
        return jnp.sum(((sp_ref[...] == tau) & (idx < p)).astype(F32), axis=-1, keepdims=True)

    lim = _tie_limit(count_lt, need, n, int(past).bit_length())
    sp = sp_ref[...]
    idx = lax.broadcasted_iota(jnp.int32, (n, past), 1)
    keep_p = (sp > tau) | ((sp == tau) & (idx <= lim))
    n_keep = jnp.sum(keep_p.astype(F32), axis=-1, keepdims=True)
    keep_n = (s_new > tau) | ((s_new == tau) & (n_keep < float(topk)))
    bias_ref[...] = jnp.where(keep_p, 0.0, -jnp.inf)
    biasn_ref[...] = jnp.where(keep_n & (lane == 0), 0.0, -jnp.inf)


def _dsa_step_select(sp, sn):
    n, past = sp.shape
    topk = min(TOPK_MAX, (past + 1) // 4)
    return pl.pallas_call(
        functools.partial(_dsa_step_select_kernel, topk=topk),
        grid=(1,),
        in_specs=[_const_spec((n, past)), _const_spec((n, LANES))],
        out_specs=[_const_spec((n, past)), _const_spec((n, LANES))],
        out_shape=[jax.ShapeDtypeStruct((n, past), F32), jax.ShapeDtypeStruct((n, LANES), F32)],
        compiler_params=_params(1),
        name="dsa_step_select",
    )(sp, sn)


def _dsa_step_attend_kernel(pt_ref, bias_ref, biasn_ref, qbd_ref, kn_ref, vn_ref, *rest):
    kp = rest[:KV_PAGES_PER_STEP]
    vp = rest[KV_PAGES_PER_STEP:2 * KV_PAGES_PER_STEP]
    o_ref, kbuf, vbuf, m_ref, l_ref, acc_ref = rest[2 * KV_PAGES_PER_STEP:]
    s = pl.program_id(1)
    n_steps = pl.num_programs(1)
    step_keys = KV_PAGES_PER_STEP * PAGE_SIZE

    @pl.when(s == 0)
    def _():
        m_ref[...] = jnp.full(m_ref.shape, -jnp.inf, F32)
        l_ref[...] = jnp.zeros(l_ref.shape, F32)
        acc_ref[...] = jnp.zeros(acc_ref.shape, F32)

    for j in range(KV_PAGES_PER_STEP):
        kbuf[:, j * PAGE_SIZE:(j + 1) * PAGE_SIZE] = kp[j][...].astype(BF16)
        vbuf[:, j * PAGE_SIZE:(j + 1) * PAGE_SIZE] = vp[j][...].astype(BF16)
    qbd = qbd_ref[0]
    logit = _dot(qbd, kbuf[...]) + bias_ref[0]
    m_new = jnp.maximum(m_ref[...], jnp.max(logit, axis=-1, keepdims=True))
    m_safe = jnp.where(m_new == -jnp.inf, 0.0, m_new)
    scale = jnp.exp(m_ref[...] - m_safe)
    p = jnp.exp(logit - m_safe)
    l_ref[...] = l_ref[...] * scale + jnp.sum(p, axis=-1, keepdims=True)
    acc_ref[...] = acc_ref[...] * scale + _dot_nt(p.astype(BF16), vbuf[...])
    m_ref[...] = m_new

    @pl.when(s == n_steps - 1)
    def _():
        kn = kn_ref[0].astype(BF16).astype(F32)
        vn = vn_ref[0].astype(BF16).astype(F32)
        ln_ = jnp.sum(qbd.astype(F32) * kn, axis=-1, keepdims=True) + biasn_ref[0][:, 0:1]
        m_old = m_ref[...]
        m_new2 = jnp.maximum(m_old, ln_)
        sc = jnp.exp(m_old - m_new2)
        pn = jnp.exp(ln_ - m_new2)
        den = l_ref[...] * sc + pn
        r = (acc_ref[...] * sc + pn.astype(BF16).astype(F32) * vn) / den
        pieces = [r[h:h + 1, (h // GROUP) * HEAD_DIM:(h // GROUP + 1) * HEAD_DIM] for h in range(N_HEADS)]
        o_ref[0] = jnp.concatenate(pieces, axis=-1)


def _dsa_step_attend(page_table, bias3, biasn3, qbd, kn3, vn3, cache_k_t, cache_v_t, layer):
    n, n_pages = page_table.shape
    steps = n_pages // KV_PAGES_PER_STEP
    step_keys = KV_PAGES_PER_STEP * PAGE_SIZE
    page_specs = _page_specs(layer, KV_DIM, KV_PAGES_PER_STEP)
    row = lambda i, s, pt: (i, 0, 0)
    grid_spec = pltpu.PrefetchScalarGridSpec(
        num_scalar_prefetch=1,
        grid=(n, steps),
        in_specs=[
            pl.BlockSpec((1, 1, step_keys), lambda i, s, pt: (i, 0, s)),
            pl.BlockSpec((1, 1, LANES), row),
            pl.BlockSpec((1, N_HEADS, KV_DIM), row),
            pl.BlockSpec((1, 1, KV_DIM), row),
            pl.BlockSpec((1, 1, KV_DIM), row),
        ] + page_specs + page_specs,
        out_specs=pl.BlockSpec((1, 1, Q_END), row),
        scratch_shapes=[
            pltpu.VMEM((KV_DIM, step_keys), BF16), pltpu.VMEM((KV_DIM, step_keys), BF16),
            pltpu.VMEM((N_HEADS, 1), F32), pltpu.VMEM((N_HEADS, 1), F32), pltpu.VMEM((N_HEADS, KV_DIM), F32),
        ],
    )
    return pl.pallas_call(
        _dsa_step_attend_kernel,
        grid_spec=grid_spec,
        out_shape=jax.ShapeDtypeStruct((n, 1, Q_END), F32),
        compiler_params=_params(2),
        name="dsa_step_attend",
    )(page_table, bias3, biasn3, qbd, kn3, vn3,
      *([cache_k_t] * KV_PAGES_PER_STEP), *([cache_v_t] * KV_PAGES_PER_STEP))


def _proj_ln_kernel(a_ref, x_ref, w_ref, g_ref, b_ref, o_ref):
    m = _dot3(a_ref[...], w_ref[...])
    o_ref[...] = _ln(ALPHA * x_ref[...] + m, g_ref[...], b_ref[...])


def _proj_ln(a, x, w, g, b, layer):
    n = x.shape[0]
    c = _const_spec
    in_specs = [c((n, a.shape[1])), c((n, D_MODEL)), _layer_spec(w.shape[1:], layer), c((1, D_MODEL)),
                c((1, D_MODEL))]
    return pl.pallas_call(
        _proj_ln_kernel,
        grid=(1,),
        in_specs=in_specs,
        out_specs=_const_spec((n, D_MODEL)),
        out_shape=jax.ShapeDtypeStruct((n, D_MODEL), F32),
        compiler_params=_params(1),
        name="proj_ln",
    )(a, x, w, g, b)


def _moe_kernel(x_ref, wr_ref, br_ref, w1_ref, w3_ref, w2_ref, g_ref, b_ref, o_ref, acc, comb, xb, *, precise):
    e = pl.program_id(1)
    tm = x_ref.shape[0]
    lane = lax.broadcasted_iota(jnp.int32, (tm, LANES), 1)

    @pl.when(e == 0)
    def _():
        x = x_ref[...]
        logits = _dot3(x, wr_ref[...])
        logits = jnp.where(lane < N_EXPERTS, logits + br_ref[...], -jnp.inf)
        lane_f = lane.astype(F32)
        m1 = jnp.max(logits, axis=-1, keepdims=True)
        i1 = jnp.min(jnp.where(logits == m1, lane_f, float(LANES)), axis=-1, keepdims=True)
        rest = jnp.where(lane_f == i1, -jnp.inf, logits)
        m2 = jnp.max(rest, axis=-1, keepdims=True)
        i2 = jnp.min(jnp.where(rest == m2, lane_f, float(LANES)), axis=-1, keepdims=True)
        e2 = jnp.exp(m2 - m1)
        den = 1.0 + e2
        comb[...] = jnp.where(lane_f == i1, 1.0 / den, 0.0) + jnp.where(lane_f == i2, e2 / den, 0.0)
        xb[...] = x.astype(xb.dtype)
        acc[...] = jnp.zeros(acc.shape, F32)

    c = jnp.sum(jnp.where(lane == e, comb[...], 0.0), axis=-1, keepdims=True)
    if precise:
        h = _silu(_dot3(xb[...], w1_ref[0])) * _dot3(xb[...], w3_ref[0])
        acc[...] += _dot3(c * h, w2_ref[0])
    else:
        h = _silu(_dot(xb[...], w1_ref[0])) * _dot(xb[...], w3_ref[0])
        acc[...] += _dot((c * h).astype(BF16), w2_ref[0])

    @pl.when(e == N_EXPERTS - 1)
    def _():
        o_ref[...] = _ln(ALPHA * x_ref[...] + acc[...], g_ref[...], b_ref[...])


def _moe(x, wr, br, w1, w3, w2, g, b, layer, tm, precise=False):
    m = x.shape[0]
    wspec = pl.BlockSpec((None, 1, D_MODEL, D_MODEL), lambda i, e: (layer, e, 0, 0))
    return pl.pallas_call(
        functools.partial(_moe_kernel, precise=precise),
        grid=(m // tm, N_EXPERTS),
        in_specs=[
            pl.BlockSpec((tm, D_MODEL), lambda i, e: (i, 0)),
            _const_spec((D_MODEL, LANES)),
            _const_spec((1, LANES)),
            wspec, wspec, wspec,
            _const_spec((1, D_MODEL)),
            _const_spec((1, D_MODEL)),
        ],
        out_specs=pl.BlockSpec((tm, D_MODEL), lambda i, e: (i, 0)),
        out_shape=jax.ShapeDtypeStruct((m, D_MODEL), F32),
        scratch_shapes=[pltpu.VMEM((tm, D_MODEL), F32), pltpu.VMEM((tm, LANES), F32),
                        pltpu.VMEM((tm, D_MODEL), F32 if precise else BF16)],
        compiler_params=_params(2),
        name="moe",
    )(x, wr, br, w1, w3, w2, g, b)


def kernel(x_prompt, x_sample, cache_k, cache_v, cache_kidx, state_conv, page_table, ln1_g, ln1_b, ln2_g, ln2_b,
           w_in_even, conv_w, sgu_ln_g, sgu_ln_b, sgu_w, sgu_b, w_out_even, ffn_w1, ffn_w3, ffn_w2, w_in_odd,
           w_out_odd, router_w, router_b, moe_w1, moe_w3, moe_w2):
    bp, tp, _ = x_prompt.shape
    ns = x_sample.shape[0]
    n_pool = cache_k.shape[1]
    past = page_table.shape[1] * PAGE_SIZE

    xp = x_prompt.reshape(bp * tp, D_MODEL)
    xs = x_sample.reshape(ns, D_MODEL)
    tab_p, tabt_p = _rope_tables(jnp.arange(tp))
    tab_s, tabt_s = _rope_tables(jnp.full((ns,), past, jnp.int32))
    ck_t = jnp.transpose(cache_k, (0, 1, 3, 4, 2)).reshape(-1, n_pool, KV_DIM, PAGE_SIZE)
    cv_t = jnp.transpose(cache_v, (0, 1, 3, 4, 2)).reshape(-1, n_pool, KV_DIM, PAGE_SIZE)
    cki_t = jnp.swapaxes(cache_kidx, 2, 3)
    row = lambda a: a.reshape(1, -1)
    head_kv = (jnp.arange(N_HEADS)[:, None] // GROUP == jnp.arange(N_KV_HEADS)[None, :]).astype(BF16)
    w_in_even_b, w_out_even_b, w_out_odd_b = w_in_even.astype(BF16), w_out_even.astype(BF16), w_out_odd.astype(BF16)
    ffn_w1_b, ffn_w3_b, ffn_w2_b = ffn_w1.astype(BF16), ffn_w3.astype(BF16), ffn_w2.astype(BF16)
    moe_w1_b, moe_w3_b, moe_w2_b = moe_w1.astype(BF16), moe_w3.astype(BF16), moe_w2.astype(BF16)

    k_p, v_p, ki_p, conv_p = [], [], [], []
    k_s, v_s, ki_s, conv_s, chunk_s = [], [], [], [], []
    for layer in range(DEPTH):
        i = layer // 2
        g1, b1, g2, b2 = row(ln1_g[layer]), row(ln1_b[layer]), row(ln2_g[layer]), row(ln2_b[layer])
        if layer % 2 == 0:
            sg, sbn = row(sgu_ln_g[i]), row(sgu_ln_b[i])
            bs_b = jnp.broadcast_to(sgu_b[i][:, :, None], (SGU_GROUPS, CHUNK, CHUNK))
            xp, cp = _even_mixer(xp, bp, tp, w_in_even_b, conv_w[i], sg, sbn, sgu_w[i], bs_b, w_out_even_b, g1, b1, i)
            wd = row(jnp.repeat(sgu_w[i][:, 0, 0], CHUNK))
            b0 = row(jnp.repeat(sgu_b[i][:, 0], CHUNK))
            xs, zs, vs = _even_mixer_step(xs, state_conv[i][:, 0], state_conv[i][:, 1], w_in_even, conv_w[i],
                                          sg, sbn, wd, b0, w_out_even, g1, b1, i)
            conv_p.append(cp)
            conv_s.append(jnp.stack([state_conv[i][:, 1], zs], axis=1))
            chunk_s.append(vs.reshape(ns, 1, SGU_CH))
            xp = _ffn(xp, ffn_w1_b, ffn_w3_b, ffn_w2_b, g2, b2, i, tm=1024)
            xs = _ffn_step(xs, ffn_w1, ffn_w3, ffn_w2, g2, b2, i)
        else:
            w = w_in_odd[i]
            wq32 = jnp.concatenate(
                [w[:, :Q_END], w[:, V_END:QI_END],
                 jnp.pad(w[:, QI_END + IDX_DIM:], ((0, 0), (0, LANES - IDX_HEADS)))], axis=1)
            wkt32 = jnp.concatenate([w[:, Q_END:V_END], w[:, QI_END:QI_END + IDX_DIM]], axis=1).T
            wq, wkt = wq32.astype(BF16), wkt32.astype(BF16)
            qp, qip, wip, ktp, vtp, kitp, ktb, vtb, kitb = _odd_proj(xp, wq, wkt, tab_p, tabt_p, bp, tp, tm=512)
            head_major = (bp, N_KV_HEADS, HEAD_DIM, tp)
            xp = _dsa_prompt(xp, qp, qip, wip, kitb, ktb.reshape(head_major), vtb.reshape(head_major), w_out_odd_b,
                             g1, b1, bp, tp, i)
            k_p.append(ktp)
            v_p.append(vtp)
            ki_p.append(kitp)

            qs, qis, wis, kts, vts, kits, _, _, _ = _odd_proj(xs, wq32, wkt32, tab_s, tabt_s, 1, ns, tm=ns,
                                                              precise=True)
            kn, vn, kis = kts[0].T, vts[0].T, kits[0].T
            sp, sn = _dsa_step_score(page_table, qis.reshape(ns, IDX_HEADS, IDX_DIM),
                                     wis[:, :IDX_HEADS].reshape(ns, IDX_HEADS, 1), kis.reshape(ns, 1, IDX_DIM),
                                     cki_t, i)
            qbd = (qs.astype(BF16).reshape(ns, N_HEADS, 1, HEAD_DIM)
                   * head_kv[None, :, :, None]).reshape(ns, N_HEADS, KV_DIM)
            bias_s, biasn_s = _dsa_step_select(sp.reshape(ns, past), sn.reshape(ns, LANES))
            a_s = _dsa_step_attend(page_table, bias_s.reshape(ns, 1, past), biasn_s.reshape(ns, 1, LANES), qbd,
                                   kn.reshape(ns, 1, KV_DIM), vn.reshape(ns, 1, KV_DIM), ck_t, cv_t, i)
            xs = _proj_ln(a_s.reshape(ns, Q_END), xs, w_out_odd, g1, b1, i)
            k_s.append(kn.reshape(ns, 1, N_KV_HEADS, HEAD_DIM))
            v_s.append(vn.reshape(ns, 1, N_KV_HEADS, HEAD_DIM))
            ki_s.append(kis.reshape(ns, 1, IDX_DIM))

            wr = jnp.pad(router_w[i], ((0, 0), (0, LANES - N_EXPERTS)))
            br = jnp.pad(row(router_b[i]), ((0, 0), (0, LANES - N_EXPERTS)))
            xp = _moe(xp, wr, br, moe_w1_b, moe_w3_b, moe_w2_b, g2, b2, i, tm=1024)
            xs = _moe(xs, wr, br, moe_w1, moe_w3, moe_w2, g2, b2, i, tm=ns, precise=True)

    n_odd = len(k_p)
    kv_shape = (n_odd, bp, N_KV_HEADS, HEAD_DIM, tp)
    k_prompt = jnp.stack(k_p).reshape(kv_shape).transpose(0, 1, 4, 2, 3)
    v_prompt = jnp.stack(v_p).reshape(kv_shape).transpose(0, 1, 4, 2, 3)
    kidx_prompt = jnp.stack(ki_p).transpose(0, 1, 3, 2)
    return (xp.reshape(bp, tp, D_MODEL), xs.reshape(ns, 1, D_MODEL),
            k_prompt, v_prompt, kidx_prompt, jnp.stack(conv_p),
            jnp.stack(k_s), jnp.stack(v_s), jnp.stack(ki_s), jnp.stack(conv_s), jnp.stack(chunk_s))
```

```python
import functools

import jax
import jax.numpy as jnp
import numpy as np
from jax import lax
from jax.experimental import pallas as pl
from jax.experimental.pallas import tpu as pltpu

D_MODEL = 1024
DEPTH = 4
PAGE_SIZE = 128
CONV_CH = 512
CONV_W = 3
SGU_CH = 512
SGU_GROUPS = 4
CHUNK = 128
N_HEADS = 16
HEAD_DIM = 64
N_KV_HEADS = 4
GROUP = N_HEADS // N_KV_HEADS
KV_DIM = N_KV_HEADS * HEAD_DIM
ROPE_DIM = 16
ROPE_THETA = 500000.0
IDX_HEADS = 8
IDX_DIM = 64
TOPK_MAX = 256
D_FF = 2816
N_EXPERTS = 8
ALPHA = (2 * DEPTH) ** 0.25
LN_EPS = 1e-5

EVEN_IN = 3 * CONV_CH + 2 * SGU_CH
Q_END = N_HEADS * HEAD_DIM
K_END = Q_END + KV_DIM
V_END = K_END + KV_DIM
QI_END = V_END + IDX_HEADS * IDX_DIM

LANES = 128
VMEM_LIMIT = 56 * 1024 * 1024
BF16 = jnp.bfloat16
F32 = jnp.float32


def _params(n_axes):
    return pltpu.CompilerParams(dimension_semantics=("arbitrary",) * n_axes, vmem_limit_bytes=VMEM_LIMIT)


def _const_spec(shape, buffers=None):
    nd = len(shape)
    mode = None if buffers is None else pl.Buffered(buffers)
    return pl.BlockSpec(shape, lambda *_: (0,) * nd, pipeline_mode=mode)


def _layer_spec(shape, layer, buffers=None):
    nd = len(shape)
    mode = None if buffers is None else pl.Buffered(buffers)
    return pl.BlockSpec((None,) + tuple(shape), lambda *_: (layer,) + (0,) * nd, pipeline_mode=mode)


def _ln(x, g, b):
    mu = jnp.mean(x, axis=-1, keepdims=True)
    xc = x - mu
    var = jnp.mean(xc * xc, axis=-1, keepdims=True)
    return xc * lax.rsqrt(var + LN_EPS) * g + b


def _dot(a, b):
    return jnp.dot(a, b, preferred_element_type=F32)


def _dot_nt(a, b):
    return lax.dot_general(a, b, (((1,), (1,)), ((), ())), preferred_element_type=F32)


def _split(a):
    hi = a.astype(BF16)
    return hi, (a - hi.astype(F32)).astype(BF16)


def _dot3(a, b, dot=_dot):
    ah, al = _split(a)
    bh, bl = _split(b)
    return dot(ah, bh) + (dot(ah, bl) + dot(al, bh))


def _silu(x):
    return x * (1.0 / (1.0 + jnp.exp(-x)))


BISECT_STEPS = 22


def _kth_largest(count_gt, min_above, row_min, row_max, k):
    lo0 = row_min - jnp.maximum(jnp.abs(row_min), 1.0)

    def halve(i, c):
        lo, hi = c
        mid = lo + (hi - lo) * 0.5
        below = count_gt(mid) < k
        return jnp.where(below, lo, mid), jnp.where(below, mid, hi)

    lo, _ = lax.fori_loop(0, BISECT_STEPS, halve, (lo0, row_max))
    tau = min_above(lo)
    above = count_gt(tau)

    def unsettled(c):
        return jnp.max((c[2] >= k).astype(F32)) > 0.0

    def advance(c):
        lo, tau, above = c
        lo = jnp.where(above >= k, tau, lo)
        tau = min_above(lo)
        return lo, tau, count_gt(tau)

    _, tau, above = lax.while_loop(unsettled, advance, (lo, tau, above))
    return tau, above


def _tie_limit(count_lt, need, rows, n_bits):
    def body(i, p):
        cand = p + jnp.left_shift(jnp.int32(1), jnp.int32(n_bits - 1) - i)
        return jnp.where(count_lt(cand) < need, cand, p)
    return lax.fori_loop(0, n_bits, body, jnp.zeros((rows, 1), jnp.int32))


def _even_mixer_kernel(x_ref, win_ref, cw_ref, sg_ref, sb_ref, wm_ref, bs_ref, wout_ref, g_ref, b_ref,
                       o_ref, cs_ref, zbuf, ybuf, *, tm):
    t = pl.program_id(1)
    x = x_ref[...]
    h = _dot(x.astype(BF16), win_ref[...])
    gate_b = h[:, 0:CONV_CH]
    z = h[:, CONV_CH:2 * CONV_CH] * h[:, 2 * CONV_CH:3 * CONV_CH]
    u = h[:, 3 * CONV_CH:3 * CONV_CH + SGU_CH]
    v = h[:, 3 * CONV_CH + SGU_CH:]

    @pl.when(t == 0)
    def _():
        zbuf[0:8, :] = jnp.zeros((8, CONV_CH), F32)

    zbuf[8:8 + tm, :] = z
    conv = cw_ref[0:1, :] * zbuf[6:6 + tm, :] + cw_ref[1:2, :] * zbuf[7:7 + tm, :] + cw_ref[2:3, :] * z
    ybuf[:, 0:CONV_CH] = (gate_b * conv).astype(BF16)
    cs_ref[0] = zbuf[tm + 6:tm + 8, :]
    zbuf[0:8, :] = zbuf[tm:tm + 8, :]

    vn = _ln(v, sg_ref[...], sb_ref[...])
    row = lax.broadcasted_iota(jnp.int32, (CHUNK, CHUNK), 0)
    col = lax.broadcasted_iota(jnp.int32, (CHUNK, CHUNK), 1)
    for g in range(SGU_GROUPS):
        wm = jnp.where(row >= col, wm_ref[g], 0.0).astype(BF16)
        for c in range(tm // CHUNK):
            rs = slice(c * CHUNK, (c + 1) * CHUNK)
            cs = slice(g * CHUNK, (g + 1) * CHUNK)
            s = _dot(wm, vn[rs, cs].astype(BF16)) + bs_ref[g]
            ybuf[rs, CONV_CH + g * CHUNK:CONV_CH + (g + 1) * CHUNK] = (u[rs, cs] * s).astype(BF16)

    m = _dot(ybuf[...], wout_ref[...])
    o_ref[...] = _ln(ALPHA * x + m, g_ref[...], b_ref[...])


def _even_mixer(x, bsz, t, w_in, conv_w, sgu_g, sgu_bn, sgu_w, bs_b, w_out, g, b, layer, tm=512):
    nt = t // tm
    kern = functools.partial(_even_mixer_kernel, tm=tm)
    return pl.pallas_call(
        kern,
        grid=(bsz, nt),
        in_specs=[
            pl.BlockSpec((tm, D_MODEL), lambda i, j: (i * nt + j, 0)),
            _layer_spec((D_MODEL, EVEN_IN), layer),
            _const_spec((CONV_W, CONV_CH)),
            _const_spec((1, SGU_CH)),
            _const_spec((1, SGU_CH)),
            _const_spec((SGU_GROUPS, CHUNK, CHUNK)),
            _const_spec((SGU_GROUPS, CHUNK, CHUNK)),
            _layer_spec((D_MODEL, D_MODEL), layer),
            _const_spec((1, D_MODEL)),
            _const_spec((1, D_MODEL)),
        ],
        out_specs=[
            pl.BlockSpec((tm, D_MODEL), lambda i, j: (i * nt + j, 0)),
            pl.BlockSpec((1, CONV_W - 1, CONV_CH), lambda i, j: (i, 0, 0)),
        ],
        out_shape=[
            jax.ShapeDtypeStruct((bsz * t, D_MODEL), F32),
            jax.ShapeDtypeStruct((bsz, CONV_W - 1, CONV_CH), F32),
        ],
        scratch_shapes=[pltpu.VMEM((tm + 8, CONV_CH), F32), pltpu.VMEM((tm, D_MODEL), BF16)],
        compiler_params=_params(2),
        name="even_mixer",
    )(x, w_in, conv_w, sgu_g, sgu_bn, sgu_w, bs_b, w_out, g, b)


def _even_mixer_step_kernel(x_ref, p0_ref, p1_ref, win_ref, cw_ref, sg_ref, sb_ref, wd_ref, b0_ref, wout_ref,
                            g_ref, b_ref, o_ref, z_ref, vn_ref):
    x = x_ref[...]
    gate_b, gate_c, h_in, u, v = [_dot3(x, win_ref[:, j * CONV_CH:(j + 1) * CONV_CH]) for j in range(5)]
    z = gate_c * h_in
    conv = cw_ref[0:1, :] * p0_ref[...] + cw_ref[1:2, :] * p1_ref[...] + cw_ref[2:3, :] * z
    vn = _ln(v, sg_ref[...], sb_ref[...])
    s = wd_ref[...] * vn + b0_ref[...]
    y = jnp.concatenate([gate_b * conv, u * s], axis=-1)
    m = _dot3(y, wout_ref[...])
    o_ref[...] = _ln(ALPHA * x + m, g_ref[...], b_ref[...])
    z_ref[...] = z
    vn_ref[...] = vn


def _even_mixer_step(x, p0, p1, w_in, conv_w, sgu_g, sgu_bn, wd, b0, w_out, g, b, layer):
    n = x.shape[0]
    c = _const_spec
    in_specs = [c((n, D_MODEL)), c((n, CONV_CH)), c((n, CONV_CH)), _layer_spec((D_MODEL, EVEN_IN), layer),
                c((CONV_W, CONV_CH)), c((1, SGU_CH)), c((1, SGU_CH)), c((1, SGU_CH)), c((1, SGU_CH)),
                _layer_spec((D_MODEL, D_MODEL), layer), c((1, D_MODEL)), c((1, D_MODEL))]
    return pl.pallas_call(
        _even_mixer_step_kernel,
        grid=(1,),
        in_specs=in_specs,
        out_specs=[_const_spec((n, D_MODEL)), _const_spec((n, CONV_CH)), _const_spec((n, SGU_CH))],
        out_shape=[jax.ShapeDtypeStruct((n, D_MODEL), F32), jax.ShapeDtypeStruct((n, CONV_CH), F32),
                   jax.ShapeDtypeStruct((n, SGU_CH), F32)],
        compiler_params=_params(1),
        name="even_mixer_step",
    )(x, p0, p1, w_in, conv_w, sgu_g, sgu_bn, wd, b0, w_out, g, b)


FF_TILE = 256


def _ffn_kernel(x_ref, w1_ref, w3_ref, w2_ref, g_ref, b_ref, o_ref, hbuf):
    x = x_ref[...]
    xb = x.astype(BF16)
    for f in range(D_FF // FF_TILE):
        cs = slice(f * FF_TILE, (f + 1) * FF_TILE)
        hbuf[:, cs] = (_silu(_dot(xb, w1_ref[:, cs])) * _dot(xb, w3_ref[:, cs])).astype(BF16)
    y = _dot(hbuf[...], w2_ref[...])
    o_ref[...] = _ln(ALPHA * x + y, g_ref[...], b_ref[...])


def _ffn(x, w1, w3, w2, g, b, layer, tm):
    m = x.shape[0]
    return pl.pallas_call(
        _ffn_kernel,
        grid=(m // tm,),
        in_specs=[
            pl.BlockSpec((tm, D_MODEL), lambda i: (i, 0)),
            _layer_spec((D_MODEL, D_FF), layer, buffers=1),
            _layer_spec((D_MODEL, D_FF), layer, buffers=1),
            _layer_spec((D_FF, D_MODEL), layer, buffers=1),
            _const_spec((1, D_MODEL)),
            _const_spec((1, D_MODEL)),
        ],
        out_specs=pl.BlockSpec((tm, D_MODEL), lambda i: (i, 0)),
        out_shape=jax.ShapeDtypeStruct((m, D_MODEL), F32),
        scratch_shapes=[pltpu.VMEM((tm, D_FF), BF16)],
        compiler_params=_params(1),
        name="ffn",
    )(x, w1, w3, w2, g, b)


def _ffn_step_kernel(x_ref, w1_ref, w3_ref, w2_ref, g_ref, b_ref, o_ref, acc):
    f = pl.program_id(0)

    @pl.when(f == 0)
    def _():
        acc[...] = jnp.zeros(acc.shape, F32)

    x = x_ref[...]
    h = _silu(_dot3(x, w1_ref[...])) * _dot3(x, w3_ref[...])
    acc[...] += _dot3(h, w2_ref[...])

    @pl.when(f == pl.num_programs(0) - 1)
    def _():
        o_ref[...] = _ln(ALPHA * x + acc[...], g_ref[...], b_ref[...])


def _ffn_step(x, w1, w3, w2, g, b, layer):
    n = x.shape[0]
    return pl.pallas_call(
        _ffn_step_kernel,
        grid=(D_FF // FF_TILE,),
        in_specs=[
            _const_spec((n, D_MODEL)),
            pl.BlockSpec((None, D_MODEL, FF_TILE), lambda f: (layer, 0, f)),
            pl.BlockSpec((None, D_MODEL, FF_TILE), lambda f: (layer, 0, f)),
            pl.BlockSpec((None, FF_TILE, D_MODEL), lambda f: (layer, f, 0)),
            _const_spec((1, D_MODEL)),
            _const_spec((1, D_MODEL)),
        ],
        out_specs=_const_spec((n, D_MODEL)),
        out_shape=jax.ShapeDtypeStruct((n, D_MODEL), F32),
        scratch_shapes=[pltpu.VMEM((n, D_MODEL), F32)],
        compiler_params=_params(1),
        name="ffn_step",
    )(x, w1, w3, w2, g, b)


QQI = Q_END + IDX_HEADS * IDX_DIM
KVT_ROWS = 2 * KV_DIM + IDX_DIM


def _rope(xc, c, sa, sb):
    return xc * c + pltpu.roll(xc, 8, 1) * sa + pltpu.roll(xc, LANES - 8, 1) * sb


def _rope_rows(ht, r0, cos, sin, out_ref, o0, n_rows):
    x1, x2 = ht[r0:r0 + 8, :], ht[r0 + 8:r0 + 16, :]
    out_ref[o0:o0 + 8, :] = x1 * cos - x2 * sin
    out_ref[o0 + 8:o0 + 16, :] = x1 * sin + x2 * cos
    out_ref[o0 + 16:o0 + n_rows, :] = ht[r0 + 16:r0 + n_rows, :]


def _odd_proj_kernel(x_ref, wq_ref, wkt_ref, tab_ref, tabt_ref,
                     q_ref, qi_ref, wi_ref, kt_ref, vt_ref, kit_ref, ktb_ref, vtb_ref, kitb_ref, *, precise):
    if precise:
        x = x_ref[...]
        h = _dot3(x, wq_ref[...])
        ht = _dot3(wkt_ref[...], x, dot=_dot_nt)
    else:
        xb = x_ref[...].astype(BF16)
        h = _dot(xb, wq_ref[...])
        ht = _dot_nt(wkt_ref[...], xb)
    c, sa, sb = tab_ref[:, 0:LANES], tab_ref[:, LANES:2 * LANES], tab_ref[:, 2 * LANES:3 * LANES]
    q_scale = HEAD_DIM ** -0.5 if precise else float(HEAD_DIM ** -0.5 * np.log2(np.e))
    for j in range(Q_END // LANES):
        cs = slice(j * LANES, (j + 1) * LANES)
        q_ref[:, cs] = (_rope(h[:, cs], c, sa, sb) * q_scale).astype(q_ref.dtype)
    for j in range(IDX_HEADS * IDX_DIM // LANES):
        cs = slice(j * LANES, (j + 1) * LANES)
        qi_ref[:, cs] = (_rope(h[:, Q_END + j * LANES:Q_END + (j + 1) * LANES], c, sa, sb)
                         * (IDX_DIM ** -0.5)).astype(qi_ref.dtype)
    wi_ref[...] = h[:, QQI:QQI + LANES] * (IDX_HEADS ** -0.5)

    cos, sin = tabt_ref[0:8, :], tabt_ref[8:16, :]
    for hd in range(N_KV_HEADS):
        _rope_rows(ht, hd * HEAD_DIM, cos, sin, kt_ref, hd * HEAD_DIM, HEAD_DIM)
    vt_ref[...] = ht[KV_DIM:2 * KV_DIM, :]
    _rope_rows(ht, 2 * KV_DIM, cos, sin, kit_ref, 0, IDX_DIM)
    ktb_ref[...] = kt_ref[...].astype(BF16)
    vtb_ref[...] = vt_ref[...].astype(BF16)
    kitb_ref[...] = kit_ref[...].astype(BF16)


def _odd_proj(x, wq, wkt, tab, tabt, bsz, t, tm, precise=False):
    nt = t // tm
    tok = lambda i: (i, 0)
    feat = lambda i: (i // nt, 0, i % nt)
    q_dtype = F32 if precise else BF16
    row_widths = [(Q_END, q_dtype), (IDX_HEADS * IDX_DIM, q_dtype), (LANES, F32)]
    feat_rows = [(KV_DIM, F32), (KV_DIM, F32), (IDX_DIM, F32), (KV_DIM, BF16), (KV_DIM, BF16), (IDX_DIM, BF16)]
    return pl.pallas_call(
        functools.partial(_odd_proj_kernel, precise=precise),
        grid=(bsz * nt,),
        in_specs=[
            pl.BlockSpec((tm, D_MODEL), tok),
            _const_spec((D_MODEL, QQI + LANES)),
            _const_spec((KVT_ROWS, D_MODEL)),
            pl.BlockSpec((tm, 3 * LANES), lambda i: (i % nt, 0)),
            pl.BlockSpec((ROPE_DIM, tm), lambda i: (0, i % nt)),
        ],
        out_specs=[pl.BlockSpec((tm, w_), tok) for w_, _ in row_widths]
        + [pl.BlockSpec((None, r_, tm), feat) for r_, _ in feat_rows],
        out_shape=[jax.ShapeDtypeStruct((bsz * t, w_), d_) for w_, d_ in row_widths]
        + [jax.ShapeDtypeStruct((bsz, r_, t), d_) for r_, d_ in feat_rows],
        compiler_params=_params(1),
        name="odd_proj",
    )(x, wq, wkt, tab, tabt)


def _rope_tables(pos):
    half = ROPE_DIM // 2
    inv = ROPE_THETA ** (-jnp.arange(half, dtype=F32) / half)
    ang = pos.astype(F32)[:, None] * inv[None, :]
    cos, sin = jnp.cos(ang), jnp.sin(ang)
    n = pos.shape[0]
    one = jnp.ones((n, HEAD_DIM - ROPE_DIM), F32)
    zero = jnp.zeros((n, HEAD_DIM - ROPE_DIM), F32)
    zh = jnp.zeros((n, half), F32)
    c64 = jnp.concatenate([cos, cos, one], axis=-1)
    sa64 = jnp.concatenate([zh, sin, zero], axis=-1)
    sb64 = jnp.concatenate([-sin, zh, zero], axis=-1)
    tab = jnp.concatenate([c64, c64, sa64, sa64, sb64, sb64], axis=-1)
    tabt = jnp.concatenate([cos.T, sin.T], axis=0)
    return tab, tabt


KEY_BLOCK = 512
COUNT_ROWS = 64


def _lane_fold(a, op):
    parts = [a[:, c * LANES:(c + 1) * LANES] for c in range(a.shape[1] // LANES)]
    while len(parts) > 1:
        parts = [op(parts[i], parts[i + 1]) for i in range(0, len(parts), 2)]
    return parts[0]


def _dsa_prompt_kernel(x_ref, q_ref, qi_ref, wi_ref, kit_ref, kt_ref, vt_ref, wout_ref, g_ref, b_ref,
                       o_ref, key_ref, bias_ref, lbuf, wib, q3, a3, mbuf, obuf, thr_ref, cnt_ref, ext_ref,
                       *, tq, topk):
    t = pl.program_id(1)
    n_blk = (t * tq) // KEY_BLOCK + 1

    def blk(kb):
        return pl.ds(pl.multiple_of(kb * KEY_BLOCK, KEY_BLOCK), KEY_BLOCK)

    def bcast(col):
        return jnp.broadcast_to(col, (tq, LANES))

    def chunks(a):
        return [a[:, c * LANES:(c + 1) * LANES] for c in range(KEY_BLOCK // LANES)]

    row = lax.broadcasted_iota(jnp.int32, (tq, KEY_BLOCK), 0)
    lane = lax.broadcasted_iota(jnp.int32, (tq, KEY_BLOCK), 1)

    for h in range(IDX_HEADS):
        wib[h] = bcast(wi_ref[:, h:h + 1])
    for h in range(N_HEADS):
        q3[h // GROUP, (h % GROUP) * tq:(h % GROUP + 1) * tq, :] = q_ref[:, h * HEAD_DIM:(h + 1) * HEAD_DIM]

    ext_ref[0] = jnp.full((tq, LANES), -jnp.inf, F32)
    ext_ref[1] = jnp.full((tq, LANES), jnp.inf, F32)

    def score_body(kb, carry):
        ki = kit_ref[:, blk(kb)]
        score = [jnp.zeros((tq, LANES), F32)] * (KEY_BLOCK // LANES)
        for h in range(IDX_HEADS):
            d = _dot(qi_ref[:, h * IDX_DIM:(h + 1) * IDX_DIM], ki)
            w = wib[h]
            score = [s_ + w * jnp.maximum(d_, 0.0) for s_, d_ in zip(score, chunks(d))]
        score = jnp.concatenate(score, axis=-1)
        adm = kb * KEY_BLOCK + lane <= t * tq + row
        key_ref[:, blk(kb)] = jnp.where(adm, score, -jnp.inf)
        ext_ref[0] = jnp.maximum(ext_ref[0], _lane_fold(jnp.where(adm, score, -jnp.inf), jnp.maximum))
        ext_ref[1] = jnp.minimum(ext_ref[1], _lane_fold(jnp.where(adm, score, jnp.inf), jnp.minimum))
        return carry

    lax.fori_loop(0, n_blk, score_body, 0)
    row_max = jnp.max(ext_ref[0], axis=-1, keepdims=True)
    row_min = jnp.min(ext_ref[1], axis=-1, keepdims=True)

    def sweep(thr, init, step, finish):
        thr_ref[...] = bcast(thr)
        cnt_ref[...] = jnp.full((tq, LANES), init, F32)

        def body(kb, c_):
            for r in range(tq // COUNT_ROWS):
                rs = slice(r * COUNT_ROWS, (r + 1) * COUNT_ROWS)
                th = thr_ref[rs, :]
                acc = cnt_ref[rs, :]
                for k_ in chunks(key_ref[rs, blk(kb)]):
                    acc = step(acc, k_, th)
                cnt_ref[rs, :] = acc
            return c_

        lax.fori_loop(0, n_blk, body, 0)
        return finish(cnt_ref[...], axis=-1, keepdims=True)

    def count(cmp, thr):
        return sweep(thr, 0.0, lambda acc, k_, th: acc + cmp(k_, th).astype(F32), jnp.sum)

    def min_above(thr):
        return sweep(thr, jnp.inf, lambda acc, k_, th: jnp.minimum(acc, jnp.where(k_ > th, k_, jnp.inf)), jnp.min)

    tau, above = _kth_largest(functools.partial(count, jnp.greater), min_above, row_min, row_max, float(topk))
    taub = bcast(tau)
    need = float(topk) - above
    n_eq = count(jnp.equal, tau)
    crowded = jnp.max((n_eq > need).astype(F32)) > 0.0

    @pl.when(jnp.logical_not(crowded))
    def _():
        def bias_body(kb, carry):
            out = [jnp.where((k_ >= taub) & (k_ > -jnp.inf), 0.0, -jnp.inf)
                   for k_ in chunks(key_ref[:, blk(kb)])]
            bias_ref[:, blk(kb)] = jnp.concatenate(out, axis=-1)
            return carry
        lax.fori_loop(0, n_blk, bias_body, 0)

    @pl.when(crowded)
    def _():
        r_ = lax.broadcasted_iota(jnp.int32, (KEY_BLOCK, KEY_BLOCK), 0)
        c_ = lax.broadcasted_iota(jnp.int32, (KEY_BLOCK, KEY_BLOCK), 1)
        before = jnp.where(r_ < c_, 1.0, 0.0).astype(BF16)
        needb = bcast(need)

        def bias_body(kb, seen):
            ks = chunks(key_ref[:, blk(kb)])
            tie = [jnp.where(k_ == taub, 1.0, 0.0) for k_ in ks]
            rank = chunks(_dot(jnp.concatenate(tie, axis=-1).astype(BF16), before))
            out = []
            for k_, t_, r in zip(ks, tie, rank):
                keep = ((k_ > taub) | ((t_ > 0.0) & (seen + r < needb))) & (k_ > -jnp.inf)
                out.append(jnp.where(keep, 0.0, -jnp.inf))
            bias_ref[:, blk(kb)] = jnp.concatenate(out, axis=-1)
            return seen + bcast(jnp.sum(functools.reduce(jnp.add, tie), axis=-1, keepdims=True))

        lax.fori_loop(0, n_blk, bias_body, jnp.zeros((tq, LANES), F32))

    def group_body(kv, carry):
        qg = q3[kv]
        mbuf[...] = jnp.full(mbuf.shape, -jnp.inf, F32)
        obuf[...] = jnp.zeros(obuf.shape, F32)

        def logit_body(kb, c_):
            logit = _dot(qg, kt_ref[kv, :, blk(kb)])
            bias = bias_ref[:, blk(kb)]
            for j in range(GROUP):
                rows = slice(j * tq, (j + 1) * tq)
                lj = logit[rows, :] + bias
                lbuf[rows, blk(kb)] = lj
                mbuf[rows, :] = jnp.maximum(mbuf[rows, :], _lane_fold(lj, jnp.maximum))
            return c_

        lax.fori_loop(0, n_blk, logit_body, 0)
        mbs = [bcast(jnp.max(mbuf[j * tq:(j + 1) * tq, :], axis=-1, keepdims=True)) for j in range(GROUP)]

        def pv_body(kb, c_):
            pj = []
            for j in range(GROUP):
                rows = slice(j * tq, (j + 1) * tq)
                ps = [jnp.exp2(l_ - mbs[j]) for l_ in chunks(lbuf[rows, blk(kb)])]
                pj.append(jnp.concatenate(ps, axis=-1).astype(BF16))
            v1 = jnp.concatenate([vt_ref[kv, :, blk(kb)], jnp.ones((LANES - HEAD_DIM, KEY_BLOCK), BF16)], axis=0)
            obuf[...] += _dot_nt(jnp.concatenate(pj, axis=0), v1)
            return c_

        lax.fori_loop(0, n_blk, pv_body, 0)
        a3[kv] = (obuf[:, 0:HEAD_DIM] / obuf[:, HEAD_DIM:HEAD_DIM + 1]).astype(BF16)
        return carry

    lax.fori_loop(0, N_KV_HEADS, group_body, 0)

    a = jnp.concatenate([a3[h // GROUP, (h % GROUP) * tq:(h % GROUP + 1) * tq, :] for h in range(N_HEADS)], axis=-1)
    m = _dot(a, wout_ref[...])
    o_ref[...] = _ln(ALPHA * x_ref[...] + m, g_ref[...], b_ref[...])


def _dsa_prompt(x, q, qi, wi, kit, kt, vt, w_out, g, b, bsz, t, layer, tq=256):
    nq = t // tq
    topk = min(TOPK_MAX, t // 4)
    kern = functools.partial(_dsa_prompt_kernel, tq=tq, topk=topk)
    qmap = lambda i, j: (i * nq + j, 0)
    return pl.pallas_call(
        kern,
        grid=(bsz, nq),
        in_specs=[
            pl.BlockSpec((tq, D_MODEL), qmap),
            pl.BlockSpec((tq, Q_END), qmap),
            pl.BlockSpec((tq, IDX_HEADS * IDX_DIM), qmap),
            pl.BlockSpec((tq, LANES), qmap),
            pl.BlockSpec((None, IDX_DIM, t), lambda i, j: (i, 0, 0)),
            pl.BlockSpec((None, N_KV_HEADS, HEAD_DIM, t), lambda i, j: (i, 0, 0, 0)),
            pl.BlockSpec((None, N_KV_HEADS, HEAD_DIM, t), lambda i, j: (i, 0, 0, 0)),
            _layer_spec((Q_END, D_MODEL), layer),
            _const_spec((1, D_MODEL)),
            _const_spec((1, D_MODEL)),
        ],
        out_specs=pl.BlockSpec((tq, D_MODEL), qmap),
        out_shape=jax.ShapeDtypeStruct((bsz * t, D_MODEL), F32),
        scratch_shapes=[
            pltpu.VMEM((tq, t), F32),
            pltpu.VMEM((tq, t), F32),
            pltpu.VMEM((GROUP * tq, t), F32),
            pltpu.VMEM((IDX_HEADS, tq, LANES), F32),
            pltpu.VMEM((N_KV_HEADS, GROUP * tq, HEAD_DIM), BF16),
            pltpu.VMEM((N_KV_HEADS, GROUP * tq, HEAD_DIM), BF16),
            pltpu.VMEM((GROUP * tq, LANES), F32),
            pltpu.VMEM((GROUP * tq, LANES), F32),
            pltpu.VMEM((tq, LANES), F32),
            pltpu.VMEM((tq, LANES), F32),
            pltpu.VMEM((2, tq, LANES), F32),
        ],
        compiler_params=_params(2),
        name="dsa_prompt",
    )(x, q, qi, wi, kit, kt, vt, w_out, g, b)


IDX_PAGES_PER_STEP = 32
KV_PAGES_PER_STEP = 16


def _page_specs(layer, n_rows, pages_per_step):
    def spec(j):
        return pl.BlockSpec((None, None, n_rows, PAGE_SIZE),
                            lambda i, s, pt: (layer, pt[i, s * pages_per_step + j], 0, 0))
    return [spec(j) for j in range(pages_per_step)]


def _dsa_step_score_kernel(pt_ref, qi_ref, wi_ref, kin_ref, *rest):
    pages = rest[:IDX_PAGES_PER_STEP]
    o_ref, on_ref, kbuf_hi, kbuf_lo = rest[IDX_PAGES_PER_STEP:]
    for j, p in enumerate(pages):
        hi, lo = _split(p[...])
        kbuf_hi[:, j * PAGE_SIZE:(j + 1) * PAGE_SIZE] = hi
        kbuf_lo[:, j * PAGE_SIZE:(j + 1) * PAGE_SIZE] = lo
    qi = qi_ref[0]
    wi = wi_ref[0]
    qh, ql = _split(qi)
    d = _dot(qh, kbuf_hi[...]) + (_dot(qh, kbuf_lo[...]) + _dot(ql, kbuf_hi[...]))
    o_ref[0] = jnp.sum(wi * jnp.maximum(d, 0.0), axis=0, keepdims=True)
    dn = jnp.sum(qi * kin_ref[0], axis=-1, keepdims=True)
    sn = jnp.sum(wi * jnp.maximum(dn, 0.0), axis=0, keepdims=True)
    on_ref[0] = jnp.broadcast_to(sn, (1, LANES))


def _dsa_step_score(page_table, qi3, wi3, kin3, cache_kidx_t, layer):
    n, n_pages = page_table.shape
    steps = n_pages // IDX_PAGES_PER_STEP
    step_keys = IDX_PAGES_PER_STEP * PAGE_SIZE
    row = lambda i, s, pt: (i, 0, 0)
    grid_spec = pltpu.PrefetchScalarGridSpec(
        num_scalar_prefetch=1,
        grid=(n, steps),
        in_specs=[
            pl.BlockSpec((1, IDX_HEADS, IDX_DIM), row),
            pl.BlockSpec((1, IDX_HEADS, 1), row),
            pl.BlockSpec((1, 1, IDX_DIM), row),
        ] + _page_specs(layer, IDX_DIM, IDX_PAGES_PER_STEP),
        out_specs=[
            pl.BlockSpec((1, 1, step_keys), lambda i, s, pt: (i, 0, s)),
            pl.BlockSpec((1, 1, LANES), row),
        ],
        scratch_shapes=[pltpu.VMEM((IDX_DIM, step_keys), BF16), pltpu.VMEM((IDX_DIM, step_keys), BF16)],
    )
    return pl.pallas_call(
        _dsa_step_score_kernel,
        grid_spec=grid_spec,
        out_shape=[jax.ShapeDtypeStruct((n, 1, n_pages * PAGE_SIZE), F32),
                   jax.ShapeDtypeStruct((n, 1, LANES), F32)],
        compiler_params=_params(2),
        name="dsa_step_score",
    )(page_table, qi3, wi3, kin3, *([cache_kidx_t] * IDX_PAGES_PER_STEP))


def _dsa_step_select_kernel(sp_ref, sn_ref, bias_ref, biasn_ref, *, topk):
    n, past = sp_ref.shape
    lane = lax.broadcasted_iota(jnp.int32, (n, LANES), 1)
    s_new = sn_ref[:, 0:1]

    def count_gt(x):
        return jnp.sum((sp_ref[...] > x).astype(F32), axis=-1, keepdims=True) + (s_new > x).astype(F32)

    def min_above(x):
        sp = sp_ref[...]
        past_min = jnp.min(jnp.where(sp > x, sp, jnp.inf), axis=-1, keepdims=True)
        return jnp.minimum(past_min, jnp.where(s_new > x, s_new, jnp.inf))

    row_max = jnp.maximum(jnp.max(sp_ref[...], axis=-1, keepdims=True), s_new)
    row_min = jnp.minimum(jnp.min(sp_ref[...], axis=-1, keepdims=True), s_new)
    tau, above = _kth_largest(count_gt, min_above, row_min, row_max, float(topk))
    need = float(topk) - above

    def count_lt(p):
        idx = lax.broadcasted_iota(jnp.int32, (n, past), 1)
        return jnp.sum(((sp_ref[...] == tau) & (idx < p)).astype(F32), axis=-1, keepdims=True)

    lim = _tie_limit(count_lt, need, n, int(past).bit_length())
    sp = sp_ref[...]
    idx = lax.broadcasted_iota(jnp.int32, (n, past), 1)
    keep_p = (sp > tau) | ((sp == tau) & (idx <= lim))
    n_keep = jnp.sum(keep_p.astype(F32), axis=-1, keepdims=True)
    keep_n = (s_new > tau) | ((s_new == tau) & (n_keep < float(topk)))
    bias_ref[...] = jnp.where(keep_p, 0.0, -jnp.inf)
    biasn_ref[...] = jnp.where(keep_n & (lane == 0), 0.0, -jnp.inf)


def _dsa_step_select(sp, sn):
    n, past = sp.shape
    topk = min(TOPK_MAX, (past + 1) // 4)
    return pl.pallas_call(
        functools.partial(_dsa_step_select_kernel, topk=topk),
        grid=(1,),
        in_specs=[_const_spec((n, past)), _const_spec((n, LANES))],
        out_specs=[_const_spec((n, past)), _const_spec((n, LANES))],
        out_shape=[jax.ShapeDtypeStruct((n, past), F32), jax.ShapeDtypeStruct((n, LANES), F32)],
        compiler_params=_params(1),
        name="dsa_step_select",
    )(sp, sn)


def _dsa_step_attend_kernel(pt_ref, bias_ref, biasn_ref, qbd_ref, kn_ref, vn_ref, *rest):
    kp = rest[:KV_PAGES_PER_STEP]
    vp = rest[KV_PAGES_PER_STEP:2 * KV_PAGES_PER_STEP]
    o_ref, kbuf, vbuf, m_ref, l_ref, acc_ref = rest[2 * KV_PAGES_PER_STEP:]
    s = pl.program_id(1)
    n_steps = pl.num_programs(1)
    step_keys = KV_PAGES_PER_STEP * PAGE_SIZE

    @pl.when(s == 0)
    def _():
        m_ref[...] = jnp.full(m_ref.shape, -jnp.inf, F32)
        l_ref[...] = jnp.zeros(l_ref.shape, F32)
        acc_ref[...] = jnp.zeros(acc_ref.shape, F32)

    for j in range(KV_PAGES_PER_STEP):
        kbuf[:, j * PAGE_SIZE:(j + 1) * PAGE_SIZE] = kp[j][...].astype(BF16)
        vbuf[:, j * PAGE_SIZE:(j + 1) * PAGE_SIZE] = vp[j][...].astype(BF16)
    qbd = qbd_ref[0]
    logit = _dot(qbd, kbuf[...]) + bias_ref[0]
    m_new = jnp.maximum(m_ref[...], jnp.max(logit, axis=-1, keepdims=True))
    m_safe = jnp.where(m_new == -jnp.inf, 0.0, m_new)
    scale = jnp.exp(m_ref[...] - m_safe)
    p = jnp.exp(logit - m_safe)
    l_ref[...] = l_ref[...] * scale + jnp.sum(p, axis=-1, keepdims=True)
    acc_ref[...] = acc_ref[...] * scale + _dot_nt(p.astype(BF16), vbuf[...])
    m_ref[...] = m_new

    @pl.when(s == n_steps - 1)
    def _():
        kn = kn_ref[0].astype(BF16).astype(F32)
        vn = vn_ref[0].astype(BF16).astype(F32)
        ln_ = jnp.sum(qbd.astype(F32) * kn, axis=-1, keepdims=True) + biasn_ref[0][:, 0:1]
        m_old = m_ref[...]
        m_new2 = jnp.maximum(m_old, ln_)
        sc = jnp.exp(m_old - m_new2)
        pn = jnp.exp(ln_ - m_new2)
        den = l_ref[...] * sc + pn
        r = (acc_ref[...] * sc + pn.astype(BF16).astype(F32) * vn) / den
        pieces = [r[h:h + 1, (h // GROUP) * HEAD_DIM:(h // GROUP + 1) * HEAD_DIM] for h in range(N_HEADS)]
        o_ref[0] = jnp.concatenate(pieces, axis=-1)


def _dsa_step_attend(page_table, bias3, biasn3, qbd, kn3, vn3, cache_k_t, cache_v_t, layer):
    n, n_pages = page_table.shape
    steps = n_pages // KV_PAGES_PER_STEP
    step_keys = KV_PAGES_PER_STEP * PAGE_SIZE
    page_specs = _page_specs(layer, KV_DIM, KV_PAGES_PER_STEP)
    row = lambda i, s, pt: (i, 0, 0)
    grid_spec = pltpu.PrefetchScalarGridSpec(
        num_scalar_prefetch=1,
        grid=(n, steps),
        in_specs=[
            pl.BlockSpec((1, 1, step_keys), lambda i, s, pt: (i, 0, s)),
            pl.BlockSpec((1, 1, LANES), row),
            pl.BlockSpec((1, N_HEADS, KV_DIM), row),
            pl.BlockSpec((1, 1, KV_DIM), row),
            pl.BlockSpec((1, 1, KV_DIM), row),
        ] + page_specs + page_specs,
        out_specs=pl.BlockSpec((1, 1, Q_END), row),
        scratch_shapes=[
            pltpu.VMEM((KV_DIM, step_keys), BF16), pltpu.VMEM((KV_DIM, step_keys), BF16),
            pltpu.VMEM((N_HEADS, 1), F32), pltpu.VMEM((N_HEADS, 1), F32), pltpu.VMEM((N_HEADS, KV_DIM), F32),
        ],
    )
    return pl.pallas_call(
        _dsa_step_attend_kernel,
        grid_spec=grid_spec,
        out_shape=jax.ShapeDtypeStruct((n, 1, Q_END), F32),
        compiler_params=_params(2),
        name="dsa_step_attend",
    )(page_table, bias3, biasn3, qbd, kn3, vn3,
      *([cache_k_t] * KV_PAGES_PER_STEP), *([cache_v_t] * KV_PAGES_PER_STEP))


def _proj_ln_kernel(a_ref, x_ref, w_ref, g_ref, b_ref, o_ref):
    m = _dot3(a_ref[...], w_ref[...])
    o_ref[...] = _ln(ALPHA * x_ref[...] + m, g_ref[...], b_ref[...])


def _proj_ln(a, x, w, g, b, layer):
    n = x.shape[0]
    c = _const_spec
    in_specs = [c((n, a.shape[1])), c((n, D_MODEL)), _layer_spec(w.shape[1:], layer), c((1, D_MODEL)),
                c((1, D_MODEL))]
    return pl.pallas_call(
        _proj_ln_kernel,
        grid=(1,),
        in_specs=in_specs,
        out_specs=_const_spec((n, D_MODEL)),
        out_shape=jax.ShapeDtypeStruct((n, D_MODEL), F32),
        compiler_params=_params(1),
        name="proj_ln",
    )(a, x, w, g, b)


def _moe_kernel(x_ref, wr_ref, br_ref, w1_ref, w3_ref, w2_ref, g_ref, b_ref, o_ref, acc, comb, xb, *, precise):
    e = pl.program_id(1)
    tm = x_ref.shape[0]
    lane = lax.broadcasted_iota(jnp.int32, (tm, LANES), 1)

    @pl.when(e == 0)
    def _():
        x = x_ref[...]
        logits = _dot3(x, wr_ref[...])
        logits = jnp.where(lane < N_EXPERTS, logits + br_ref[...], -jnp.inf)
        lane_f = lane.astype(F32)
        m1 = jnp.max(logits, axis=-1, keepdims=True)
        i1 = jnp.min(jnp.where(logits == m1, lane_f, float(LANES)), axis=-1, keepdims=True)
        rest = jnp.where(lane_f == i1, -jnp.inf, logits)
        m2 = jnp.max(rest, axis=-1, keepdims=True)
        i2 = jnp.min(jnp.where(rest == m2, lane_f, float(LANES)), axis=-1, keepdims=True)
        e2 = jnp.exp(m2 - m1)
        den = 1.0 + e2
        comb[...] = jnp.where(lane_f == i1, 1.0 / den, 0.0) + jnp.where(lane_f == i2, e2 / den, 0.0)
        xb[...] = x.astype(xb.dtype)
        acc[...] = jnp.zeros(acc.shape, F32)

    c = jnp.sum(jnp.where(lane == e, comb[...], 0.0), axis=-1, keepdims=True)
    if precise:
        h = _silu(_dot3(xb[...], w1_ref[0])) * _dot3(xb[...], w3_ref[0])
        acc[...] += _dot3(c * h, w2_ref[0])
    else:
        h = _silu(_dot(xb[...], w1_ref[0])) * _dot(xb[...], w3_ref[0])
        acc[...] += _dot((c * h).astype(BF16), w2_ref[0])

    @pl.when(e == N_EXPERTS - 1)
    def _():
        o_ref[...] = _ln(ALPHA * x_ref[...] + acc[...], g_ref[...], b_ref[...])


def _moe(x, wr, br, w1, w3, w2, g, b, layer, tm, precise=False):
    m = x.shape[0]
    wspec = pl.BlockSpec((None, 1, D_MODEL, D_MODEL), lambda i, e: (layer, e, 0, 0))
    return pl.pallas_call(
        functools.partial(_moe_kernel, precise=precise),
        grid=(m // tm, N_EXPERTS),
        in_specs=[
            pl.BlockSpec((tm, D_MODEL), lambda i, e: (i, 0)),
            _const_spec((D_MODEL, LANES)),
            _const_spec((1, LANES)),
            wspec, wspec, wspec,
            _const_spec((1, D_MODEL)),
            _const_spec((1, D_MODEL)),
        ],
        out_specs=pl.BlockSpec((tm, D_MODEL), lambda i, e: (i, 0)),
        out_shape=jax.ShapeDtypeStruct((m, D_MODEL), F32),
        scratch_shapes=[pltpu.VMEM((tm, D_MODEL), F32), pltpu.VMEM((tm, LANES), F32),
                        pltpu.VMEM((tm, D_MODEL), F32 if precise else BF16)],
        compiler_params=_params(2),
        name="moe",
    )(x, wr, br, w1, w3, w2, g, b)


def kernel(x_prompt, x_sample, cache_k, cache_v, cache_kidx, state_conv, page_table, ln1_g, ln1_b, ln2_g, ln2_b,
           w_in_even, conv_w, sgu_ln_g, sgu_ln_b, sgu_w, sgu_b, w_out_even, ffn_w1, ffn_w3, ffn_w2, w_in_odd,
           w_out_odd, router_w, router_b, moe_w1, moe_w3, moe_w2):
    bp, tp, _ = x_prompt.shape
    ns = x_sample.shape[0]
    n_pool = cache_k.shape[1]
    past = page_table.shape[1] * PAGE_SIZE

    xp = x_prompt.reshape(bp * tp, D_MODEL)
    xs = x_sample.reshape(ns, D_MODEL)
    tab_p, tabt_p = _rope_tables(jnp.arange(tp))
    tab_s, tabt_s = _rope_tables(jnp.full((ns,), past, jnp.int32))
    ck_t = jnp.transpose(cache_k, (0, 1, 3, 4, 2)).reshape(-1, n_pool, KV_DIM, PAGE_SIZE)
    cv_t = jnp.transpose(cache_v, (0, 1, 3, 4, 2)).reshape(-1, n_pool, KV_DIM, PAGE_SIZE)
    cki_t = jnp.swapaxes(cache_kidx, 2, 3)
    row = lambda a: a.reshape(1, -1)
    head_kv = (jnp.arange(N_HEADS)[:, None] // GROUP == jnp.arange(N_KV_HEADS)[None, :]).astype(BF16)
    w_in_even_b, w_out_even_b, w_out_odd_b = w_in_even.astype(BF16), w_out_even.astype(BF16), w_out_odd.astype(BF16)
    ffn_w1_b, ffn_w3_b, ffn_w2_b = ffn_w1.astype(BF16), ffn_w3.astype(BF16), ffn_w2.astype(BF16)
    moe_w1_b, moe_w3_b, moe_w2_b = moe_w1.astype(BF16), moe_w3.astype(BF16), moe_w2.astype(BF16)

    k_p, v_p, ki_p, conv_p = [], [], [], []
    k_s, v_s, ki_s, conv_s, chunk_s = [], [], [], [], []
    for layer in range(DEPTH):
        i = layer // 2
        g1, b1, g2, b2 = row(ln1_g[layer]), row(ln1_b[layer]), row(ln2_g[layer]), row(ln2_b[layer])
        if layer % 2 == 0:
            sg, sbn = row(sgu_ln_g[i]), row(sgu_ln_b[i])
            bs_b = jnp.broadcast_to(sgu_b[i][:, :, None], (SGU_GROUPS, CHUNK, CHUNK))
            xp, cp = _even_mixer(xp, bp, tp, w_in_even_b, conv_w[i], sg, sbn, sgu_w[i], bs_b, w_out_even_b, g1, b1, i)
            wd = row(jnp.repeat(sgu_w[i][:, 0, 0], CHUNK))
            b0 = row(jnp.repeat(sgu_b[i][:, 0], CHUNK))
            xs, zs, vs = _even_mixer_step(xs, state_conv[i][:, 0], state_conv[i][:, 1], w_in_even, conv_w[i],
                                          sg, sbn, wd, b0, w_out_even, g1, b1, i)
            conv_p.append(cp)
            conv_s.append(jnp.stack([state_conv[i][:, 1], zs], axis=1))
            chunk_s.append(vs.reshape(ns, 1, SGU_CH))
            xp = _ffn(xp, ffn_w1_b, ffn_w3_b, ffn_w2_b, g2, b2, i, tm=1024)
            xs = _ffn_step(xs, ffn_w1, ffn_w3, ffn_w2, g2, b2, i)
        else:
            w = w_in_odd[i]
            wq32 = jnp.concatenate(
                [w[:, :Q_END], w[:, V_END:QI_END],
                 jnp.pad(w[:, QI_END + IDX_DIM:], ((0, 0), (0, LANES - IDX_HEADS)))], axis=1)
            wkt32 = jnp.concatenate([w[:, Q_END:V_END], w[:, QI_END:QI_END + IDX_DIM]], axis=1).T
            wq, wkt = wq32.astype(BF16), wkt32.astype(BF16)
            qp, qip, wip, ktp, vtp, kitp, ktb, vtb, kitb = _odd_proj(xp, wq, wkt, tab_p, tabt_p, bp, tp, tm=512)
            head_major = (bp, N_KV_HEADS, HEAD_DIM, tp)
            xp = _dsa_prompt(xp, qp, qip, wip, kitb, ktb.reshape(head_major), vtb.reshape(head_major), w_out_odd_b,
                             g1, b1, bp, tp, i)
            k_p.append(ktp)
            v_p.append(vtp)
            ki_p.append(kitp)

            qs, qis, wis, kts, vts, kits, _, _, _ = _odd_proj(xs, wq32, wkt32, tab_s, tabt_s, 1, ns, tm=ns,
                                                              precise=True)
            kn, vn, kis = kts[0].T, vts[0].T, kits[0].T
            sp, sn = _dsa_step_score(page_table, qis.reshape(ns, IDX_HEADS, IDX_DIM),
                                     wis[:, :IDX_HEADS].reshape(ns, IDX_HEADS, 1), kis.reshape(ns, 1, IDX_DIM),
                                     cki_t, i)
            qbd = (qs.astype(BF16).reshape(ns, N_HEADS, 1, HEAD_DIM)
                   * head_kv[None, :, :, None]).reshape(ns, N_HEADS, KV_DIM)
            bias_s, biasn_s = _dsa_step_select(sp.reshape(ns, past), sn.reshape(ns, LANES))
            a_s = _dsa_step_attend(page_table, bias_s.reshape(ns, 1, past), biasn_s.reshape(ns, 1, LANES), qbd,
                                   kn.reshape(ns, 1, KV_DIM), vn.reshape(ns, 1, KV_DIM), ck_t, cv_t, i)
            xs = _proj_ln(a_s.reshape(ns, Q_END), xs, w_out_odd, g1, b1, i)
            k_s.append(kn.reshape(ns, 1, N_KV_HEADS, HEAD_DIM))
            v_s.append(vn.reshape(ns, 1, N_KV_HEADS, HEAD_DIM))
            ki_s.append(kis.reshape(ns, 1, IDX_DIM))

            wr = jnp.pad(router_w[i], ((0, 0), (0, LANES - N_EXPERTS)))
            br = jnp.pad(row(router_b[i]), ((0, 0), (0, LANES - N_EXPERTS)))
            xp = _moe(xp, wr, br, moe_w1_b, moe_w3_b, moe_w2_b, g2, b2, i, tm=1024)
            xs = _moe(xs, wr, br, moe_w1, moe_w3, moe_w2, g2, b2, i, tm=ns, precise=True)

    n_odd = len(k_p)
    kv_shape = (n_odd, bp, N_KV_HEADS, HEAD_DIM, tp)
    k_prompt = jnp.stack(k_p).reshape(kv_shape).transpose(0, 1, 4, 2, 3)
    v_prompt = jnp.stack(v_p).reshape(kv_shape).transpose(0, 1, 4, 2, 3)
    kidx_prompt = jnp.stack(ki_p).transpose(0, 1, 3, 2)
    return (xp.reshape(bp, tp, D_MODEL), xs.reshape(ns, 1, D_MODEL),
            k_prompt, v_prompt, kidx_prompt, jnp.stack(conv_p),
            jnp.stack(k_s), jnp.stack(v_s), jnp.stack(ki_s), jnp.stack(conv_s), jnp.stack(chunk_s))
```

```python
import functools

import jax
import jax.numpy as jnp
import numpy as np
from jax import lax
from jax.experimental import pallas as pl
from jax.experimental.pallas import tpu as pltpu

D_MODEL = 1024
DEPTH = 4
PAGE_SIZE = 128
CONV_CH = 512
CONV_W = 3
SGU_CH = 512
SGU_GROUPS = 4
CHUNK = 128
N_HEADS = 16
HEAD_DIM = 64
N_KV_HEADS = 4
GROUP = N_HEADS // N_KV_HEADS
KV_DIM = N_KV_HEADS * HEAD_DIM
ROPE_DIM = 16
ROPE_THETA = 500000.0
IDX_HEADS = 8
IDX_DIM = 64
TOPK_MAX = 256
D_FF = 2816
N_EXPERTS = 8
ALPHA = (2 * DEPTH) ** 0.25
LN_EPS = 1e-5

EVEN_IN = 3 * CONV_CH + 2 * SGU_CH
Q_END = N_HEADS * HEAD_DIM
K_END = Q_END + KV_DIM
V_END = K_END + KV_DIM
QI_END = V_END + IDX_HEADS * IDX_DIM

LANES = 128
VMEM_LIMIT = 56 * 1024 * 1024
BF16 = jnp.bfloat16
F32 = jnp.float32


def _params(n_axes):
    return pltpu.CompilerParams(dimension_semantics=("arbitrary",) * n_axes, vmem_limit_bytes=VMEM_LIMIT)


def _const_spec(shape, buffers=None):
    nd = len(shape)
    mode = None if buffers is None else pl.Buffered(buffers)
    return pl.BlockSpec(shape, lambda *_: (0,) * nd, pipeline_mode=mode)


def _layer_spec(shape, layer, buffers=None):
    nd = len(shape)
    mode = None if buffers is None else pl.Buffered(buffers)
    return pl.BlockSpec((None,) + tuple(shape), lambda *_: (layer,) + (0,) * nd, pipeline_mode=mode)


def _ln(x, g, b):
    mu = jnp.mean(x, axis=-1, keepdims=True)
    xc = x - mu
    var = jnp.mean(xc * xc, axis=-1, keepdims=True)
    return xc * lax.rsqrt(var + LN_EPS) * g + b


def _dot(a, b):
    return jnp.dot(a, b, preferred_element_type=F32)


def _dot_nt(a, b):
    return lax.dot_general(a, b, (((1,), (1,)), ((), ())), preferred_element_type=F32)


def _split(a):
    hi = a.astype(BF16)
    return hi, (a - hi.astype(F32)).astype(BF16)


def _dot3(a, b, dot=_dot):
    ah, al = _split(a)
    bh, bl = _split(b)
    return dot(ah, bh) + (dot(ah, bl) + dot(al, bh))


def _silu(x):
    return x * (1.0 / (1.0 + jnp.exp(-x)))


BISECT_STEPS = 20


def _kth_largest(count_gt, min_above, row_min, row_max, k):
    lo0 = row_min - jnp.maximum(jnp.abs(row_min), 1.0)

    def halve(i, c):
        lo, hi = c
        mid = lo + (hi - lo) * 0.5
        below = count_gt(mid) < k
        return jnp.where(below, lo, mid), jnp.where(below, mid, hi)

    lo, _ = lax.fori_loop(0, BISECT_STEPS, halve, (lo0, row_max))
    tau = min_above(lo)
    above = count_gt(tau)

    def unsettled(c):
        return jnp.max((c[2] >= k).astype(F32)) > 0.0

    def advance(c):
        lo, tau, above = c
        lo = jnp.where(above >= k, tau, lo)
        tau = min_above(lo)
        return lo, tau, count_gt(tau)

    _, tau, above = lax.while_loop(unsettled, advance, (lo, tau, above))
    return tau, above


def _tie_limit(count_lt, need, rows, n_bits):
    def body(i, p):
        cand = p + jnp.left_shift(jnp.int32(1), jnp.int32(n_bits - 1) - i)
        return jnp.where(count_lt(cand) < need, cand, p)
    return lax.fori_loop(0, n_bits, body, jnp.zeros((rows, 1), jnp.int32))


def _even_mixer_kernel(x_ref, win_ref, cw_ref, sg_ref, sb_ref, wm_ref, bs_ref, wout_ref, g_ref, b_ref,
                       o_ref, cs_ref, zbuf, ybuf, *, tm):
    t = pl.program_id(1)
    x = x_ref[...]
    h = _dot(x.astype(BF16), win_ref[...])
    gate_b = h[:, 0:CONV_CH]
    z = h[:, CONV_CH:2 * CONV_CH] * h[:, 2 * CONV_CH:3 * CONV_CH]
    u = h[:, 3 * CONV_CH:3 * CONV_CH + SGU_CH]
    v = h[:, 3 * CONV_CH + SGU_CH:]

    @pl.when(t == 0)
    def _():
        zbuf[0:8, :] = jnp.zeros((8, CONV_CH), F32)

    zbuf[8:8 + tm, :] = z
    conv = cw_ref[0:1, :] * zbuf[6:6 + tm, :] + cw_ref[1:2, :] * zbuf[7:7 + tm, :] + cw_ref[2:3, :] * z
    ybuf[:, 0:CONV_CH] = (gate_b * conv).astype(BF16)
    cs_ref[0] = zbuf[tm + 6:tm + 8, :]
    zbuf[0:8, :] = zbuf[tm:tm + 8, :]

    vn = _ln(v, sg_ref[...], sb_ref[...])
    row = lax.broadcasted_iota(jnp.int32, (CHUNK, CHUNK), 0)
    col = lax.broadcasted_iota(jnp.int32, (CHUNK, CHUNK), 1)
    for g in range(SGU_GROUPS):
        wm = jnp.where(row >= col, wm_ref[g], 0.0).astype(BF16)
        for c in range(tm // CHUNK):
            rs = slice(c * CHUNK, (c + 1) * CHUNK)
            cs = slice(g * CHUNK, (g + 1) * CHUNK)
            s = _dot(wm, vn[rs, cs].astype(BF16)) + bs_ref[g]
            ybuf[rs, CONV_CH + g * CHUNK:CONV_CH + (g + 1) * CHUNK] = (u[rs, cs] * s).astype(BF16)

    m = _dot(ybuf[...], wout_ref[...])
    o_ref[...] = _ln(ALPHA * x + m, g_ref[...], b_ref[...])


def _even_mixer(x, bsz, t, w_in, conv_w, sgu_g, sgu_bn, sgu_w, bs_b, w_out, g, b, layer, tm=512):
    nt = t // tm
    kern = functools.partial(_even_mixer_kernel, tm=tm)
    return pl.pallas_call(
        kern,
        grid=(bsz, nt),
        in_specs=[
            pl.BlockSpec((tm, D_MODEL), lambda i, j: (i * nt + j, 0)),
            _layer_spec((D_MODEL, EVEN_IN), layer),
            _const_spec((CONV_W, CONV_CH)),
            _const_spec((1, SGU_CH)),
            _const_spec((1, SGU_CH)),
            _const_spec((SGU_GROUPS, CHUNK, CHUNK)),
            _const_spec((SGU_GROUPS, CHUNK, CHUNK)),
            _layer_spec((D_MODEL, D_MODEL), layer),
            _const_spec((1, D_MODEL)),
            _const_spec((1, D_MODEL)),
        ],
        out_specs=[
            pl.BlockSpec((tm, D_MODEL), lambda i, j: (i * nt + j, 0)),
            pl.BlockSpec((1, CONV_W - 1, CONV_CH), lambda i, j: (i, 0, 0)),
        ],
        out_shape=[
            jax.ShapeDtypeStruct((bsz * t, D_MODEL), F32),
            jax.ShapeDtypeStruct((bsz, CONV_W - 1, CONV_CH), F32),
        ],
        scratch_shapes=[pltpu.VMEM((tm + 8, CONV_CH), F32), pltpu.VMEM((tm, D_MODEL), BF16)],
        compiler_params=_params(2),
        name="even_mixer",
    )(x, w_in, conv_w, sgu_g, sgu_bn, sgu_w, bs_b, w_out, g, b)


def _even_mixer_step_kernel(x_ref, p0_ref, p1_ref, win_ref, cw_ref, sg_ref, sb_ref, wd_ref, b0_ref, wout_ref,
                            g_ref, b_ref, o_ref, z_ref, vn_ref):
    x = x_ref[...]
    gate_b, gate_c, h_in, u, v = [_dot3(x, win_ref[:, j * CONV_CH:(j + 1) * CONV_CH]) for j in range(5)]
    z = gate_c * h_in
    conv = cw_ref[0:1, :] * p0_ref[...] + cw_ref[1:2, :] * p1_ref[...] + cw_ref[2:3, :] * z
    vn = _ln(v, sg_ref[...], sb_ref[...])
    s = wd_ref[...] * vn + b0_ref[...]
    y = jnp.concatenate([gate_b * conv, u * s], axis=-1)
    m = _dot3(y, wout_ref[...])
    o_ref[...] = _ln(ALPHA * x + m, g_ref[...], b_ref[...])
    z_ref[...] = z
    vn_ref[...] = vn


def _even_mixer_step(x, p0, p1, w_in, conv_w, sgu_g, sgu_bn, wd, b0, w_out, g, b, layer):
    n = x.shape[0]
    c = _const_spec
    in_specs = [c((n, D_MODEL)), c((n, CONV_CH)), c((n, CONV_CH)), _layer_spec((D_MODEL, EVEN_IN), layer),
                c((CONV_W, CONV_CH)), c((1, SGU_CH)), c((1, SGU_CH)), c((1, SGU_CH)), c((1, SGU_CH)),
                _layer_spec((D_MODEL, D_MODEL), layer), c((1, D_MODEL)), c((1, D_MODEL))]
    return pl.pallas_call(
        _even_mixer_step_kernel,
        grid=(1,),
        in_specs=in_specs,
        out_specs=[_const_spec((n, D_MODEL)), _const_spec((n, CONV_CH)), _const_spec((n, SGU_CH))],
        out_shape=[jax.ShapeDtypeStruct((n, D_MODEL), F32), jax.ShapeDtypeStruct((n, CONV_CH), F32),
                   jax.ShapeDtypeStruct((n, SGU_CH), F32)],
        compiler_params=_params(1),
        name="even_mixer_step",
    )(x, p0, p1, w_in, conv_w, sgu_g, sgu_bn, wd, b0, w_out, g, b)


FF_TILE = 256


def _ffn_kernel(x_ref, w1_ref, w3_ref, w2_ref, g_ref, b_ref, o_ref, hbuf):
    x = x_ref[...]
    xb = x.astype(BF16)
    for f in range(D_FF // FF_TILE):
        cs = slice(f * FF_TILE, (f + 1) * FF_TILE)
        hbuf[:, cs] = (_silu(_dot(xb, w1_ref[:, cs])) * _dot(xb, w3_ref[:, cs])).astype(BF16)
    y = _dot(hbuf[...], w2_ref[...])
    o_ref[...] = _ln(ALPHA * x + y, g_ref[...], b_ref[...])


def _ffn(x, w1, w3, w2, g, b, layer, tm):
    m = x.shape[0]
    return pl.pallas_call(
        _ffn_kernel,
        grid=(m // tm,),
        in_specs=[
            pl.BlockSpec((tm, D_MODEL), lambda i: (i, 0)),
            _layer_spec((D_MODEL, D_FF), layer, buffers=1),
            _layer_spec((D_MODEL, D_FF), layer, buffers=1),
            _layer_spec((D_FF, D_MODEL), layer, buffers=1),
            _const_spec((1, D_MODEL)),
            _const_spec((1, D_MODEL)),
        ],
        out_specs=pl.BlockSpec((tm, D_MODEL), lambda i: (i, 0)),
        out_shape=jax.ShapeDtypeStruct((m, D_MODEL), F32),
        scratch_shapes=[pltpu.VMEM((tm, D_FF), BF16)],
        compiler_params=_params(1),
        name="ffn",
    )(x, w1, w3, w2, g, b)


def _ffn_step_kernel(x_ref, w1_ref, w3_ref, w2_ref, g_ref, b_ref, o_ref, acc):
    f = pl.program_id(0)

    @pl.when(f == 0)
    def _():
        acc[...] = jnp.zeros(acc.shape, F32)

    x = x_ref[...]
    h = _silu(_dot3(x, w1_ref[...])) * _dot3(x, w3_ref[...])
    acc[...] += _dot3(h, w2_ref[...])

    @pl.when(f == pl.num_programs(0) - 1)
    def _():
        o_ref[...] = _ln(ALPHA * x + acc[...], g_ref[...], b_ref[...])


def _ffn_step(x, w1, w3, w2, g, b, layer):
    n = x.shape[0]
    return pl.pallas_call(
        _ffn_step_kernel,
        grid=(D_FF // FF_TILE,),
        in_specs=[
            _const_spec((n, D_MODEL)),
            pl.BlockSpec((None, D_MODEL, FF_TILE), lambda f: (layer, 0, f)),
            pl.BlockSpec((None, D_MODEL, FF_TILE), lambda f: (layer, 0, f)),
            pl.BlockSpec((None, FF_TILE, D_MODEL), lambda f: (layer, f, 0)),
            _const_spec((1, D_MODEL)),
            _const_spec((1, D_MODEL)),
        ],
        out_specs=_const_spec((n, D_MODEL)),
        out_shape=jax.ShapeDtypeStruct((n, D_MODEL), F32),
        scratch_shapes=[pltpu.VMEM((n, D_MODEL), F32)],
        compiler_params=_params(1),
        name="ffn_step",
    )(x, w1, w3, w2, g, b)


QQI = Q_END + IDX_HEADS * IDX_DIM
KVT_ROWS = 2 * KV_DIM + IDX_DIM


def _rope(xc, c, sa, sb):
    return xc * c + pltpu.roll(xc, 8, 1) * sa + pltpu.roll(xc, LANES - 8, 1) * sb


def _rope_rows(ht, r0, cos, sin, out_ref, o0, n_rows):
    x1, x2 = ht[r0:r0 + 8, :], ht[r0 + 8:r0 + 16, :]
    out_ref[o0:o0 + 8, :] = x1 * cos - x2 * sin
    out_ref[o0 + 8:o0 + 16, :] = x1 * sin + x2 * cos
    out_ref[o0 + 16:o0 + n_rows, :] = ht[r0 + 16:r0 + n_rows, :]


def _odd_proj_kernel(x_ref, wq_ref, wkt_ref, tab_ref, tabt_ref,
                     q_ref, qi_ref, wi_ref, kt_ref, vt_ref, kit_ref, ktb_ref, vtb_ref, kitb_ref, *, precise):
    if precise:
        x = x_ref[...]
        h = _dot3(x, wq_ref[...])
        ht = _dot3(wkt_ref[...], x, dot=_dot_nt)
    else:
        xb = x_ref[...].astype(BF16)
        h = _dot(xb, wq_ref[...])
        ht = _dot_nt(wkt_ref[...], xb)
    c, sa, sb = tab_ref[:, 0:LANES], tab_ref[:, LANES:2 * LANES], tab_ref[:, 2 * LANES:3 * LANES]
    q_scale = HEAD_DIM ** -0.5 if precise else float(HEAD_DIM ** -0.5 * np.log2(np.e))
    for j in range(Q_END // LANES):
        cs = slice(j * LANES, (j + 1) * LANES)
        q_ref[:, cs] = (_rope(h[:, cs], c, sa, sb) * q_scale).astype(q_ref.dtype)
    for j in range(IDX_HEADS * IDX_DIM // LANES):
        cs = slice(j * LANES, (j + 1) * LANES)
        qi_ref[:, cs] = (_rope(h[:, Q_END + j * LANES:Q_END + (j + 1) * LANES], c, sa, sb)
                         * (IDX_DIM ** -0.5)).astype(qi_ref.dtype)
    wi_ref[...] = h[:, QQI:QQI + LANES] * (IDX_HEADS ** -0.5)

    cos, sin = tabt_ref[0:8, :], tabt_ref[8:16, :]
    for hd in range(N_KV_HEADS):
        _rope_rows(ht, hd * HEAD_DIM, cos, sin, kt_ref, hd * HEAD_DIM, HEAD_DIM)
    vt_ref[...] = ht[KV_DIM:2 * KV_DIM, :]
    _rope_rows(ht, 2 * KV_DIM, cos, sin, kit_ref, 0, IDX_DIM)
    ktb_ref[...] = kt_ref[...].astype(BF16)
    vtb_ref[...] = vt_ref[...].astype(BF16)
    kitb_ref[...] = kit_ref[...].astype(BF16)


def _odd_proj(x, wq, wkt, tab, tabt, bsz, t, tm, precise=False):
    nt = t // tm
    tok = lambda i: (i, 0)
    feat = lambda i: (i // nt, 0, i % nt)
    q_dtype = F32 if precise else BF16
    row_widths = [(Q_END, q_dtype), (IDX_HEADS * IDX_DIM, q_dtype), (LANES, F32)]
    feat_rows = [(KV_DIM, F32), (KV_DIM, F32), (IDX_DIM, F32), (KV_DIM, BF16), (KV_DIM, BF16), (IDX_DIM, BF16)]
    return pl.pallas_call(
        functools.partial(_odd_proj_kernel, precise=precise),
        grid=(bsz * nt,),
        in_specs=[
            pl.BlockSpec((tm, D_MODEL), tok),
            _const_spec((D_MODEL, QQI + LANES)),
            _const_spec((KVT_ROWS, D_MODEL)),
            pl.BlockSpec((tm, 3 * LANES), lambda i: (i % nt, 0)),
            pl.BlockSpec((ROPE_DIM, tm), lambda i: (0, i % nt)),
        ],
        out_specs=[pl.BlockSpec((tm, w_), tok) for w_, _ in row_widths]
        + [pl.BlockSpec((None, r_, tm), feat) for r_, _ in feat_rows],
        out_shape=[jax.ShapeDtypeStruct((bsz * t, w_), d_) for w_, d_ in row_widths]
        + [jax.ShapeDtypeStruct((bsz, r_, t), d_) for r_, d_ in feat_rows],
        compiler_params=_params(1),
        name="odd_proj",
    )(x, wq, wkt, tab, tabt)


def _rope_tables(pos):
    half = ROPE_DIM // 2
    inv = ROPE_THETA ** (-jnp.arange(half, dtype=F32) / half)
    ang = pos.astype(F32)[:, None] * inv[None, :]
    cos, sin = jnp.cos(ang), jnp.sin(ang)
    n = pos.shape[0]
    one = jnp.ones((n, HEAD_DIM - ROPE_DIM), F32)
    zero = jnp.zeros((n, HEAD_DIM - ROPE_DIM), F32)
    zh = jnp.zeros((n, half), F32)
    c64 = jnp.concatenate([cos, cos, one], axis=-1)
    sa64 = jnp.concatenate([zh, sin, zero], axis=-1)
    sb64 = jnp.concatenate([-sin, zh, zero], axis=-1)
    tab = jnp.concatenate([c64, c64, sa64, sa64, sb64, sb64], axis=-1)
    tabt = jnp.concatenate([cos.T, sin.T], axis=0)
    return tab, tabt


KEY_BLOCK = 512
COUNT_ROWS = 64


def _lane_fold(a, op):
    parts = [a[:, c * LANES:(c + 1) * LANES] for c in range(a.shape[1] // LANES)]
    while len(parts) > 1:
        parts = [op(parts[i], parts[i + 1]) for i in range(0, len(parts), 2)]
    return parts[0]


def _dsa_prompt_kernel(x_ref, q_ref, qi_ref, wi_ref, kit_ref, kt_ref, vt_ref, wout_ref, g_ref, b_ref,
                       o_ref, key_ref, bias_ref, lbuf, wib, q3, a3, mbuf, obuf, thr_ref, cnt_ref, ext_ref,
                       *, tq, topk):
    t = pl.program_id(1)
    n_blk = (t * tq) // KEY_BLOCK + 1

    def blk(kb):
        return pl.ds(pl.multiple_of(kb * KEY_BLOCK, KEY_BLOCK), KEY_BLOCK)

    def bcast(col):
        return jnp.broadcast_to(col, (tq, LANES))

    def chunks(a):
        return [a[:, c * LANES:(c + 1) * LANES] for c in range(KEY_BLOCK // LANES)]

    row = lax.broadcasted_iota(jnp.int32, (tq, KEY_BLOCK), 0)
    lane = lax.broadcasted_iota(jnp.int32, (tq, KEY_BLOCK), 1)

    for h in range(IDX_HEADS):
        wib[h] = bcast(wi_ref[:, h:h + 1])
    for h in range(N_HEADS):
        q3[h // GROUP, (h % GROUP) * tq:(h % GROUP + 1) * tq, :] = q_ref[:, h * HEAD_DIM:(h + 1) * HEAD_DIM]

    ext_ref[0] = jnp.full((tq, LANES), -jnp.inf, F32)
    ext_ref[1] = jnp.full((tq, LANES), jnp.inf, F32)

    def score_body(kb, carry):
        ki = kit_ref[:, blk(kb)]
        score = [jnp.zeros((tq, LANES), F32)] * (KEY_BLOCK // LANES)
        for h in range(IDX_HEADS):
            d = _dot(qi_ref[:, h * IDX_DIM:(h + 1) * IDX_DIM], ki)
            w = wib[h]
            score = [s_ + w * jnp.maximum(d_, 0.0) for s_, d_ in zip(score, chunks(d))]
        score = jnp.concatenate(score, axis=-1)
        adm = kb * KEY_BLOCK + lane <= t * tq + row
        key_ref[:, blk(kb)] = jnp.where(adm, score, -jnp.inf)
        ext_ref[0] = jnp.maximum(ext_ref[0], _lane_fold(jnp.where(adm, score, -jnp.inf), jnp.maximum))
        ext_ref[1] = jnp.minimum(ext_ref[1], _lane_fold(jnp.where(adm, score, jnp.inf), jnp.minimum))
        return carry

    lax.fori_loop(0, n_blk, score_body, 0)
    row_max = jnp.max(ext_ref[0], axis=-1, keepdims=True)
    row_min = jnp.min(ext_ref[1], axis=-1, keepdims=True)

    def sweep(thr, init, step, finish):
        thr_ref[...] = bcast(thr)
        cnt_ref[...] = jnp.full((tq, LANES), init, F32)

        def body(kb, c_):
            for r in range(tq // COUNT_ROWS):
                rs = slice(r * COUNT_ROWS, (r + 1) * COUNT_ROWS)
                th = thr_ref[rs, :]
                acc = cnt_ref[rs, :]
                for k_ in chunks(key_ref[rs, blk(kb)]):
                    acc = step(acc, k_, th)
                cnt_ref[rs, :] = acc
            return c_

        lax.fori_loop(0, n_blk, body, 0)
        return finish(cnt_ref[...], axis=-1, keepdims=True)

    def count(cmp, thr):
        return sweep(thr, 0.0, lambda acc, k_, th: acc + cmp(k_, th).astype(F32), jnp.sum)

    def min_above(thr):
        return sweep(thr, jnp.inf, lambda acc, k_, th: jnp.minimum(acc, jnp.where(k_ > th, k_, jnp.inf)), jnp.min)

    tau, above = _kth_largest(functools.partial(count, jnp.greater), min_above, row_min, row_max, float(topk))
    taub = bcast(tau)
    need = float(topk) - above
    n_eq = count(jnp.equal, tau)
    crowded = jnp.max((n_eq > need).astype(F32)) > 0.0

    @pl.when(jnp.logical_not(crowded))
    def _():
        def bias_body(kb, carry):
            out = [jnp.where((k_ >= taub) & (k_ > -jnp.inf), 0.0, -jnp.inf)
                   for k_ in chunks(key_ref[:, blk(kb)])]
            bias_ref[:, blk(kb)] = jnp.concatenate(out, axis=-1)
            return carry
        lax.fori_loop(0, n_blk, bias_body, 0)

    @pl.when(crowded)
    def _():
        r_ = lax.broadcasted_iota(jnp.int32, (KEY_BLOCK, KEY_BLOCK), 0)
        c_ = lax.broadcasted_iota(jnp.int32, (KEY_BLOCK, KEY_BLOCK), 1)
        before = jnp.where(r_ < c_, 1.0, 0.0).astype(BF16)
        needb = bcast(need)

        def bias_body(kb, seen):
            ks = chunks(key_ref[:, blk(kb)])
            tie = [jnp.where(k_ == taub, 1.0, 0.0) for k_ in ks]
            rank = chunks(_dot(jnp.concatenate(tie, axis=-1).astype(BF16), before))
            out = []
            for k_, t_, r in zip(ks, tie, rank):
                keep = ((k_ > taub) | ((t_ > 0.0) & (seen + r < needb))) & (k_ > -jnp.inf)
                out.append(jnp.where(keep, 0.0, -jnp.inf))
            bias_ref[:, blk(kb)] = jnp.concatenate(out, axis=-1)
            return seen + bcast(jnp.sum(functools.reduce(jnp.add, tie), axis=-1, keepdims=True))

        lax.fori_loop(0, n_blk, bias_body, jnp.zeros((tq, LANES), F32))

    def group_body(kv, carry):
        qg = q3[kv]
        mbuf[...] = jnp.full(mbuf.shape, -jnp.inf, F32)
        obuf[...] = jnp.zeros(obuf.shape, F32)

        def logit_body(kb, c_):
            logit = _dot(qg, kt_ref[kv, :, blk(kb)])
            bias = bias_ref[:, blk(kb)]
            for j in range(GROUP):
                rows = slice(j * tq, (j + 1) * tq)
                lj = logit[rows, :] + bias
                lbuf[rows, blk(kb)] = lj
                mbuf[rows, :] = jnp.maximum(mbuf[rows, :], _lane_fold(lj, jnp.maximum))
            return c_

        lax.fori_loop(0, n_blk, logit_body, 0)
        mbs = [bcast(jnp.max(mbuf[j * tq:(j + 1) * tq, :], axis=-1, keepdims=True)) for j in range(GROUP)]

        def pv_body(kb, c_):
            pj = []
            for j in range(GROUP):
                rows = slice(j * tq, (j + 1) * tq)
                ps = [jnp.exp2(l_ - mbs[j]) for l_ in chunks(lbuf[rows, blk(kb)])]
                pj.append(jnp.concatenate(ps, axis=-1).astype(BF16))
            v1 = jnp.concatenate([vt_ref[kv, :, blk(kb)], jnp.ones((LANES - HEAD_DIM, KEY_BLOCK), BF16)], axis=0)
            obuf[...] += _dot_nt(jnp.concatenate(pj, axis=0), v1)
            return c_

        lax.fori_loop(0, n_blk, pv_body, 0)
        a3[kv] = (obuf[:, 0:HEAD_DIM] / obuf[:, HEAD_DIM:HEAD_DIM + 1]).astype(BF16)
        return carry

    lax.fori_loop(0, N_KV_HEADS, group_body, 0)

    a = jnp.concatenate([a3[h // GROUP, (h % GROUP) * tq:(h % GROUP + 1) * tq, :] for h in range(N_HEADS)], axis=-1)
    m = _dot(a, wout_ref[...])
    o_ref[...] = _ln(ALPHA * x_ref[...] + m, g_ref[...], b_ref[...])


def _dsa_prompt(x, q, qi, wi, kit, kt, vt, w_out, g, b, bsz, t, layer, tq=256):
    nq = t // tq
    topk = min(TOPK_MAX, t // 4)
    kern = functools.partial(_dsa_prompt_kernel, tq=tq, topk=topk)
    qmap = lambda i, j: (i * nq + j, 0)
    return pl.pallas_call(
        kern,
        grid=(bsz, nq),
        in_specs=[
            pl.BlockSpec((tq, D_MODEL), qmap),
            pl.BlockSpec((tq, Q_END), qmap),
            pl.BlockSpec((tq, IDX_HEADS * IDX_DIM), qmap),
            pl.BlockSpec((tq, LANES), qmap),
            pl.BlockSpec((None, IDX_DIM, t), lambda i, j: (i, 0, 0)),
            pl.BlockSpec((None, N_KV_HEADS, HEAD_DIM, t), lambda i, j: (i, 0, 0, 0)),
            pl.BlockSpec((None, N_KV_HEADS, HEAD_DIM, t), lambda i, j: (i, 0, 0, 0)),
            _layer_spec((Q_END, D_MODEL), layer),
            _const_spec((1, D_MODEL)),
            _const_spec((1, D_MODEL)),
        ],
        out_specs=pl.BlockSpec((tq, D_MODEL), qmap),
        out_shape=jax.ShapeDtypeStruct((bsz * t, D_MODEL), F32),
        scratch_shapes=[
            pltpu.VMEM((tq, t), F32),
            pltpu.VMEM((tq, t), F32),
            pltpu.VMEM((GROUP * tq, t), F32),
            pltpu.VMEM((IDX_HEADS, tq, LANES), F32),
            pltpu.VMEM((N_KV_HEADS, GROUP * tq, HEAD_DIM), BF16),
            pltpu.VMEM((N_KV_HEADS, GROUP * tq, HEAD_DIM), BF16),
            pltpu.VMEM((GROUP * tq, LANES), F32),
            pltpu.VMEM((GROUP * tq, LANES), F32),
            pltpu.VMEM((tq, LANES), F32),
            pltpu.VMEM((tq, LANES), F32),
            pltpu.VMEM((2, tq, LANES), F32),
        ],
        compiler_params=_params(2),
        name="dsa_prompt",
    )(x, q, qi, wi, kit, kt, vt, w_out, g, b)


IDX_PAGES_PER_STEP = 64
KV_PAGES_PER_STEP = 16


def _page_specs(layer, n_rows, pages_per_step):
    def spec(j):
        return pl.BlockSpec((None, None, n_rows, PAGE_SIZE),
                            lambda i, s, pt: (layer, pt[i, s * pages_per_step + j], 0, 0))
    return [spec(j) for j in range(pages_per_step)]


def _dsa_step_score_kernel(pt_ref, qi_ref, wi_ref, kin_ref, *rest):
    pages = rest[:IDX_PAGES_PER_STEP]
    o_ref, on_ref, kbuf_hi, kbuf_lo = rest[IDX_PAGES_PER_STEP:]
    for j, p in enumerate(pages):
        hi, lo = _split(p[...])
        kbuf_hi[:, j * PAGE_SIZE:(j + 1) * PAGE_SIZE] = hi
        kbuf_lo[:, j * PAGE_SIZE:(j + 1) * PAGE_SIZE] = lo
    qi = qi_ref[0]
    wi = wi_ref[0]
    qh, ql = _split(qi)
    d = _dot(qh, kbuf_hi[...]) + (_dot(qh, kbuf_lo[...]) + _dot(ql, kbuf_hi[...]))
    o_ref[0] = jnp.sum(wi * jnp.maximum(d, 0.0), axis=0, keepdims=True)
    dn = jnp.sum(qi * kin_ref[0], axis=-1, keepdims=True)
    sn = jnp.sum(wi * jnp.maximum(dn, 0.0), axis=0, keepdims=True)
    on_ref[0] = jnp.broadcast_to(sn, (1, LANES))


def _dsa_step_score(page_table, qi3, wi3, kin3, cache_kidx_t, layer):
    n, n_pages = page_table.shape
    steps = n_pages // IDX_PAGES_PER_STEP
    step_keys = IDX_PAGES_PER_STEP * PAGE_SIZE
    row = lambda i, s, pt: (i, 0, 0)
    grid_spec = pltpu.PrefetchScalarGridSpec(
        num_scalar_prefetch=1,
        grid=(n, steps),
        in_specs=[
            pl.BlockSpec((1, IDX_HEADS, IDX_DIM), row),
            pl.BlockSpec((1, IDX_HEADS, 1), row),
            pl.BlockSpec((1, 1, IDX_DIM), row),
        ] + _page_specs(layer, IDX_DIM, IDX_PAGES_PER_STEP),
        out_specs=[
            pl.BlockSpec((1, 1, step_keys), lambda i, s, pt: (i, 0, s)),
            pl.BlockSpec((1, 1, LANES), row),
        ],
        scratch_shapes=[pltpu.VMEM((IDX_DIM, step_keys), BF16), pltpu.VMEM((IDX_DIM, step_keys), BF16)],
    )
    return pl.pallas_call(
        _dsa_step_score_kernel,
        grid_spec=grid_spec,
        out_shape=[jax.ShapeDtypeStruct((n, 1, n_pages * PAGE_SIZE), F32),
                   jax.ShapeDtypeStruct((n, 1, LANES), F32)],
        compiler_params=_params(2),
        name="dsa_step_score",
    )(page_table, qi3, wi3, kin3, *([cache_kidx_t] * IDX_PAGES_PER_STEP))


def _dsa_step_select_kernel(sp_ref, sn_ref, bias_ref, biasn_ref, *, topk):
    n, past = sp_ref.shape
    lane = lax.broadcasted_iota(jnp.int32, (n, LANES), 1)
    s_new = sn_ref[:, 0:1]

    def count_gt(x):
        return jnp.sum((sp_ref[...] > x).astype(F32), axis=-1, keepdims=True) + (s_new > x).astype(F32)

    def min_above(x):
        sp = sp_ref[...]
        past_min = jnp.min(jnp.where(sp > x, sp, jnp.inf), axis=-1, keepdims=True)
        return jnp.minimum(past_min, jnp.where(s_new > x, s_new, jnp.inf))

    row_max = jnp.maximum(jnp.max(sp_ref[...], axis=-1, keepdims=True), s_new)
    row_min = jnp.minimum(jnp.min(sp_ref[...], axis=-1, keepdims=True), s_new)
    tau, above = _kth_largest(count_gt, min_above, row_min, row_max, float(topk))
    need = float(topk) - above

    def count_lt(p):
        idx = lax.broadcasted_iota(jnp.int32, (n, past), 1)
        return jnp.sum(((sp_ref[...] == tau) & (idx < p)).astype(F32), axis=-1, keepdims=True)

    lim = _tie_limit(count_lt, need, n, int(past).bit_length())
    sp = sp_ref[...]
    idx = lax.broadcasted_iota(jnp.int32, (n, past), 1)
    keep_p = (sp > tau) | ((sp == tau) & (idx <= lim))
    n_keep = jnp.sum(keep_p.astype(F32), axis=-1, keepdims=True)
    keep_n = (s_new > tau) | ((s_new == tau) & (n_keep < float(topk)))
    bias_ref[...] = jnp.where(keep_p, 0.0, -jnp.inf)
    biasn_ref[...] = jnp.where(keep_n & (lane == 0), 0.0, -jnp.inf)


def _dsa_step_select(sp, sn):
    n, past = sp.shape
    topk = min(TOPK_MAX, (past + 1) // 4)
    return pl.pallas_call(
        functools.partial(_dsa_step_select_kernel, topk=topk),
        grid=(1,),
        in_specs=[_const_spec((n, past)), _const_spec((n, LANES))],
        out_specs=[_const_spec((n, past)), _const_spec((n, LANES))],
        out_shape=[jax.ShapeDtypeStruct((n, past), F32), jax.ShapeDtypeStruct((n, LANES), F32)],
        compiler_params=_params(1),
        name="dsa_step_select",
    )(sp, sn)


def _dsa_step_attend_kernel(pt_ref, bias_ref, biasn_ref, qbd_ref, kn_ref, vn_ref, *rest):
    kp = rest[:KV_PAGES_PER_STEP]
    vp = rest[KV_PAGES_PER_STEP:2 * KV_PAGES_PER_STEP]
    o_ref, kbuf, vbuf, m_ref, l_ref, acc_ref = rest[2 * KV_PAGES_PER_STEP:]
    s = pl.program_id(1)
    n_steps = pl.num_programs(1)
    step_keys = KV_PAGES_PER_STEP * PAGE_SIZE

    @pl.when(s == 0)
    def _():
        m_ref[...] = jnp.full(m_ref.shape, -jnp.inf, F32)
        l_ref[...] = jnp.zeros(l_ref.shape, F32)
        acc_ref[...] = jnp.zeros(acc_ref.shape, F32)

    for j in range(KV_PAGES_PER_STEP):
        kbuf[:, j * PAGE_SIZE:(j + 1) * PAGE_SIZE] = kp[j][...].astype(BF16)
        vbuf[:, j * PAGE_SIZE:(j + 1) * PAGE_SIZE] = vp[j][...].astype(BF16)
    qbd = qbd_ref[0]
    logit = _dot(qbd, kbuf[...]) + bias_ref[0]
    m_new = jnp.maximum(m_ref[...], jnp.max(logit, axis=-1, keepdims=True))
    m_safe = jnp.where(m_new == -jnp.inf, 0.0, m_new)
    scale = jnp.exp(m_ref[...] - m_safe)
    p = jnp.exp(logit - m_safe)
    l_ref[...] = l_ref[...] * scale + jnp.sum(p, axis=-1, keepdims=True)
    acc_ref[...] = acc_ref[...] * scale + _dot_nt(p.astype(BF16), vbuf[...])
    m_ref[...] = m_new

    @pl.when(s == n_steps - 1)
    def _():
        kn = kn_ref[0].astype(BF16).astype(F32)
        vn = vn_ref[0].astype(BF16).astype(F32)
        ln_ = jnp.sum(qbd.astype(F32) * kn, axis=-1, keepdims=True) + biasn_ref[0][:, 0:1]
        m_old = m_ref[...]
        m_new2 = jnp.maximum(m_old, ln_)
        sc = jnp.exp(m_old - m_new2)
        pn = jnp.exp(ln_ - m_new2)
        den = l_ref[...] * sc + pn
        r = (acc_ref[...] * sc + pn.astype(BF16).astype(F32) * vn) / den
        pieces = [r[h:h + 1, (h // GROUP) * HEAD_DIM:(h // GROUP + 1) * HEAD_DIM] for h in range(N_HEADS)]
        o_ref[0] = jnp.concatenate(pieces, axis=-1)


def _dsa_step_attend(page_table, bias3, biasn3, qbd, kn3, vn3, cache_k_t, cache_v_t, layer):
    n, n_pages = page_table.shape
    steps = n_pages // KV_PAGES_PER_STEP
    step_keys = KV_PAGES_PER_STEP * PAGE_SIZE
    page_specs = _page_specs(layer, KV_DIM, KV_PAGES_PER_STEP)
    row = lambda i, s, pt: (i, 0, 0)
    grid_spec = pltpu.PrefetchScalarGridSpec(
        num_scalar_prefetch=1,
        grid=(n, steps),
        in_specs=[
            pl.BlockSpec((1, 1, step_keys), lambda i, s, pt: (i, 0, s)),
            pl.BlockSpec((1, 1, LANES), row),
            pl.BlockSpec((1, N_HEADS, KV_DIM), row),
            pl.BlockSpec((1, 1, KV_DIM), row),
            pl.BlockSpec((1, 1, KV_DIM), row),
        ] + page_specs + page_specs,
        out_specs=pl.BlockSpec((1, 1, Q_END), row),
        scratch_shapes=[
            pltpu.VMEM((KV_DIM, step_keys), BF16), pltpu.VMEM((KV_DIM, step_keys), BF16),
            pltpu.VMEM((N_HEADS, 1), F32), pltpu.VMEM((N_HEADS, 1), F32), pltpu.VMEM((N_HEADS, KV_DIM), F32),
        ],
    )
    return pl.pallas_call(
        _dsa_step_attend_kernel,
        grid_spec=grid_spec,
        out_shape=jax.ShapeDtypeStruct((n, 1, Q_END), F32),
        compiler_params=_params(2),
        name="dsa_step_attend",
    )(page_table, bias3, biasn3, qbd, kn3, vn3,
      *([cache_k_t] * KV_PAGES_PER_STEP), *([cache_v_t] * KV_PAGES_PER_STEP))


def _proj_ln_kernel(a_ref, x_ref, w_ref, g_ref, b_ref, o_ref):
    m = _dot3(a_ref[...], w_ref[...])
    o_ref[...] = _ln(ALPHA * x_ref[...] + m, g_ref[...], b_ref[...])


def _proj_ln(a, x, w, g, b, layer):
    n = x.shape[0]
    c = _const_spec
    in_specs = [c((n, a.shape[1])), c((n, D_MODEL)), _layer_spec(w.shape[1:], layer), c((1, D_MODEL)),
                c((1, D_MODEL))]
    return pl.pallas_call(
        _proj_ln_kernel,
        grid=(1,),
        in_specs=in_specs,
        out_specs=_const_spec((n, D_MODEL)),
        out_shape=jax.ShapeDtypeStruct((n, D_MODEL), F32),
        compiler_params=_params(1),
        name="proj_ln",
    )(a, x, w, g, b)


def _moe_kernel(x_ref, wr_ref, br_ref, w1_ref, w3_ref, w2_ref, g_ref, b_ref, o_ref, acc, comb, xb, *, precise):
    e = pl.program_id(1)
    tm = x_ref.shape[0]
    lane = lax.broadcasted_iota(jnp.int32, (tm, LANES), 1)

    @pl.when(e == 0)
    def _():
        x = x_ref[...]
        logits = _dot3(x, wr_ref[...])
        logits = jnp.where(lane < N_EXPERTS, logits + br_ref[...], -jnp.inf)
        lane_f = lane.astype(F32)
        m1 = jnp.max(logits, axis=-1, keepdims=True)
        i1 = jnp.min(jnp.where(logits == m1, lane_f, float(LANES)), axis=-1, keepdims=True)
        rest = jnp.where(lane_f == i1, -jnp.inf, logits)
        m2 = jnp.max(rest, axis=-1, keepdims=True)
        i2 = jnp.min(jnp.where(rest == m2, lane_f, float(LANES)), axis=-1, keepdims=True)
        e2 = jnp.exp(m2 - m1)
        den = 1.0 + e2
        comb[...] = jnp.where(lane_f == i1, 1.0 / den, 0.0) + jnp.where(lane_f == i2, e2 / den, 0.0)
        xb[...] = x.astype(xb.dtype)
        acc[...] = jnp.zeros(acc.shape, F32)

    c = jnp.sum(jnp.where(lane == e, comb[...], 0.0), axis=-1, keepdims=True)
    if precise:
        h = _silu(_dot3(xb[...], w1_ref[0])) * _dot3(xb[...], w3_ref[0])
        acc[...] += _dot3(c * h, w2_ref[0])
    else:
        h = _silu(_dot(xb[...], w1_ref[0])) * _dot(xb[...], w3_ref[0])
        acc[...] += _dot((c * h).astype(BF16), w2_ref[0])

    @pl.when(e == N_EXPERTS - 1)
    def _():
        o_ref[...] = _ln(ALPHA * x_ref[...] + acc[...], g_ref[...], b_ref[...])


def _moe(x, wr, br, w1, w3, w2, g, b, layer, tm, precise=False):
    m = x.shape[0]
    wspec = pl.BlockSpec((None, 1, D_MODEL, D_MODEL), lambda i, e: (layer, e, 0, 0))
    return pl.pallas_call(
        functools.partial(_moe_kernel, precise=precise),
        grid=(m // tm, N_EXPERTS),
        in_specs=[
            pl.BlockSpec((tm, D_MODEL), lambda i, e: (i, 0)),
            _const_spec((D_MODEL, LANES)),
            _const_spec((1, LANES)),
            wspec, wspec, wspec,
            _const_spec((1, D_MODEL)),
            _const_spec((1, D_MODEL)),
        ],
        out_specs=pl.BlockSpec((tm, D_MODEL), lambda i, e: (i, 0)),
        out_shape=jax.ShapeDtypeStruct((m, D_MODEL), F32),
        scratch_shapes=[pltpu.VMEM((tm, D_MODEL), F32), pltpu.VMEM((tm, LANES), F32),
                        pltpu.VMEM((tm, D_MODEL), F32 if precise else BF16)],
        compiler_params=_params(2),
        name="moe",
    )(x, wr, br, w1, w3, w2, g, b)


def kernel(x_prompt, x_sample, cache_k, cache_v, cache_kidx, state_conv, page_table, ln1_g, ln1_b, ln2_g, ln2_b,
           w_in_even, conv_w, sgu_ln_g, sgu_ln_b, sgu_w, sgu_b, w_out_even, ffn_w1, ffn_w3, ffn_w2, w_in_odd,
           w_out_odd, router_w, router_b, moe_w1, moe_w3, moe_w2):
    bp, tp, _ = x_prompt.shape
    ns = x_sample.shape[0]
    n_pool = cache_k.shape[1]
    past = page_table.shape[1] * PAGE_SIZE

    xp = x_prompt.reshape(bp * tp, D_MODEL)
    xs = x_sample.reshape(ns, D_MODEL)
    tab_p, tabt_p = _rope_tables(jnp.arange(tp))
    tab_s, tabt_s = _rope_tables(jnp.full((ns,), past, jnp.int32))
    ck_t = jnp.transpose(cache_k, (0, 1, 3, 4, 2)).reshape(-1, n_pool, KV_DIM, PAGE_SIZE)
    cv_t = jnp.transpose(cache_v, (0, 1, 3, 4, 2)).reshape(-1, n_pool, KV_DIM, PAGE_SIZE)
    cki_t = jnp.swapaxes(cache_kidx, 2, 3)
    row = lambda a: a.reshape(1, -1)
    head_kv = (jnp.arange(N_HEADS)[:, None] // GROUP == jnp.arange(N_KV_HEADS)[None, :]).astype(BF16)
    w_in_even_b, w_out_even_b, w_out_odd_b = w_in_even.astype(BF16), w_out_even.astype(BF16), w_out_odd.astype(BF16)
    ffn_w1_b, ffn_w3_b, ffn_w2_b = ffn_w1.astype(BF16), ffn_w3.astype(BF16), ffn_w2.astype(BF16)
    moe_w1_b, moe_w3_b, moe_w2_b = moe_w1.astype(BF16), moe_w3.astype(BF16), moe_w2.astype(BF16)

    k_p, v_p, ki_p, conv_p = [], [], [], []
    k_s, v_s, ki_s, conv_s, chunk_s = [], [], [], [], []
    for layer in range(DEPTH):
        i = layer // 2
        g1, b1, g2, b2 = row(ln1_g[layer]), row(ln1_b[layer]), row(ln2_g[layer]), row(ln2_b[layer])
        if layer % 2 == 0:
            sg, sbn = row(sgu_ln_g[i]), row(sgu_ln_b[i])
            bs_b = jnp.broadcast_to(sgu_b[i][:, :, None], (SGU_GROUPS, CHUNK, CHUNK))
            xp, cp = _even_mixer(xp, bp, tp, w_in_even_b, conv_w[i], sg, sbn, sgu_w[i], bs_b, w_out_even_b, g1, b1, i)
            wd = row(jnp.repeat(sgu_w[i][:, 0, 0], CHUNK))
            b0 = row(jnp.repeat(sgu_b[i][:, 0], CHUNK))
            xs, zs, vs = _even_mixer_step(xs, state_conv[i][:, 0], state_conv[i][:, 1], w_in_even, conv_w[i],
                                          sg, sbn, wd, b0, w_out_even, g1, b1, i)
            conv_p.append(cp)
            conv_s.append(jnp.stack([state_conv[i][:, 1], zs], axis=1))
            chunk_s.append(vs.reshape(ns, 1, SGU_CH))
            xp = _ffn(xp, ffn_w1_b, ffn_w3_b, ffn_w2_b, g2, b2, i, tm=1024)
            xs = _ffn_step(xs, ffn_w1, ffn_w3, ffn_w2, g2, b2, i)
        else:
            w = w_in_odd[i]
            wq32 = jnp.concatenate(
                [w[:, :Q_END], w[:, V_END:QI_END],
                 jnp.pad(w[:, QI_END + IDX_DIM:], ((0, 0), (0, LANES - IDX_HEADS)))], axis=1)
            wkt32 = jnp.concatenate([w[:, Q_END:V_END], w[:, QI_END:QI_END + IDX_DIM]], axis=1).T
            wq, wkt = wq32.astype(BF16), wkt32.astype(BF16)
            qp, qip, wip, ktp, vtp, kitp, ktb, vtb, kitb = _odd_proj(xp, wq, wkt, tab_p, tabt_p, bp, tp, tm=512)
            head_major = (bp, N_KV_HEADS, HEAD_DIM, tp)
            xp = _dsa_prompt(xp, qp, qip, wip, kitb, ktb.reshape(head_major), vtb.reshape(head_major), w_out_odd_b,
                             g1, b1, bp, tp, i)
            k_p.append(ktp)
            v_p.append(vtp)
            ki_p.append(kitp)

            qs, qis, wis, kts, vts, kits, _, _, _ = _odd_proj(xs, wq32, wkt32, tab_s, tabt_s, 1, ns, tm=ns,
                                                              precise=True)
            kn, vn, kis = kts[0].T, vts[0].T, kits[0].T
            sp, sn = _dsa_step_score(page_table, qis.reshape(ns, IDX_HEADS, IDX_DIM),
                                     wis[:, :IDX_HEADS].reshape(ns, IDX_HEADS, 1), kis.reshape(ns, 1, IDX_DIM),
                                     cki_t, i)
            qbd = (qs.astype(BF16).reshape(ns, N_HEADS, 1, HEAD_DIM)
                   * head_kv[None, :, :, None]).reshape(ns, N_HEADS, KV_DIM)
            bias_s, biasn_s = _dsa_step_select(sp.reshape(ns, past), sn.reshape(ns, LANES))
            a_s = _dsa_step_attend(page_table, bias_s.reshape(ns, 1, past), biasn_s.reshape(ns, 1, LANES), qbd,
                                   kn.reshape(ns, 1, KV_DIM), vn.reshape(ns, 1, KV_DIM), ck_t, cv_t, i)
            xs = _proj_ln(a_s.reshape(ns, Q_END), xs, w_out_odd, g1, b1, i)
            k_s.append(kn.reshape(ns, 1, N_KV_HEADS, HEAD_DIM))
            v_s.append(vn.reshape(ns, 1, N_KV_HEADS, HEAD_DIM))
            ki_s.append(kis.reshape(ns, 1, IDX_DIM))

            wr = jnp.pad(router_w[i], ((0, 0), (0, LANES - N_EXPERTS)))
            br = jnp.pad(row(router_b[i]), ((0, 0), (0, LANES - N_EXPERTS)))
            xp = _moe(xp, wr, br, moe_w1_b, moe_w3_b, moe_w2_b, g2, b2, i, tm=1024)
            xs = _moe(xs, wr, br, moe_w1, moe_w3, moe_w2, g2, b2, i, tm=ns, precise=True)

    n_odd = len(k_p)
    kv_shape = (n_odd, bp, N_KV_HEADS, HEAD_DIM, tp)
    k_prompt = jnp.stack(k_p).reshape(kv_shape).transpose(0, 1, 4, 2, 3)
    v_prompt = jnp.stack(v_p).reshape(kv_shape).transpose(0, 1, 4, 2, 3)
    kidx_prompt = jnp.stack(ki_p).transpose(0, 1, 3, 2)
    return (xp.reshape(bp, tp, D_MODEL), xs.reshape(ns, 1, D_MODEL),
            k_prompt, v_prompt, kidx_prompt, jnp.stack(conv_p),
            jnp.stack(k_s), jnp.stack(v_s), jnp.stack(ki_s), jnp.stack(conv_s), jnp.stack(chunk_s))
```

```python
import functools

import jax
import jax.numpy as jnp
import numpy as np
from jax import lax
from jax.experimental import pallas as pl
from jax.experimental.pallas import tpu as pltpu

D_MODEL = 1024
DEPTH = 4
PAGE_SIZE = 128
CONV_CH = 512
CONV_W = 3
SGU_CH = 512
SGU_GROUPS = 4
CHUNK = 128
N_HEADS = 16
HEAD_DIM = 64
N_KV_HEADS = 4
GROUP = N_HEADS // N_KV_HEADS
KV_DIM = N_KV_HEADS * HEAD_DIM
ROPE_DIM = 16
ROPE_THETA = 500000.0
IDX_HEADS = 8
IDX_DIM = 64
TOPK_MAX = 256
D_FF = 2816
N_EXPERTS = 8
ALPHA = (2 * DEPTH) ** 0.25
LN_EPS = 1e-5

EVEN_IN = 3 * CONV_CH + 2 * SGU_CH
Q_END = N_HEADS * HEAD_DIM
K_END = Q_END + KV_DIM
V_END = K_END + KV_DIM
QI_END = V_END + IDX_HEADS * IDX_DIM

LANES = 128
VMEM_LIMIT = 56 * 1024 * 1024
BF16 = jnp.bfloat16
F32 = jnp.float32


def _params(n_axes):
    return pltpu.CompilerParams(dimension_semantics=("arbitrary",) * n_axes, vmem_limit_bytes=VMEM_LIMIT)


def _const_spec(shape, buffers=None):
    nd = len(shape)
    mode = None if buffers is None else pl.Buffered(buffers)
    return pl.BlockSpec(shape, lambda *_: (0,) * nd, pipeline_mode=mode)


def _layer_spec(shape, layer, buffers=None):
    nd = len(shape)
    mode = None if buffers is None else pl.Buffered(buffers)
    return pl.BlockSpec((None,) + tuple(shape), lambda *_: (layer,) + (0,) * nd, pipeline_mode=mode)


def _ln(x, g, b):
    mu = jnp.mean(x, axis=-1, keepdims=True)
    xc = x - mu
    var = jnp.mean(xc * xc, axis=-1, keepdims=True)
    return xc * lax.rsqrt(var + LN_EPS) * g + b


def _dot(a, b):
    return jnp.dot(a, b, preferred_element_type=F32)


def _dot_nt(a, b):
    return lax.dot_general(a, b, (((1,), (1,)), ((), ())), preferred_element_type=F32)


def _split(a):
    hi = a.astype(BF16)
    return hi, (a - hi.astype(F32)).astype(BF16)


def _dot3(a, b, dot=_dot):
    ah, al = _split(a)
    bh, bl = _split(b)
    return dot(ah, bh) + (dot(ah, bl) + dot(al, bh))


def _silu(x):
    return x * (1.0 / (1.0 + jnp.exp(-x)))


BISECT_STEPS = 20


def _kth_largest(count_gt, min_above, row_min, row_max, k):
    lo0 = row_min - jnp.maximum(jnp.abs(row_min), 1.0)

    def halve(i, c):
        lo, hi = c
        mid = lo + (hi - lo) * 0.5
        below = count_gt(mid) < k
        return jnp.where(below, lo, mid), jnp.where(below, mid, hi)

    lo, _ = lax.fori_loop(0, BISECT_STEPS, halve, (lo0, row_max))
    tau = min_above(lo)
    above = count_gt(tau)

    def unsettled(c):
        return jnp.max((c[2] >= k).astype(F32)) > 0.0

    def advance(c):
        lo, tau, above = c
        lo = jnp.where(above >= k, tau, lo)
        tau = min_above(lo)
        return lo, tau, count_gt(tau)

    _, tau, above = lax.while_loop(unsettled, advance, (lo, tau, above))
    return tau, above


def _tie_limit(count_lt, need, rows, n_bits):
    def body(i, p):
        cand = p + jnp.left_shift(jnp.int32(1), jnp.int32(n_bits - 1) - i)
        return jnp.where(count_lt(cand) < need, cand, p)
    return lax.fori_loop(0, n_bits, body, jnp.zeros((rows, 1), jnp.int32))


def _even_mixer_kernel(x_ref, win_ref, cw_ref, sg_ref, sb_ref, wm_ref, bs_ref, wout_ref, g_ref, b_ref,
                       o_ref, cs_ref, zbuf, ybuf, *, tm):
    t = pl.program_id(1)
    x = x_ref[...]
    h = _dot(x.astype(BF16), win_ref[...])
    gate_b = h[:, 0:CONV_CH]
    z = h[:, CONV_CH:2 * CONV_CH] * h[:, 2 * CONV_CH:3 * CONV_CH]
    u = h[:, 3 * CONV_CH:3 * CONV_CH + SGU_CH]
    v = h[:, 3 * CONV_CH + SGU_CH:]

    @pl.when(t == 0)
    def _():
        zbuf[0:8, :] = jnp.zeros((8, CONV_CH), F32)

    zbuf[8:8 + tm, :] = z
    conv = cw_ref[0:1, :] * zbuf[6:6 + tm, :] + cw_ref[1:2, :] * zbuf[7:7 + tm, :] + cw_ref[2:3, :] * z
    ybuf[:, 0:CONV_CH] = (gate_b * conv).astype(BF16)
    cs_ref[0] = zbuf[tm + 6:tm + 8, :]
    zbuf[0:8, :] = zbuf[tm:tm + 8, :]

    vn = _ln(v, sg_ref[...], sb_ref[...])
    row = lax.broadcasted_iota(jnp.int32, (CHUNK, CHUNK), 0)
    col = lax.broadcasted_iota(jnp.int32, (CHUNK, CHUNK), 1)
    for g in range(SGU_GROUPS):
        wm = jnp.where(row >= col, wm_ref[g], 0.0).astype(BF16)
        for c in range(tm // CHUNK):
            rs = slice(c * CHUNK, (c + 1) * CHUNK)
            cs = slice(g * CHUNK, (g + 1) * CHUNK)
            s = _dot(wm, vn[rs, cs].astype(BF16)) + bs_ref[g]
            ybuf[rs, CONV_CH + g * CHUNK:CONV_CH + (g + 1) * CHUNK] = (u[rs, cs] * s).astype(BF16)

    m = _dot(ybuf[...], wout_ref[...])
    o_ref[...] = _ln(ALPHA * x + m, g_ref[...], b_ref[...])


def _even_mixer(x, bsz, t, w_in, conv_w, sgu_g, sgu_bn, sgu_w, bs_b, w_out, g, b, layer, tm=512):
    nt = t // tm
    kern = functools.partial(_even_mixer_kernel, tm=tm)
    return pl.pallas_call(
        kern,
        grid=(bsz, nt),
        in_specs=[
            pl.BlockSpec((tm, D_MODEL), lambda i, j: (i * nt + j, 0)),
            _layer_spec((D_MODEL, EVEN_IN), layer),
            _const_spec((CONV_W, CONV_CH)),
            _const_spec((1, SGU_CH)),
            _const_spec((1, SGU_CH)),
            _const_spec((SGU_GROUPS, CHUNK, CHUNK)),
            _const_spec((SGU_GROUPS, CHUNK, CHUNK)),
            _layer_spec((D_MODEL, D_MODEL), layer),
            _const_spec((1, D_MODEL)),
            _const_spec((1, D_MODEL)),
        ],
        out_specs=[
            pl.BlockSpec((tm, D_MODEL), lambda i, j: (i * nt + j, 0)),
            pl.BlockSpec((1, CONV_W - 1, CONV_CH), lambda i, j: (i, 0, 0)),
        ],
        out_shape=[
            jax.ShapeDtypeStruct((bsz * t, D_MODEL), F32),
            jax.ShapeDtypeStruct((bsz, CONV_W - 1, CONV_CH), F32),
        ],
        scratch_shapes=[pltpu.VMEM((tm + 8, CONV_CH), F32), pltpu.VMEM((tm, D_MODEL), BF16)],
        compiler_params=_params(2),
        name="even_mixer",
    )(x, w_in, conv_w, sgu_g, sgu_bn, sgu_w, bs_b, w_out, g, b)


def _even_mixer_step_kernel(x_ref, p0_ref, p1_ref, win_ref, cw_ref, sg_ref, sb_ref, wd_ref, b0_ref, wout_ref,
                            g_ref, b_ref, o_ref, z_ref, vn_ref):
    x = x_ref[...]
    gate_b, gate_c, h_in, u, v = [_dot3(x, win_ref[:, j * CONV_CH:(j + 1) * CONV_CH]) for j in range(5)]
    z = gate_c * h_in
    conv = cw_ref[0:1, :] * p0_ref[...] + cw_ref[1:2, :] * p1_ref[...] + cw_ref[2:3, :] * z
    vn = _ln(v, sg_ref[...], sb_ref[...])
    s = wd_ref[...] * vn + b0_ref[...]
    y = jnp.concatenate([gate_b * conv, u * s], axis=-1)
    m = _dot3(y, wout_ref[...])
    o_ref[...] = _ln(ALPHA * x + m, g_ref[...], b_ref[...])
    z_ref[...] = z
    vn_ref[...] = vn


def _even_mixer_step(x, p0, p1, w_in, conv_w, sgu_g, sgu_bn, wd, b0, w_out, g, b, layer):
    n = x.shape[0]
    c = _const_spec
    in_specs = [c((n, D_MODEL)), c((n, CONV_CH)), c((n, CONV_CH)), _layer_spec((D_MODEL, EVEN_IN), layer),
                c((CONV_W, CONV_CH)), c((1, SGU_CH)), c((1, SGU_CH)), c((1, SGU_CH)), c((1, SGU_CH)),
                _layer_spec((D_MODEL, D_MODEL), layer), c((1, D_MODEL)), c((1, D_MODEL))]
    return pl.pallas_call(
        _even_mixer_step_kernel,
        grid=(1,),
        in_specs=in_specs,
        out_specs=[_const_spec((n, D_MODEL)), _const_spec((n, CONV_CH)), _const_spec((n, SGU_CH))],
        out_shape=[jax.ShapeDtypeStruct((n, D_MODEL), F32), jax.ShapeDtypeStruct((n, CONV_CH), F32),
                   jax.ShapeDtypeStruct((n, SGU_CH), F32)],
        compiler_params=_params(1),
        name="even_mixer_step",
    )(x, p0, p1, w_in, conv_w, sgu_g, sgu_bn, wd, b0, w_out, g, b)


FF_TILE = 256


def _ffn_kernel(x_ref, w1_ref, w3_ref, w2_ref, g_ref, b_ref, o_ref, hbuf):
    x = x_ref[...]
    xb = x.astype(BF16)
    for f in range(D_FF // FF_TILE):
        cs = slice(f * FF_TILE, (f + 1) * FF_TILE)
        hbuf[:, cs] = (_silu(_dot(xb, w1_ref[:, cs])) * _dot(xb, w3_ref[:, cs])).astype(BF16)
    y = _dot(hbuf[...], w2_ref[...])
    o_ref[...] = _ln(ALPHA * x + y, g_ref[...], b_ref[...])


def _ffn(x, w1, w3, w2, g, b, layer, tm):
    m = x.shape[0]
    return pl.pallas_call(
        _ffn_kernel,
        grid=(m // tm,),
        in_specs=[
            pl.BlockSpec((tm, D_MODEL), lambda i: (i, 0)),
            _layer_spec((D_MODEL, D_FF), layer, buffers=1),
            _layer_spec((D_MODEL, D_FF), layer, buffers=1),
            _layer_spec((D_FF, D_MODEL), layer, buffers=1),
            _const_spec((1, D_MODEL)),
            _const_spec((1, D_MODEL)),
        ],
        out_specs=pl.BlockSpec((tm, D_MODEL), lambda i: (i, 0)),
        out_shape=jax.ShapeDtypeStruct((m, D_MODEL), F32),
        scratch_shapes=[pltpu.VMEM((tm, D_FF), BF16)],
        compiler_params=_params(1),
        name="ffn",
    )(x, w1, w3, w2, g, b)


def _ffn_step_kernel(x_ref, w1_ref, w3_ref, w2_ref, g_ref, b_ref, o_ref, acc):
    f = pl.program_id(0)

    @pl.when(f == 0)
    def _():
        acc[...] = jnp.zeros(acc.shape, F32)

    x = x_ref[...]
    h = _silu(_dot3(x, w1_ref[...])) * _dot3(x, w3_ref[...])
    acc[...] += _dot3(h, w2_ref[...])

    @pl.when(f == pl.num_programs(0) - 1)
    def _():
        o_ref[...] = _ln(ALPHA * x + acc[...], g_ref[...], b_ref[...])


def _ffn_step(x, w1, w3, w2, g, b, layer):
    n = x.shape[0]
    return pl.pallas_call(
        _ffn_step_kernel,
        grid=(D_FF // FF_TILE,),
        in_specs=[
            _const_spec((n, D_MODEL)),
            pl.BlockSpec((None, D_MODEL, FF_TILE), lambda f: (layer, 0, f)),
            pl.BlockSpec((None, D_MODEL, FF_TILE), lambda f: (layer, 0, f)),
            pl.BlockSpec((None, FF_TILE, D_MODEL), lambda f: (layer, f, 0)),
            _const_spec((1, D_MODEL)),
            _const_spec((1, D_MODEL)),
        ],
        out_specs=_const_spec((n, D_MODEL)),
        out_shape=jax.ShapeDtypeStruct((n, D_MODEL), F32),
        scratch_shapes=[pltpu.VMEM((n, D_MODEL), F32)],
        compiler_params=_params(1),
        name="ffn_step",
    )(x, w1, w3, w2, g, b)


QQI = Q_END + IDX_HEADS * IDX_DIM
KVT_ROWS = 2 * KV_DIM + IDX_DIM


def _rope(xc, c, sa, sb):
    return xc * c + pltpu.roll(xc, 8, 1) * sa + pltpu.roll(xc, LANES - 8, 1) * sb


def _rope_rows(ht, r0, cos, sin, out_ref, o0, n_rows):
    x1, x2 = ht[r0:r0 + 8, :], ht[r0 + 8:r0 + 16, :]
    out_ref[o0:o0 + 8, :] = x1 * cos - x2 * sin
    out_ref[o0 + 8:o0 + 16, :] = x1 * sin + x2 * cos
    out_ref[o0 + 16:o0 + n_rows, :] = ht[r0 + 16:r0 + n_rows, :]


def _odd_proj_kernel(x_ref, wq_ref, wkt_ref, tab_ref, tabt_ref,
                     q_ref, qi_ref, wi_ref, kt_ref, vt_ref, kit_ref, ktb_ref, vtb_ref, kitb_ref, *, precise):
    if precise:
        x = x_ref[...]
        h = _dot3(x, wq_ref[...])
        ht = _dot3(wkt_ref[...], x, dot=_dot_nt)
    else:
        xb = x_ref[...].astype(BF16)
        h = _dot(xb, wq_ref[...])
        ht = _dot_nt(wkt_ref[...], xb)
    c, sa, sb = tab_ref[:, 0:LANES], tab_ref[:, LANES:2 * LANES], tab_ref[:, 2 * LANES:3 * LANES]
    q_scale = HEAD_DIM ** -0.5 if precise else float(HEAD_DIM ** -0.5 * np.log2(np.e))
    for j in range(Q_END // LANES):
        cs = slice(j * LANES, (j + 1) * LANES)
        q_ref[:, cs] = (_rope(h[:, cs], c, sa, sb) * q_scale).astype(q_ref.dtype)
    for j in range(IDX_HEADS * IDX_DIM // LANES):
        cs = slice(j * LANES, (j + 1) * LANES)
        qi_ref[:, cs] = (_rope(h[:, Q_END + j * LANES:Q_END + (j + 1) * LANES], c, sa, sb)
                         * (IDX_DIM ** -0.5)).astype(qi_ref.dtype)
    wi_ref[...] = h[:, QQI:QQI + LANES] * (IDX_HEADS ** -0.5)

    cos, sin = tabt_ref[0:8, :], tabt_ref[8:16, :]
    for hd in range(N_KV_HEADS):
        _rope_rows(ht, hd * HEAD_DIM, cos, sin, kt_ref, hd * HEAD_DIM, HEAD_DIM)
    vt_ref[...] = ht[KV_DIM:2 * KV_DIM, :]
    _rope_rows(ht, 2 * KV_DIM, cos, sin, kit_ref, 0, IDX_DIM)
    ktb_ref[...] = kt_ref[...].astype(BF16)
    vtb_ref[...] = vt_ref[...].astype(BF16)
    kitb_ref[...] = kit_ref[...].astype(BF16)


def _odd_proj(x, wq, wkt, tab, tabt, bsz, t, tm, precise=False):
    nt = t // tm
    tok = lambda i: (i, 0)
    feat = lambda i: (i // nt, 0, i % nt)
    q_dtype = F32 if precise else BF16
    row_widths = [(Q_END, q_dtype), (IDX_HEADS * IDX_DIM, q_dtype), (LANES, F32)]
    feat_rows = [(KV_DIM, F32), (KV_DIM, F32), (IDX_DIM, F32), (KV_DIM, BF16), (KV_DIM, BF16), (IDX_DIM, BF16)]
    return pl.pallas_call(
        functools.partial(_odd_proj_kernel, precise=precise),
        grid=(bsz * nt,),
        in_specs=[
            pl.BlockSpec((tm, D_MODEL), tok),
            _const_spec((D_MODEL, QQI + LANES)),
            _const_spec((KVT_ROWS, D_MODEL)),
            pl.BlockSpec((tm, 3 * LANES), lambda i: (i % nt, 0)),
            pl.BlockSpec((ROPE_DIM, tm), lambda i: (0, i % nt)),
        ],
        out_specs=[pl.BlockSpec((tm, w_), tok) for w_, _ in row_widths]
        + [pl.BlockSpec((None, r_, tm), feat) for r_, _ in feat_rows],
        out_shape=[jax.ShapeDtypeStruct((bsz * t, w_), d_) for w_, d_ in row_widths]
        + [jax.ShapeDtypeStruct((bsz, r_, t), d_) for r_, d_ in feat_rows],
        compiler_params=_params(1),
        name="odd_proj",
    )(x, wq, wkt, tab, tabt)


def _rope_tables(pos):
    half = ROPE_DIM // 2
    inv = ROPE_THETA ** (-jnp.arange(half, dtype=F32) / half)
    ang = pos.astype(F32)[:, None] * inv[None, :]
    cos, sin = jnp.cos(ang), jnp.sin(ang)
    n = pos.shape[0]
    one = jnp.ones((n, HEAD_DIM - ROPE_DIM), F32)
    zero = jnp.zeros((n, HEAD_DIM - ROPE_DIM), F32)
    zh = jnp.zeros((n, half), F32)
    c64 = jnp.concatenate([cos, cos, one], axis=-1)
    sa64 = jnp.concatenate([zh, sin, zero], axis=-1)
    sb64 = jnp.concatenate([-sin, zh, zero], axis=-1)
    tab = jnp.concatenate([c64, c64, sa64, sa64, sb64, sb64], axis=-1)
    tabt = jnp.concatenate([cos.T, sin.T], axis=0)
    return tab, tabt


KEY_BLOCK = 256
COUNT_ROWS = 64


def _lane_fold(a, op):
    parts = [a[:, c * LANES:(c + 1) * LANES] for c in range(a.shape[1] // LANES)]
    while len(parts) > 1:
        parts = [op(parts[i], parts[i + 1]) for i in range(0, len(parts), 2)]
    return parts[0]


def _dsa_prompt_kernel(x_ref, q_ref, qi_ref, wi_ref, kit_ref, kt_ref, vt_ref, wout_ref, g_ref, b_ref,
                       o_ref, key_ref, bias_ref, lbuf, wib, q3, a3, mbuf, obuf, thr_ref, cnt_ref, ext_ref,
                       *, tq, topk):
    t = pl.program_id(1)
    n_blk = (t * tq) // KEY_BLOCK + 1

    def blk(kb):
        return pl.ds(pl.multiple_of(kb * KEY_BLOCK, KEY_BLOCK), KEY_BLOCK)

    def bcast(col):
        return jnp.broadcast_to(col, (tq, LANES))

    def chunks(a):
        return [a[:, c * LANES:(c + 1) * LANES] for c in range(KEY_BLOCK // LANES)]

    row = lax.broadcasted_iota(jnp.int32, (tq, KEY_BLOCK), 0)
    lane = lax.broadcasted_iota(jnp.int32, (tq, KEY_BLOCK), 1)

    for h in range(IDX_HEADS):
        wib[h] = bcast(wi_ref[:, h:h + 1])
    for h in range(N_HEADS):
        q3[h // GROUP, (h % GROUP) * tq:(h % GROUP + 1) * tq, :] = q_ref[:, h * HEAD_DIM:(h + 1) * HEAD_DIM]

    ext_ref[0] = jnp.full((tq, LANES), -jnp.inf, F32)
    ext_ref[1] = jnp.full((tq, LANES), jnp.inf, F32)

    def score_body(kb, carry):
        ki = kit_ref[:, blk(kb)]
        score = [jnp.zeros((tq, LANES), F32)] * (KEY_BLOCK // LANES)
        for h in range(IDX_HEADS):
            d = _dot(qi_ref[:, h * IDX_DIM:(h + 1) * IDX_DIM], ki)
            w = wib[h]
            score = [s_ + w * jnp.maximum(d_, 0.0) for s_, d_ in zip(score, chunks(d))]
        score = jnp.concatenate(score, axis=-1)
        adm = kb * KEY_BLOCK + lane <= t * tq + row
        key_ref[:, blk(kb)] = jnp.where(adm, score, -jnp.inf)
        ext_ref[0] = jnp.maximum(ext_ref[0], _lane_fold(jnp.where(adm, score, -jnp.inf), jnp.maximum))
        ext_ref[1] = jnp.minimum(ext_ref[1], _lane_fold(jnp.where(adm, score, jnp.inf), jnp.minimum))
        return carry

    lax.fori_loop(0, n_blk, score_body, 0)
    row_max = jnp.max(ext_ref[0], axis=-1, keepdims=True)
    row_min = jnp.min(ext_ref[1], axis=-1, keepdims=True)

    def sweep(thr, init, step, finish):
        thr_ref[...] = bcast(thr)
        cnt_ref[...] = jnp.full((tq, LANES), init, F32)

        def body(kb, c_):
            for r in range(tq // COUNT_ROWS):
                rs = slice(r * COUNT_ROWS, (r + 1) * COUNT_ROWS)
                th = thr_ref[rs, :]
                acc = cnt_ref[rs, :]
                for k_ in chunks(key_ref[rs, blk(kb)]):
                    acc = step(acc, k_, th)
                cnt_ref[rs, :] = acc
            return c_

        lax.fori_loop(0, n_blk, body, 0)
        return finish(cnt_ref[...], axis=-1, keepdims=True)

    def count(cmp, thr):
        return sweep(thr, 0.0, lambda acc, k_, th: acc + cmp(k_, th).astype(F32), jnp.sum)

    def min_above(thr):
        return sweep(thr, jnp.inf, lambda acc, k_, th: jnp.minimum(acc, jnp.where(k_ > th, k_, jnp.inf)), jnp.min)

    tau, above = _kth_largest(functools.partial(count, jnp.greater), min_above, row_min, row_max, float(topk))
    taub = bcast(tau)
    need = float(topk) - above
    n_eq = count(jnp.equal, tau)
    crowded = jnp.max((n_eq > need).astype(F32)) > 0.0

    @pl.when(jnp.logical_not(crowded))
    def _():
        def bias_body(kb, carry):
            out = [jnp.where((k_ >= taub) & (k_ > -jnp.inf), 0.0, -jnp.inf)
                   for k_ in chunks(key_ref[:, blk(kb)])]
            bias_ref[:, blk(kb)] = jnp.concatenate(out, axis=-1)
            return carry
        lax.fori_loop(0, n_blk, bias_body, 0)

    @pl.when(crowded)
    def _():
        r_ = lax.broadcasted_iota(jnp.int32, (KEY_BLOCK, KEY_BLOCK), 0)
        c_ = lax.broadcasted_iota(jnp.int32, (KEY_BLOCK, KEY_BLOCK), 1)
        before = jnp.where(r_ < c_, 1.0, 0.0).astype(BF16)
        needb = bcast(need)

        def bias_body(kb, seen):
            ks = chunks(key_ref[:, blk(kb)])
            tie = [jnp.where(k_ == taub, 1.0, 0.0) for k_ in ks]
            rank = chunks(_dot(jnp.concatenate(tie, axis=-1).astype(BF16), before))
            out = []
            for k_, t_, r in zip(ks, tie, rank):
                keep = ((k_ > taub) | ((t_ > 0.0) & (seen + r < needb))) & (k_ > -jnp.inf)
                out.append(jnp.where(keep, 0.0, -jnp.inf))
            bias_ref[:, blk(kb)] = jnp.concatenate(out, axis=-1)
            return seen + bcast(jnp.sum(functools.reduce(jnp.add, tie), axis=-1, keepdims=True))

        lax.fori_loop(0, n_blk, bias_body, jnp.zeros((tq, LANES), F32))

    def group_body(kv, carry):
        qg = q3[kv]
        mbuf[...] = jnp.full(mbuf.shape, -jnp.inf, F32)
        obuf[...] = jnp.zeros(obuf.shape, F32)

        def logit_body(kb, c_):
            logit = _dot(qg, kt_ref[kv, :, blk(kb)])
            bias = bias_ref[:, blk(kb)]
            for j in range(GROUP):
                rows = slice(j * tq, (j + 1) * tq)
                lj = logit[rows, :] + bias
                lbuf[rows, blk(kb)] = lj
                mbuf[rows, :] = jnp.maximum(mbuf[rows, :], _lane_fold(lj, jnp.maximum))
            return c_

        lax.fori_loop(0, n_blk, logit_body, 0)
        mbs = [bcast(jnp.max(mbuf[j * tq:(j + 1) * tq, :], axis=-1, keepdims=True)) for j in range(GROUP)]

        def pv_body(kb, c_):
            pj = []
            for j in range(GROUP):
                rows = slice(j * tq, (j + 1) * tq)
                ps = [jnp.exp2(l_ - mbs[j]) for l_ in chunks(lbuf[rows, blk(kb)])]
                pj.append(jnp.concatenate(ps, axis=-1).astype(BF16))
            v1 = jnp.concatenate([vt_ref[kv, :, blk(kb)], jnp.ones((LANES - HEAD_DIM, KEY_BLOCK), BF16)], axis=0)
            obuf[...] += _dot_nt(jnp.concatenate(pj, axis=0), v1)
            return c_

        lax.fori_loop(0, n_blk, pv_body, 0)
        a3[kv] = (obuf[:, 0:HEAD_DIM] / obuf[:, HEAD_DIM:HEAD_DIM + 1]).astype(BF16)
        return carry

    lax.fori_loop(0, N_KV_HEADS, group_body, 0)

    a = jnp.concatenate([a3[h // GROUP, (h % GROUP) * tq:(h % GROUP + 1) * tq, :] for h in range(N_HEADS)], axis=-1)
    m = _dot(a, wout_ref[...])
    o_ref[...] = _ln(ALPHA * x_ref[...] + m, g_ref[...], b_ref[...])


def _dsa_prompt(x, q, qi, wi, kit, kt, vt, w_out, g, b, bsz, t, layer, tq=256):
    nq = t // tq
    topk = min(TOPK_MAX, t // 4)
    kern = functools.partial(_dsa_prompt_kernel, tq=tq, topk=topk)
    qmap = lambda i, j: (i * nq + j, 0)
    return pl.pallas_call(
        kern,
        grid=(bsz, nq),
        in_specs=[
            pl.BlockSpec((tq, D_MODEL), qmap),
            pl.BlockSpec((tq, Q_END), qmap),
            pl.BlockSpec((tq, IDX_HEADS * IDX_DIM), qmap),
            pl.BlockSpec((tq, LANES), qmap),
            pl.BlockSpec((None, IDX_DIM, t), lambda i, j: (i, 0, 0)),
            pl.BlockSpec((None, N_KV_HEADS, HEAD_DIM, t), lambda i, j: (i, 0, 0, 0)),
            pl.BlockSpec((None, N_KV_HEADS, HEAD_DIM, t), lambda i, j: (i, 0, 0, 0)),
            _layer_spec((Q_END, D_MODEL), layer),
            _const_spec((1, D_MODEL)),
            _const_spec((1, D_MODEL)),
        ],
        out_specs=pl.BlockSpec((tq, D_MODEL), qmap),
        out_shape=jax.ShapeDtypeStruct((bsz * t, D_MODEL), F32),
        scratch_shapes=[
            pltpu.VMEM((tq, t), F32),
            pltpu.VMEM((tq, t), F32),
            pltpu.VMEM((GROUP * tq, t), F32),
            pltpu.VMEM((IDX_HEADS, tq, LANES), F32),
            pltpu.VMEM((N_KV_HEADS, GROUP * tq, HEAD_DIM), BF16),
            pltpu.VMEM((N_KV_HEADS, GROUP * tq, HEAD_DIM), BF16),
            pltpu.VMEM((GROUP * tq, LANES), F32),
            pltpu.VMEM((GROUP * tq, LANES), F32),
            pltpu.VMEM((tq, LANES), F32),
            pltpu.VMEM((tq, LANES), F32),
            pltpu.VMEM((2, tq, LANES), F32),
        ],
        compiler_params=_params(2),
        name="dsa_prompt",
    )(x, q, qi, wi, kit, kt, vt, w_out, g, b)


IDX_PAGES_PER_STEP = 64
KV_PAGES_PER_STEP = 16


def _page_specs(layer, n_rows, pages_per_step):
    def spec(j):
        return pl.BlockSpec((None, None, n_rows, PAGE_SIZE),
                            lambda i, s, pt: (layer, pt[i, s * pages_per_step + j], 0, 0))
    return [spec(j) for j in range(pages_per_step)]


def _dsa_step_score_kernel(pt_ref, qi_ref, wi_ref, kin_ref, *rest):
    pages = rest[:IDX_PAGES_PER_STEP]
    o_ref, on_ref, kbuf_hi, kbuf_lo = rest[IDX_PAGES_PER_STEP:]
    for j, p in enumerate(pages):
        hi, lo = _split(p[...])
        kbuf_hi[:, j * PAGE_SIZE:(j + 1) * PAGE_SIZE] = hi
        kbuf_lo[:, j * PAGE_SIZE:(j + 1) * PAGE_SIZE] = lo
    qi = qi_ref[0]
    wi = wi_ref[0]
    qh, ql = _split(qi)
    d = _dot(qh, kbuf_hi[...]) + (_dot(qh, kbuf_lo[...]) + _dot(ql, kbuf_hi[...]))
    o_ref[0] = jnp.sum(wi * jnp.maximum(d, 0.0), axis=0, keepdims=True)
    dn = jnp.sum(qi * kin_ref[0], axis=-1, keepdims=True)
    sn = jnp.sum(wi * jnp.maximum(dn, 0.0), axis=0, keepdims=True)
    on_ref[0] = jnp.broadcast_to(sn, (1, LANES))


def _dsa_step_score(page_table, qi3, wi3, kin3, cache_kidx_t, layer):
    n, n_pages = page_table.shape
    steps = n_pages // IDX_PAGES_PER_STEP
    step_keys = IDX_PAGES_PER_STEP * PAGE_SIZE
    row = lambda i, s, pt: (i, 0, 0)
    grid_spec = pltpu.PrefetchScalarGridSpec(
        num_scalar_prefetch=1,
        grid=(n, steps),
        in_specs=[
            pl.BlockSpec((1, IDX_HEADS, IDX_DIM), row),
            pl.BlockSpec((1, IDX_HEADS, 1), row),
            pl.BlockSpec((1, 1, IDX_DIM), row),
        ] + _page_specs(layer, IDX_DIM, IDX_PAGES_PER_STEP),
        out_specs=[
            pl.BlockSpec((1, 1, step_keys), lambda i, s, pt: (i, 0, s)),
            pl.BlockSpec((1, 1, LANES), row),
        ],
        scratch_shapes=[pltpu.VMEM((IDX_DIM, step_keys), BF16), pltpu.VMEM((IDX_DIM, step_keys), BF16)],
    )
    return pl.pallas_call(
        _dsa_step_score_kernel,
        grid_spec=grid_spec,
        out_shape=[jax.ShapeDtypeStruct((n, 1, n_pages * PAGE_SIZE), F32),
                   jax.ShapeDtypeStruct((n, 1, LANES), F32)],
        compiler_params=_params(2),
        name="dsa_step_score",
    )(page_table, qi3, wi3, kin3, *([cache_kidx_t] * IDX_PAGES_PER_STEP))


def _dsa_step_select_kernel(sp_ref, sn_ref, bias_ref, biasn_ref, *, topk):
    n, past = sp_ref.shape
    lane = lax.broadcasted_iota(jnp.int32, (n, LANES), 1)
    s_new = sn_ref[:, 0:1]

    def count_gt(x):
        return jnp.sum((sp_ref[...] > x).astype(F32), axis=-1, keepdims=True) + (s_new > x).astype(F32)

    def min_above(x):
        sp = sp_ref[...]
        past_min = jnp.min(jnp.where(sp > x, sp, jnp.inf), axis=-1, keepdims=True)
        return jnp.minimum(past_min, jnp.where(s_new > x, s_new, jnp.inf))

    row_max = jnp.maximum(jnp.max(sp_ref[...], axis=-1, keepdims=True), s_new)
    row_min = jnp.minimum(jnp.min(sp_ref[...], axis=-1, keepdims=True), s_new)
    tau, above = _kth_largest(count_gt, min_above, row_min, row_max, float(topk))
    need = float(topk) - above

    def count_lt(p):
        idx = lax.broadcasted_iota(jnp.int32, (n, past), 1)
        return jnp.sum(((sp_ref[...] == tau) & (idx < p)).astype(F32), axis=-1, keepdims=True)

    lim = _tie_limit(count_lt, need, n, int(past).bit_length())
    sp = sp_ref[...]
    idx = lax.broadcasted_iota(jnp.int32, (n, past), 1)
    keep_p = (sp > tau) | ((sp == tau) & (idx <= lim))
    n_keep = jnp.sum(keep_p.astype(F32), axis=-1, keepdims=True)
    keep_n = (s_new > tau) | ((s_new == tau) & (n_keep < float(topk)))
    bias_ref[...] = jnp.where(keep_p, 0.0, -jnp.inf)
    biasn_ref[...] = jnp.where(keep_n & (lane == 0), 0.0, -jnp.inf)


def _dsa_step_select(sp, sn):
    n, past = sp.shape
    topk = min(TOPK_MAX, (past + 1) // 4)
    return pl.pallas_call(
        functools.partial(_dsa_step_select_kernel, topk=topk),
        grid=(1,),
        in_specs=[_const_spec((n, past)), _const_spec((n, LANES))],
        out_specs=[_const_spec((n, past)), _const_spec((n, LANES))],
        out_shape=[jax.ShapeDtypeStruct((n, past), F32), jax.ShapeDtypeStruct((n, LANES), F32)],
        compiler_params=_params(1),
        name="dsa_step_select",
    )(sp, sn)


def _dsa_step_attend_kernel(pt_ref, bias_ref, biasn_ref, qbd_ref, kn_ref, vn_ref, *rest):
    kp = rest[:KV_PAGES_PER_STEP]
    vp = rest[KV_PAGES_PER_STEP:2 * KV_PAGES_PER_STEP]
    o_ref, kbuf, vbuf, m_ref, l_ref, acc_ref = rest[2 * KV_PAGES_PER_STEP:]
    s = pl.program_id(1)
    n_steps = pl.num_programs(1)
    step_keys = KV_PAGES_PER_STEP * PAGE_SIZE

    @pl.when(s == 0)
    def _():
        m_ref[...] = jnp.full(m_ref.shape, -jnp.inf, F32)
        l_ref[...] = jnp.zeros(l_ref.shape, F32)
        acc_ref[...] = jnp.zeros(acc_ref.shape, F32)

    for j in range(KV_PAGES_PER_STEP):
        kbuf[:, j * PAGE_SIZE:(j + 1) * PAGE_SIZE] = kp[j][...].astype(BF16)
        vbuf[:, j * PAGE_SIZE:(j + 1) * PAGE_SIZE] = vp[j][...].astype(BF16)
    qbd = qbd_ref[0]
    logit = _dot(qbd, kbuf[...]) + bias_ref[0]
    m_new = jnp.maximum(m_ref[...], jnp.max(logit, axis=-1, keepdims=True))
    m_safe = jnp.where(m_new == -jnp.inf, 0.0, m_new)
    scale = jnp.exp(m_ref[...] - m_safe)
    p = jnp.exp(logit - m_safe)
    l_ref[...] = l_ref[...] * scale + jnp.sum(p, axis=-1, keepdims=True)
    acc_ref[...] = acc_ref[...] * scale + _dot_nt(p.astype(BF16), vbuf[...])
    m_ref[...] = m_new

    @pl.when(s == n_steps - 1)
    def _():
        kn = kn_ref[0].astype(BF16).astype(F32)
        vn = vn_ref[0].astype(BF16).astype(F32)
        ln_ = jnp.sum(qbd.astype(F32) * kn, axis=-1, keepdims=True) + biasn_ref[0][:, 0:1]
        m_old = m_ref[...]
        m_new2 = jnp.maximum(m_old, ln_)
        sc = jnp.exp(m_old - m_new2)
        pn = jnp.exp(ln_ - m_new2)
        den = l_ref[...] * sc + pn
        r = (acc_ref[...] * sc + pn.astype(BF16).astype(F32) * vn) / den
        pieces = [r[h:h + 1, (h // GROUP) * HEAD_DIM:(h // GROUP + 1) * HEAD_DIM] for h in range(N_HEADS)]
        o_ref[0] = jnp.concatenate(pieces, axis=-1)


def _dsa_step_attend(page_table, bias3, biasn3, qbd, kn3, vn3, cache_k_t, cache_v_t, layer):
    n, n_pages = page_table.shape
    steps = n_pages // KV_PAGES_PER_STEP
    step_keys = KV_PAGES_PER_STEP * PAGE_SIZE
    page_specs = _page_specs(layer, KV_DIM, KV_PAGES_PER_STEP)
    row = lambda i, s, pt: (i, 0, 0)
    grid_spec = pltpu.PrefetchScalarGridSpec(
        num_scalar_prefetch=1,
        grid=(n, steps),
        in_specs=[
            pl.BlockSpec((1, 1, step_keys), lambda i, s, pt: (i, 0, s)),
            pl.BlockSpec((1, 1, LANES), row),
            pl.BlockSpec((1, N_HEADS, KV_DIM), row),
            pl.BlockSpec((1, 1, KV_DIM), row),
            pl.BlockSpec((1, 1, KV_DIM), row),
        ] + page_specs + page_specs,
        out_specs=pl.BlockSpec((1, 1, Q_END), row),
        scratch_shapes=[
            pltpu.VMEM((KV_DIM, step_keys), BF16), pltpu.VMEM((KV_DIM, step_keys), BF16),
            pltpu.VMEM((N_HEADS, 1), F32), pltpu.VMEM((N_HEADS, 1), F32), pltpu.VMEM((N_HEADS, KV_DIM), F32),
        ],
    )
    return pl.pallas_call(
        _dsa_step_attend_kernel,
        grid_spec=grid_spec,
        out_shape=jax.ShapeDtypeStruct((n, 1, Q_END), F32),
        compiler_params=_params(2),
        name="dsa_step_attend",
    )(page_table, bias3, biasn3, qbd, kn3, vn3,
      *([cache_k_t] * KV_PAGES_PER_STEP), *([cache_v_t] * KV_PAGES_PER_STEP))


def _proj_ln_kernel(a_ref, x_ref, w_ref, g_ref, b_ref, o_ref):
    m = _dot3(a_ref[...], w_ref[...])
    o_ref[...] = _ln(ALPHA * x_ref[...] + m, g_ref[...], b_ref[...])


def _proj_ln(a, x, w, g, b, layer):
    n = x.shape[0]
    c = _const_spec
    in_specs = [c((n, a.shape[1])), c((n, D_MODEL)), _layer_spec(w.shape[1:], layer), c((1, D_MODEL)),
                c((1, D_MODEL))]
    return pl.pallas_call(
        _proj_ln_kernel,
        grid=(1,),
        in_specs=in_specs,
        out_specs=_const_spec((n, D_MODEL)),
        out_shape=jax.ShapeDtypeStruct((n, D_MODEL), F32),
        compiler_params=_params(1),
        name="proj_ln",
    )(a, x, w, g, b)


def _moe_kernel(x_ref, wr_ref, br_ref, w1_ref, w3_ref, w2_ref, g_ref, b_ref, o_ref, acc, comb, xb, *, precise):
    e = pl.program_id(1)
    tm = x_ref.shape[0]
    lane = lax.broadcasted_iota(jnp.int32, (tm, LANES), 1)

    @pl.when(e == 0)
    def _():
        x = x_ref[...]
        logits = _dot3(x, wr_ref[...])
        logits = jnp.where(lane < N_EXPERTS, logits + br_ref[...], -jnp.inf)
        lane_f = lane.astype(F32)
        m1 = jnp.max(logits, axis=-1, keepdims=True)
        i1 = jnp.min(jnp.where(logits == m1, lane_f, float(LANES)), axis=-1, keepdims=True)
        rest = jnp.where(lane_f == i1, -jnp.inf, logits)
        m2 = jnp.max(rest, axis=-1, keepdims=True)
        i2 = jnp.min(jnp.where(rest == m2, lane_f, float(LANES)), axis=-1, keepdims=True)
        e2 = jnp.exp(m2 - m1)
        den = 1.0 + e2
        comb[...] = jnp.where(lane_f == i1, 1.0 / den, 0.0) + jnp.where(lane_f == i2, e2 / den, 0.0)
        xb[...] = x.astype(xb.dtype)
        acc[...] = jnp.zeros(acc.shape, F32)

    c = jnp.sum(jnp.where(lane == e, comb[...], 0.0), axis=-1, keepdims=True)
    if precise:
        h = _silu(_dot3(xb[...], w1_ref[0])) * _dot3(xb[...], w3_ref[0])
        acc[...] += _dot3(c * h, w2_ref[0])
    else:
        h = _silu(_dot(xb[...], w1_ref[0])) * _dot(xb[...], w3_ref[0])
        acc[...] += _dot((c * h).astype(BF16), w2_ref[0])

    @pl.when(e == N_EXPERTS - 1)
    def _():
        o_ref[...] = _ln(ALPHA * x_ref[...] + acc[...], g_ref[...], b_ref[...])


def _moe(x, wr, br, w1, w3, w2, g, b, layer, tm, precise=False):
    m = x.shape[0]
    wspec = pl.BlockSpec((None, 1, D_MODEL, D_MODEL), lambda i, e: (layer, e, 0, 0))
    return pl.pallas_call(
        functools.partial(_moe_kernel, precise=precise),
        grid=(m // tm, N_EXPERTS),
        in_specs=[
            pl.BlockSpec((tm, D_MODEL), lambda i, e: (i, 0)),
            _const_spec((D_MODEL, LANES)),
            _const_spec((1, LANES)),
            wspec, wspec, wspec,
            _const_spec((1, D_MODEL)),
            _const_spec((1, D_MODEL)),
        ],
        out_specs=pl.BlockSpec((tm, D_MODEL), lambda i, e: (i, 0)),
        out_shape=jax.ShapeDtypeStruct((m, D_MODEL), F32),
        scratch_shapes=[pltpu.VMEM((tm, D_MODEL), F32), pltpu.VMEM((tm, LANES), F32),
                        pltpu.VMEM((tm, D_MODEL), F32 if precise else BF16)],
        compiler_params=_params(2),
        name="moe",
    )(x, wr, br, w1, w3, w2, g, b)


def kernel(x_prompt, x_sample, cache_k, cache_v, cache_kidx, state_conv, page_table, ln1_g, ln1_b, ln2_g, ln2_b,
           w_in_even, conv_w, sgu_ln_g, sgu_ln_b, sgu_w, sgu_b, w_out_even, ffn_w1, ffn_w3, ffn_w2, w_in_odd,
           w_out_odd, router_w, router_b, moe_w1, moe_w3, moe_w2):
    bp, tp, _ = x_prompt.shape
    ns = x_sample.shape[0]
    n_pool = cache_k.shape[1]
    past = page_table.shape[1] * PAGE_SIZE

    xp = x_prompt.reshape(bp * tp, D_MODEL)
    xs = x_sample.reshape(ns, D_MODEL)
    tab_p, tabt_p = _rope_tables(jnp.arange(tp))
    tab_s, tabt_s = _rope_tables(jnp.full((ns,), past, jnp.int32))
    ck_t = jnp.transpose(cache_k, (0, 1, 3, 4, 2)).reshape(-1, n_pool, KV_DIM, PAGE_SIZE)
    cv_t = jnp.transpose(cache_v, (0, 1, 3, 4, 2)).reshape(-1, n_pool, KV_DIM, PAGE_SIZE)
    cki_t = jnp.swapaxes(cache_kidx, 2, 3)
    row = lambda a: a.reshape(1, -1)
    head_kv = (jnp.arange(N_HEADS)[:, None] // GROUP == jnp.arange(N_KV_HEADS)[None, :]).astype(BF16)
    w_in_even_b, w_out_even_b, w_out_odd_b = w_in_even.astype(BF16), w_out_even.astype(BF16), w_out_odd.astype(BF16)
    ffn_w1_b, ffn_w3_b, ffn_w2_b = ffn_w1.astype(BF16), ffn_w3.astype(BF16), ffn_w2.astype(BF16)
    moe_w1_b, moe_w3_b, moe_w2_b = moe_w1.astype(BF16), moe_w3.astype(BF16), moe_w2.astype(BF16)

    k_p, v_p, ki_p, conv_p = [], [], [], []
    k_s, v_s, ki_s, conv_s, chunk_s = [], [], [], [], []
    for layer in range(DEPTH):
        i = layer // 2
        g1, b1, g2, b2 = row(ln1_g[layer]), row(ln1_b[layer]), row(ln2_g[layer]), row(ln2_b[layer])
        if layer % 2 == 0:
            sg, sbn = row(sgu_ln_g[i]), row(sgu_ln_b[i])
            bs_b = jnp.broadcast_to(sgu_b[i][:, :, None], (SGU_GROUPS, CHUNK, CHUNK))
            xp, cp = _even_mixer(xp, bp, tp, w_in_even_b, conv_w[i], sg, sbn, sgu_w[i], bs_b, w_out_even_b, g1, b1, i)
            wd = row(jnp.repeat(sgu_w[i][:, 0, 0], CHUNK))
            b0 = row(jnp.repeat(sgu_b[i][:, 0], CHUNK))
            xs, zs, vs = _even_mixer_step(xs, state_conv[i][:, 0], state_conv[i][:, 1], w_in_even, conv_w[i],
                                          sg, sbn, wd, b0, w_out_even, g1, b1, i)
            conv_p.append(cp)
            conv_s.append(jnp.stack([state_conv[i][:, 1], zs], axis=1))
            chunk_s.append(vs.reshape(ns, 1, SGU_CH))
            xp = _ffn(xp, ffn_w1_b, ffn_w3_b, ffn_w2_b, g2, b2, i, tm=1024)
            xs = _ffn_step(xs, ffn_w1, ffn_w3, ffn_w2, g2, b2, i)
        else:
            w = w_in_odd[i]
            wq32 = jnp.concatenate(
                [w[:, :Q_END], w[:, V_END:QI_END],
                 jnp.pad(w[:, QI_END + IDX_DIM:], ((0, 0), (0, LANES - IDX_HEADS)))], axis=1)
            wkt32 = jnp.concatenate([w[:, Q_END:V_END], w[:, QI_END:QI_END + IDX_DIM]], axis=1).T
            wq, wkt = wq32.astype(BF16), wkt32.astype(BF16)
            qp, qip, wip, ktp, vtp, kitp, ktb, vtb, kitb = _odd_proj(xp, wq, wkt, tab_p, tabt_p, bp, tp, tm=512)
            head_major = (bp, N_KV_HEADS, HEAD_DIM, tp)
            xp = _dsa_prompt(xp, qp, qip, wip, kitb, ktb.reshape(head_major), vtb.reshape(head_major), w_out_odd_b,
                             g1, b1, bp, tp, i)
            k_p.append(ktp)
            v_p.append(vtp)
            ki_p.append(kitp)

            qs, qis, wis, kts, vts, kits, _, _, _ = _odd_proj(xs, wq32, wkt32, tab_s, tabt_s, 1, ns, tm=ns,
                                                              precise=True)
            kn, vn, kis = kts[0].T, vts[0].T, kits[0].T
            sp, sn = _dsa_step_score(page_table, qis.reshape(ns, IDX_HEADS, IDX_DIM),
                                     wis[:, :IDX_HEADS].reshape(ns, IDX_HEADS, 1), kis.reshape(ns, 1, IDX_DIM),
                                     cki_t, i)
            qbd = (qs.astype(BF16).reshape(ns, N_HEADS, 1, HEAD_DIM)
                   * head_kv[None, :, :, None]).reshape(ns, N_HEADS, KV_DIM)
            bias_s, biasn_s = _dsa_step_select(sp.reshape(ns, past), sn.reshape(ns, LANES))
            a_s = _dsa_step_attend(page_table, bias_s.reshape(ns, 1, past), biasn_s.reshape(ns, 1, LANES), qbd,
                                   kn.reshape(ns, 1, KV_DIM), vn.reshape(ns, 1, KV_DIM), ck_t, cv_t, i)
            xs = _proj_ln(a_s.reshape(ns, Q_END), xs, w_out_odd, g1, b1, i)
            k_s.append(kn.reshape(ns, 1, N_KV_HEADS, HEAD_DIM))
            v_s.append(vn.reshape(ns, 1, N_KV_HEADS, HEAD_DIM))
            ki_s.append(kis.reshape(ns, 1, IDX_DIM))

            wr = jnp.pad(router_w[i], ((0, 0), (0, LANES - N_EXPERTS)))
            br = jnp.pad(row(router_b[i]), ((0, 0), (0, LANES - N_EXPERTS)))
            xp = _moe(xp, wr, br, moe_w1_b, moe_w3_b, moe_w2_b, g2, b2, i, tm=1024)
            xs = _moe(xs, wr, br, moe_w1, moe_w3, moe_w2, g2, b2, i, tm=ns, precise=True)

    n_odd = len(k_p)
    kv_shape = (n_odd, bp, N_KV_HEADS, HEAD_DIM, tp)
    k_prompt = jnp.stack(k_p).reshape(kv_shape).transpose(0, 1, 4, 2, 3)
    v_prompt = jnp.stack(v_p).reshape(kv_shape).transpose(0, 1, 4, 2, 3)
    kidx_prompt = jnp.stack(ki_p).transpose(0, 1, 3, 2)
    return (xp.reshape(bp, tp, D_MODEL), xs.reshape(ns, 1, D_MODEL),
            k_prompt, v_prompt, kidx_prompt, jnp.stack(conv_p),
            jnp.stack(k_s), jnp.stack(v_s), jnp.stack(ki_s), jnp.stack(conv_s), jnp.stack(chunk_s))
```
